```python
import math, functools
import jax, jax.numpy as jnp
from jax import lax
import numpy as np

D_MODEL = 4096
BATCH = 4
SEQ = 2048
DEPTH = 1
DEC_BATCH = 32
DEC_SEQ = 8
PAST_LEN = 8192
PAGE_SIZE = 128

N_HEADS = 16
N_KV_HEADS = 8
GROUP = N_HEADS // N_KV_HEADS
HEAD_DIM = D_MODEL // (2 * N_HEADS)
V_DIM = 2 * HEAD_DIM
Q_W = N_HEADS * 2 * HEAD_DIM
K_W = N_KV_HEADS * 2 * HEAD_DIM
V_W = N_KV_HEADS * V_DIM
ATTN_W = N_HEADS * V_DIM
SGU_W = D_MODEL
N_SGU_GROUPS = 8
SGU_GROUP_W = SGU_W // N_SGU_GROUPS
CHUNK = 128
D_FF = 11008
CONV_W = 3
Q_BLOCK = 128
SPLITS = (Q_W, Q_W + K_W, Q_W + K_W + V_W, Q_W + K_W + V_W + 2 * SGU_W)
IN_W = SPLITS[-1] + 2 * D_MODEL
SCALE = 1.0 / math.sqrt(HEAD_DIM)
NEG = -1e30
EPS = 1e-6

kernel_name = "hybrid_diffattn_chunkgmlp_convffn_step"


def _rmsnorm(x, g):
    xf = x.astype(jnp.float32)
    y = xf * lax.rsqrt(jnp.mean(xf * xf, axis=-1, keepdims=True) + EPS)
    return (y * g.astype(jnp.float32)).astype(x.dtype)


def _lambda(lq1, lk1, lq2, lk2, lambda_init):
    f = lambda a: a.astype(jnp.float32)
    return jnp.exp(jnp.sum(f(lq1) * f(lk1))) - jnp.exp(jnp.sum(f(lq2) * f(lk2))) + lambda_init


def _attn_prompt(q, k, v, lam):
    b, s = q.shape[:2]
    nb = s // Q_BLOCK
    qs = jnp.moveaxis(q.astype(jnp.float32).reshape(b, nb, Q_BLOCK, N_KV_HEADS, GROUP, 2, HEAD_DIM), 1, 0)
    kf = k.astype(jnp.float32)
    vf = v.astype(jnp.float32)
    k_pos = jnp.arange(s)

    def block(args):
        qb, i = args
        sc = jnp.einsum('bqkgcd,bskcd->bkgcqs', qb, kf) * SCALE
        q_pos = i * Q_BLOCK + jnp.arange(Q_BLOCK)
        mask = k_pos[None, :] <= q_pos[:, None]
        p = jax.nn.softmax(jnp.where(mask, sc, NEG), axis=-1)
        a = p[:, :, :, 0] - lam * p[:, :, :, 1]
        return jnp.einsum('bkgqs,bskv->bqkgv', a, vf)

    o = lax.map(block, (qs, jnp.arange(nb)))
    return jnp.moveaxis(o, 0, 1).reshape(b, s, N_HEADS, V_DIM)


def _online_update(carry, sc, vb):
    m, l, acc = carry
    m_new = jnp.maximum(m, jnp.max(sc, axis=-1))
    alpha = jnp.exp(m - m_new)
    p = jnp.exp(sc - m_new[..., None])
    l = l * alpha + jnp.sum(p, axis=-1)
    acc = acc * alpha[..., None] + jnp.einsum('bkgcqs,bskv->bkgcqv', p, vb)
    return (m_new, l, acc)


def _attn_sample(q, k, v, lam, cache_k, cache_v, page_table, layer):
    b, t = q.shape[:2]
    qf = q.astype(jnp.float32)
    stat = (b, N_KV_HEADS, GROUP, 2, t)
    carry = (jnp.full(stat, NEG, jnp.float32), jnp.zeros(stat, jnp.float32),
             jnp.zeros(stat + (V_DIM,), jnp.float32))

    def page_step(carry, pt):
        kp = cache_k[layer, pt].astype(jnp.float32)
        vp = cache_v[layer, pt].astype(jnp.float32)
        sc = jnp.einsum('bqkgcd,bskcd->bkgcqs', qf, kp) * SCALE
        return _online_update(carry, sc, vp), None

    carry, _ = lax.scan(page_step, carry, page_table.T)
    sc = jnp.einsum('bqkgcd,bskcd->bkgcqs', qf, k.astype(jnp.float32)) * SCALE
    mask = jnp.tril(jnp.ones((t, t), dtype=bool))
    m, l, acc = _online_update(carry, jnp.where(mask, sc, NEG), v.astype(jnp.float32))
    o = acc / l[..., None]
    o = o[:, :, :, 0] - lam * o[:, :, :, 1]
    return o.transpose(0, 3, 1, 2, 4).reshape(b, t, N_HEADS, V_DIM)


def _sgu(uv, g_sgu, w_spatial, b_spatial, t):
    b, s = uv.shape[:2]
    u, vv = jnp.split(jax.nn.gelu(uv, approximate=False), 2, axis=-1)
    vv = _rmsnorm(vv, g_sgu)
    vc = vv.reshape(b, s // t, t, N_SGU_GROUPS, SGU_GROUP_W)
    ws = jnp.tril(w_spatial[:, :t, :t])
    mixed = jnp.einsum('gts,bnsgc->bntgc', ws, vc) + b_spatial[:, :t].T[:, :, None]
    return u * mixed.reshape(b, s, SGU_W), vv


def _layer(x, attend, t_chunk, conv_prev, lam, lambda_init, g_attn, w_in, g_subln, g_sgu, w_spatial,
           b_spatial, w_o_a, w_o_b, w_out, g_ffn, w_gate, conv_w, conv_b, w_up, w_down):
    b, s, _ = x.shape
    h = _rmsnorm(x, g_attn)
    q, k, v, uv, gates = jnp.split(h @ w_in, SPLITS, axis=-1)
    q = q.reshape(b, s, N_KV_HEADS, GROUP, 2, HEAD_DIM)
    k = k.reshape(b, s, N_KV_HEADS, 2, HEAD_DIM)
    v = v.reshape(b, s, N_KV_HEADS, V_DIM)
    o = attend(q, k, v, lam)
    ya = (_rmsnorm(o, g_subln) * (1.0 - lambda_init)).astype(x.dtype).reshape(b, s, ATTN_W)
    yb, v_sgu = _sgu(uv, g_sgu, w_spatial, b_spatial, t_chunk)
    g_a, g_b = jnp.split(jax.nn.sigmoid(gates), 2, axis=-1)
    x = x + (g_a * (ya @ w_o_a) + g_b * (yb @ w_o_b)) @ w_out
    h2 = _rmsnorm(x, g_ffn)
    a = h2 @ w_gate
    padded = jnp.concatenate([conv_prev.astype(a.dtype), a], axis=1)
    c = conv_b + sum(conv_w[j] * padded[:, j:j + s] for j in range(CONV_W))
    x = x + (jax.nn.silu(c) * (h2 @ w_up)) @ w_down
    return x, k, v, v_sgu, padded[:, s:]


def setup_inputs(seed: int = 0) -> dict:
    key = jax.random.key(seed)
    ks = jax.random.split(key, 26)
    n_pages = PAST_LEN // PAGE_SIZE
    n_used = DEC_BATCH * n_pages
    n_pool = n_used + n_used // 4
    f32 = jnp.float32

    def nrm(k, shape, scale):
        return jax.random.normal(k, shape, f32) * scale

    return {
        "x_prompt": nrm(ks[0], (BATCH, SEQ, D_MODEL), 1.0),
        "x_sample": nrm(ks[1], (DEC_BATCH, DEC_SEQ, D_MODEL), 1.0),
        "cache_k": nrm(ks[2], (DEPTH, n_pool, PAGE_SIZE, N_KV_HEADS, 2, HEAD_DIM), 1.0),
        "cache_v": nrm(ks[3], (DEPTH, n_pool, PAGE_SIZE, N_KV_HEADS, V_DIM), 1.0),
        "state_conv": nrm(ks[4], (DEPTH, DEC_BATCH, CONV_W - 1, D_FF), 1.0),
        "page_table": jax.random.permutation(ks[5], n_pool)[:n_used].reshape(DEC_BATCH, n_pages).astype(jnp.int32),
        "g_attn": 1.0 + nrm(ks[6], (DEPTH, D_MODEL), 0.02),
        "w_in": nrm(ks[7], (DEPTH, D_MODEL, IN_W), D_MODEL ** -0.5),
        "lam_q1": nrm(ks[8], (DEPTH, HEAD_DIM), 0.1),
        "lam_k1": nrm(ks[9], (DEPTH, HEAD_DIM), 0.1),
        "lam_q2": nrm(ks[10], (DEPTH, HEAD_DIM), 0.1),
        "lam_k2": nrm(ks[11], (DEPTH, HEAD_DIM), 0.1),
        "g_subln": 1.0 + nrm(ks[12], (DEPTH, V_DIM), 0.02),
        "g_sgu": 1.0 + nrm(ks[13], (DEPTH, SGU_W), 0.02),
        "w_spatial": nrm(ks[14], (DEPTH, N_SGU_GROUPS, CHUNK, CHUNK), CHUNK ** -0.5),
        "b_spatial": 1.0 + nrm(ks[15], (DEPTH, N_SGU_GROUPS, CHUNK), 0.1),
        "w_o_a": nrm(ks[16], (DEPTH, ATTN_W, D_MODEL), ATTN_W ** -0.5),
        "w_o_b": nrm(ks[17], (DEPTH, SGU_W, D_MODEL), SGU_W ** -0.5),
        "w_out": nrm(ks[18], (DEPTH, D_MODEL, D_MODEL), D_MODEL ** -0.5),
        "g_ffn": 1.0 + nrm(ks[19], (DEPTH, D_MODEL), 0.02),
        "w_gate": nrm(ks[20], (DEPTH, D_MODEL, D_FF), D_MODEL ** -0.5),
        "conv_w": nrm(ks[21], (DEPTH, CONV_W, D_FF), 0.5),
        "conv_b": nrm(ks[22], (DEPTH, D_FF), 0.02),
        "w_up": nrm(ks[23], (DEPTH, D_MODEL, D_FF), D_MODEL ** -0.5),
        "w_down": nrm(ks[24], (DEPTH, D_FF, D_MODEL), D_FF ** -0.5),
        "g_final": 1.0 + nrm(ks[25], (D_MODEL,), 0.02),
    }


def reference(x_prompt, x_sample, cache_k, cache_v, state_conv, page_table, g_attn, w_in, lam_q1, lam_k1,
              lam_q2, lam_k2, g_subln, g_sgu, w_spatial, b_spatial, w_o_a, w_o_b, w_out, g_ffn, w_gate,
              conv_w, conv_b, w_up, w_down, g_final):
    yp, ys = x_prompt, x_sample
    k_p, v_p, k_s, v_s, s_p, s_s, c_p, c_s = [], [], [], [], [], [], [], []
    conv_zero = jnp.zeros((x_prompt.shape[0], CONV_W - 1, D_FF), x_prompt.dtype)
    for i in range(DEPTH):
        lambda_init = 0.8 - 0.6 * math.exp(-0.3 * i)
        lam = _lambda(lam_q1[i], lam_k1[i], lam_q2[i], lam_k2[i], lambda_init)
        w = (g_attn[i], w_in[i], g_subln[i], g_sgu[i], w_spatial[i], b_spatial[i], w_o_a[i], w_o_b[i],
             w_out[i], g_ffn[i], w_gate[i], conv_w[i], conv_b[i], w_up[i], w_down[i])
        yp, kp, vp, sp, cp = _layer(yp, _attn_prompt, CHUNK, conv_zero, lam, lambda_init, *w)
        attend_s = functools.partial(_attn_sample, cache_k=cache_k, cache_v=cache_v,
                                     page_table=page_table, layer=i)
        ys, kss, vss, ss, cs = _layer(ys, attend_s, x_sample.shape[1], state_conv[i], lam, lambda_init, *w)
        k_p.append(kp); v_p.append(vp); s_p.append(sp[:, sp.shape[1] - CHUNK:]); c_p.append(cp)
        k_s.append(kss); v_s.append(vss); s_s.append(ss); c_s.append(cs)
    y_prompt = _rmsnorm(yp, g_final)
    y_sample = _rmsnorm(ys, g_final)
    return (y_prompt, y_sample, jnp.stack(k_p), jnp.stack(v_p), jnp.stack(k_s), jnp.stack(v_s),
            jnp.stack(s_p), jnp.stack(s_s), jnp.stack(c_p), jnp.stack(c_s))
```

```python
import functools
import math

import jax
import jax.numpy as jnp
from jax import lax
from jax.experimental import pallas as pl
from jax.experimental.pallas import tpu as pltpu

D_MODEL = 4096
BATCH = 4
SEQ = 2048
DEC_BATCH = 32
DEC_SEQ = 8
PAST_LEN = 8192
PAGE_SIZE = 128
N_HEADS = 16
N_KV_HEADS = 8
GROUP = N_HEADS // N_KV_HEADS
HEAD_DIM = D_MODEL // (2 * N_HEADS)
V_DIM = 2 * HEAD_DIM
Q_W = N_HEADS * 2 * HEAD_DIM
K_W = N_KV_HEADS * 2 * HEAD_DIM
V_W = N_KV_HEADS * V_DIM
SGU_W = D_MODEL
N_SGU_GROUPS = 8
SGU_GROUP_W = SGU_W // N_SGU_GROUPS
CHUNK = 128
D_FF = 11008
CONV_W = 3
SCALE = 1.0 / math.sqrt(HEAD_DIM)
NEG = -1e30
EPS = 1e-6
LAMBDA_INIT = 0.8 - 0.6 * math.exp(-0.3 * 0)

ROWS_P = BATCH * SEQ
ROWS_S = DEC_BATCH * DEC_SEQ
N_PAGES = PAST_LEN // PAGE_SIZE

V7X_VMEM_BYTES = 64 * 1024 * 1024
VMEM_CAP_BYTES = 58 * 1024 * 1024

BM = 1024
N_MP = ROWS_P // BM

F32 = jnp.float32
BF16 = jnp.bfloat16
_NT = (((1,), (1,)), ((), ()))


def _nbytes(shape, dtype):
    return math.prod(shape) * jnp.dtype(dtype).itemsize


def _params(sem, vmem_bytes):
    return pltpu.CompilerParams(dimension_semantics=sem,
                                vmem_limit_bytes=int(min(vmem_bytes, VMEM_CAP_BYTES)))


def _rms_kernel(x_ref, g_ref, o_ref):
    x = x_ref[...].astype(F32)
    ms = jnp.mean(x * x, axis=-1, keepdims=True)
    o_ref[...] = (x * lax.rsqrt(ms + EPS) * g_ref[...]).astype(o_ref.dtype)


def _rmsnorm_rows(x, g, out_dtype, name):
    n, d = x.shape
    rows = min(n, 256)
    vmem = 2 * (_nbytes((rows, d), x.dtype) + _nbytes((rows, d), out_dtype)) + 4 * _nbytes((rows, d), F32)
    return pl.pallas_call(
        _rms_kernel,
        grid=(n // rows,),
        in_specs=[pl.BlockSpec((rows, d), lambda i: (i, 0)), pl.BlockSpec((1, d), lambda i: (0, 0))],
        out_specs=pl.BlockSpec((rows, d), lambda i: (i, 0)),
        out_shape=jax.ShapeDtypeStruct((n, d), out_dtype),
        compiler_params=_params(("arbitrary",), vmem),
        name=name,
    )(x, g.reshape(1, d))


def _cast_weight(w_ref, wb_ref):
    k_dim = w_ref.shape[0]
    chunk = 512

    def body(i, carry):
        r = pl.multiple_of(i * chunk, chunk)
        wb_ref[pl.ds(r, chunk), :] = w_ref[pl.ds(r, chunk), :].astype(BF16)
        return carry

    lax.fori_loop(0, k_dim // chunk, body, 0)


def _mm_kernel(*refs, n_w, n_ex, n_out, epilogue):
    it = iter(refs)
    xp_ref, xs_ref = next(it), next(it)
    w_refs = [next(it) for _ in range(n_w)]
    exp_refs = [next(it) for _ in range(n_ex)]
    exs_refs = [next(it) for _ in range(n_ex)]
    outp_refs = [next(it) for _ in range(n_out)]
    outs_refs = [next(it) for _ in range(n_out)]
    wb_refs = [next(it) for _ in range(n_w)]
    m = pl.program_id(1)

    @pl.when(m == 0)
    def _():
        for w_ref, wb_ref in zip(w_refs, wb_refs):
            _cast_weight(w_ref, wb_ref)

    def run(x_ref, ex_refs, out_refs):
        x = x_ref[...].astype(BF16)
        accs = [jnp.dot(x, wb[...], preferred_element_type=F32) for wb in wb_refs]
        vals = epilogue(accs, [r[...] for r in ex_refs])
        for o_ref, v in zip(out_refs, vals):
            o_ref[...] = v.astype(o_ref.dtype)

    @pl.when(m < N_MP)
    def _():
        run(xp_ref, exp_refs, outp_refs)

    @pl.when(m == N_MP)
    def _():
        run(xs_ref, exs_refs, outs_refs)


def _fused_matmul(xp, xs, ws, w_off, n_cols, bn, epilogue, out_dtypes, extras=(), ex_off=(), name="mm"):
    k_dim = xp.shape[1]
    n_w, n_ex, n_out = len(ws), len(extras), len(out_dtypes)
    pm = lambda n, m: (jnp.minimum(m, N_MP - 1), 0)
    in_specs = [pl.BlockSpec((BM, k_dim), pm), pl.BlockSpec((ROWS_S, k_dim), lambda n, m: (0, 0))]
    for off in w_off:
        in_specs.append(pl.BlockSpec((k_dim, bn), lambda n, m, off=off: (0, n + off)))
    for off in ex_off:
        in_specs.append(pl.BlockSpec((BM, bn), lambda n, m, off=off: (jnp.minimum(m, N_MP - 1), n + off)))
    for off in ex_off:
        in_specs.append(pl.BlockSpec((ROWS_S, bn), lambda n, m, off=off: (0, n + off)))
    out_specs = ([pl.BlockSpec((BM, bn), lambda n, m: (jnp.minimum(m, N_MP - 1), n))] * n_out
                 + [pl.BlockSpec((ROWS_S, bn), lambda n, m: (0, n))] * n_out)
    out_shape = ([jax.ShapeDtypeStruct((ROWS_P, n_cols), dt) for dt in out_dtypes]
                 + [jax.ShapeDtypeStruct((ROWS_S, n_cols), dt) for dt in out_dtypes])
    vmem = (2 * (_nbytes((BM, k_dim), xp.dtype) + _nbytes((ROWS_S, k_dim), xs.dtype))
            + n_w * (2 * _nbytes((k_dim, bn), F32) + _nbytes((k_dim, bn), BF16))
            + 2 * (n_ex + n_out) * _nbytes((BM + ROWS_S, bn), F32)
            + (n_w + 2) * _nbytes((BM, bn), F32))
    args = [xp, xs] + list(ws) + [e[0] for e in extras] + [e[1] for e in extras]
    res = pl.pallas_call(
        functools.partial(_mm_kernel, n_w=n_w, n_ex=n_ex, n_out=n_out, epilogue=epilogue),
        grid=(n_cols // bn, N_MP + 1),
        in_specs=in_specs,
        out_specs=out_specs,
        out_shape=out_shape,
        scratch_shapes=[pltpu.VMEM((k_dim, bn), BF16) for _ in range(n_w)],
        compiler_params=_params(("arbitrary", "arbitrary"), vmem),
        name=name,
    )(*args)
    return res[:n_out], res[n_out:]


def _gelu_exact(x):
    return 0.5 * x * (1.0 + lax.erf(x * math.sqrt(0.5)))


def _lambda_value(lv):
    a = jnp.sum(lv[0:1] * lv[1:2], axis=-1, keepdims=True)
    b = jnp.sum(lv[2:3] * lv[3:4], axis=-1, keepdims=True)
    return jnp.exp(a) - jnp.exp(b) + LAMBDA_INIT


def _head_norm(o, g):
    ms = jnp.mean(o * o, axis=-1, keepdims=True)
    return o * lax.rsqrt(ms + EPS) * g * (1.0 - LAMBDA_INIT)


def _attn_prompt_kernel(q_ref, k_ref, v_ref, lam_ref, g_ref, o_ref, kb, vb, m_s, l_s, acc_s, *, bq):
    qi = pl.program_id(2)

    @pl.when(qi == 0)
    def _():
        def body(i, carry):
            r = pl.multiple_of(i * 256, 256)
            kb[pl.ds(r, 256), :] = k_ref[pl.ds(r, 256), :].astype(BF16)
            vb[pl.ds(r, 256), :] = v_ref[pl.ds(r, 256), :].astype(BF16)
            return carry
        lax.fori_loop(0, SEQ // 256, body, 0)

    q = q_ref[...]
    qc = [jnp.concatenate([q[:, (g * 2 + c) * HEAD_DIM:(g * 2 + c + 1) * HEAD_DIM] for g in range(GROUP)], axis=0)
          for c in range(2)]
    m_s[...] = jnp.full(m_s.shape, NEG, F32)
    l_s[...] = jnp.zeros(l_s.shape, F32)
    acc_s[...] = jnp.zeros(acc_s.shape, F32)

    def step(j, masked):
        r = pl.multiple_of(j * bq, bq)
        kj = kb[pl.ds(r, bq), :]
        vj = vb[pl.ds(r, bq), :]
        for c in range(2):
            s = lax.dot_general(qc[c], kj[:, c * HEAD_DIM:(c + 1) * HEAD_DIM], _NT, preferred_element_type=F32)
            if masked:
                row = lax.broadcasted_iota(jnp.int32, s.shape, 0) & (bq - 1)
                col = lax.broadcasted_iota(jnp.int32, s.shape, 1)
                s = jnp.where(col <= row, s, NEG)
            m_prev = m_s[c]
            m_new = jnp.maximum(m_prev, jnp.max(s, axis=-1, keepdims=True))
            alpha = jnp.exp(m_prev - m_new)
            p = jnp.exp(s - m_new)
            l_s[c] = alpha * l_s[c] + jnp.sum(p, axis=-1, keepdims=True)
            acc_s[c] = alpha * acc_s[c] + jnp.dot(p.astype(BF16), vj, preferred_element_type=F32)
            m_s[c] = m_new

    def loop_body(j, carry):
        step(j, False)
        return carry

    lax.fori_loop(0, qi, loop_body, 0)
    step(qi, True)

    lam = _lambda_value(lam_ref[...])
    for g in range(GROUP):
        rows = slice(g * bq, (g + 1) * bq)
        o = acc_s[0, rows] / l_s[0, rows] - lam * (acc_s[1, rows] / l_s[1, rows])
        o_ref[:, g * V_DIM:(g + 1) * V_DIM] = _head_norm(o, g_ref[...]).astype(o_ref.dtype)


def _attn_prompt(q, k, v, lamv, g_subln):
    bq = 256
    nq = SEQ // bq
    qw = GROUP * 2 * HEAD_DIM
    vmem = (2 * (_nbytes((bq, qw), BF16) * 2 + 2 * _nbytes((SEQ, V_DIM), F32))
            + 2 * _nbytes((SEQ, V_DIM), BF16) + 2 * _nbytes((2, GROUP * bq, V_DIM + 256), F32)
            + 8 * _nbytes((GROUP * bq, bq), F32))
    return pl.pallas_call(
        functools.partial(_attn_prompt_kernel, bq=bq),
        grid=(BATCH, N_KV_HEADS, nq),
        in_specs=[
            pl.BlockSpec((bq, qw), lambda b, h, i: (b * nq + i, h)),
            pl.BlockSpec((SEQ, 2 * HEAD_DIM), lambda b, h, i: (b, h)),
            pl.BlockSpec((SEQ, V_DIM), lambda b, h, i: (b, h)),
            pl.BlockSpec((4, HEAD_DIM), lambda b, h, i: (0, 0)),
            pl.BlockSpec((1, V_DIM), lambda b, h, i: (0, 0)),
        ],
        out_specs=pl.BlockSpec((bq, GROUP * V_DIM), lambda b, h, i: (b * nq + i, h)),
        out_shape=jax.ShapeDtypeStruct((ROWS_P, N_HEADS * V_DIM), BF16),
        scratch_shapes=[
            pltpu.VMEM((SEQ, 2 * HEAD_DIM), BF16), pltpu.VMEM((SEQ, V_DIM), BF16),
            pltpu.VMEM((2, GROUP * bq, 1), F32), pltpu.VMEM((2, GROUP * bq, 1), F32),
            pltpu.VMEM((2, GROUP * bq, V_DIM), F32),
        ],
        compiler_params=_params(("arbitrary", "arbitrary", "arbitrary"), vmem),
        name="attn_prompt",
    )(q, k, v, lamv, g_subln)


PAGES_PER_STEP = 4
QROWS = 2 * GROUP * DEC_SEQ


def _attn_sample_kernel(pt_ref, qbd_ref, knew_ref, vnew_ref, lam_ref, g_ref, *rest):
    del pt_ref
    kp_refs = rest[:PAGES_PER_STEP]
    vp_refs = rest[PAGES_PER_STEP:3 * PAGES_PER_STEP]
    o_ref, m_s, l_s, acc_s = rest[3 * PAGES_PER_STEP:]
    j = pl.program_id(1)

    @pl.when(j == 0)
    def _():
        m_s[...] = jnp.full(m_s.shape, NEG, F32)
        l_s[...] = jnp.zeros(l_s.shape, F32)
        acc_s[...] = jnp.zeros(acc_s.shape, F32)

    def scores(k_tiles):
        cols = []
        for kt in k_tiles:
            rows = [lax.dot_general(qbd_ref[0, h], kt(h), _NT, preferred_element_type=F32)
                    for h in range(N_KV_HEADS)]
            cols.append(jnp.concatenate(rows, axis=0))
        return cols[0] if len(cols) == 1 else jnp.concatenate(cols, axis=1)

    def update(s, v_tiles):
        m_prev = m_s[...]
        m_new = jnp.maximum(m_prev, jnp.max(s, axis=-1, keepdims=True))
        alpha = jnp.exp(m_prev - m_new)
        p = jnp.exp(s - m_new)
        l_s[...] = alpha * l_s[...] + jnp.sum(p, axis=-1, keepdims=True)
        pb = p.astype(BF16)
        pv = []
        for h in range(N_KV_HEADS):
            acc = None
            for i, vt in enumerate(v_tiles):
                d = jnp.dot(pb[h * QROWS:(h + 1) * QROWS, i * PAGE_SIZE:(i + 1) * PAGE_SIZE], vt(h),
                            preferred_element_type=F32)
                acc = d if acc is None else acc + d
            pv.append(acc)
        acc_s[...] = alpha * acc_s[...] + jnp.concatenate(pv, axis=0)
        m_s[...] = m_new

    def k_page(ref):
        def get(h):
            maps = [ref[pl.ds(h * 2 + c, PAGE_SIZE, stride=2 * N_KV_HEADS), :] for c in range(2)]
            return jnp.concatenate(maps, axis=1).astype(BF16)
        return get

    def v_page(ref_lo, ref_hi):
        def get(h):
            halves = [r[pl.ds(h, PAGE_SIZE, stride=N_KV_HEADS), :] for r in (ref_lo, ref_hi)]
            return jnp.concatenate(halves, axis=1).astype(BF16)
        return get

    update(scores([k_page(r) for r in kp_refs]),
           [v_page(vp_refs[2 * p], vp_refs[2 * p + 1]) for p in range(PAGES_PER_STEP)])

    @pl.when(j == pl.num_programs(1) - 1)
    def _():
        pad = jnp.zeros((PAGE_SIZE - DEC_SEQ, K_W), F32)
        kn = jnp.concatenate([knew_ref[...], pad], axis=0).astype(BF16)
        vn = jnp.concatenate([vnew_ref[...], pad], axis=0).astype(BF16)
        s = scores([lambda h: kn[:, h * 2 * HEAD_DIM:(h + 1) * 2 * HEAD_DIM]])
        t = lax.broadcasted_iota(jnp.int32, s.shape, 0) & (DEC_SEQ - 1)
        col = lax.broadcasted_iota(jnp.int32, s.shape, 1)
        update(jnp.where(col <= t, s, NEG), [lambda h: vn[:, h * V_DIM:(h + 1) * V_DIM]])

        lam = _lambda_value(lam_ref[...])
        o = acc_s[...] / l_s[...]
        half = GROUP * DEC_SEQ
        for h in range(N_KV_HEADS):
            d = o[h * QROWS:h * QROWS + half] - lam * o[h * QROWS + half:(h + 1) * QROWS]
            y = _head_norm(d, g_ref[...])
            for g in range(GROUP):
                c0 = (h * GROUP + g) * V_DIM
                o_ref[:, c0:c0 + V_DIM] = y[g * DEC_SEQ:(g + 1) * DEC_SEQ]


def _attn_sample(q_s, k_s, v_s, cache_k, cache_v, page_table, lamv, g_subln):
    q6 = q_s.reshape(DEC_BATCH, DEC_SEQ, N_KV_HEADS, GROUP, 2, HEAD_DIM).transpose(0, 2, 4, 3, 1, 5)
    zero = jnp.zeros_like(q6[:, :, 0])
    top = jnp.concatenate([q6[:, :, 0], zero], axis=-1).reshape(DEC_BATCH, N_KV_HEADS, QROWS // 2, 2 * HEAD_DIM)
    bot = jnp.concatenate([zero, q6[:, :, 1]], axis=-1).reshape(DEC_BATCH, N_KV_HEADS, QROWS // 2, 2 * HEAD_DIM)
    qbd = jnp.concatenate([top, bot], axis=2)
    n_pool = cache_k.shape[1]
    ck = cache_k.reshape(n_pool, PAGE_SIZE * N_KV_HEADS * 2, HEAD_DIM)
    cv = cache_v.reshape(n_pool, PAGE_SIZE * N_KV_HEADS, V_DIM)
    pps = PAGES_PER_STEP

    def page_spec(p, arr, half=0):
        return pl.BlockSpec((None, arr.shape[1], HEAD_DIM), lambda b, j, pt, p=p: (pt[b, j * pps + p], 0, half))

    in_specs = [
        pl.BlockSpec((1, N_KV_HEADS, QROWS, 2 * HEAD_DIM), lambda b, j, pt: (b, 0, 0, 0)),
        pl.BlockSpec((DEC_SEQ, K_W), lambda b, j, pt: (b, 0)),
        pl.BlockSpec((DEC_SEQ, V_W), lambda b, j, pt: (b, 0)),
        pl.BlockSpec((4, HEAD_DIM), lambda b, j, pt: (0, 0)),
        pl.BlockSpec((1, V_DIM), lambda b, j, pt: (0, 0)),
    ] + [page_spec(p, ck) for p in range(pps)] + [page_spec(p, cv, half) for p in range(pps) for half in range(2)]
    rows = N_KV_HEADS * QROWS
    vmem = (2 * 2 * pps * _nbytes((PAGE_SIZE, K_W), F32) + 2 * pps * _nbytes((PAGE_SIZE, K_W), BF16)
            + 8 * _nbytes((rows, pps * PAGE_SIZE), F32) + 4 * _nbytes((rows, V_DIM), F32)
            + 4 * _nbytes((DEC_SEQ, Q_W), F32))
    return pl.pallas_call(
        _attn_sample_kernel,
        grid_spec=pltpu.PrefetchScalarGridSpec(
            num_scalar_prefetch=1,
            grid=(DEC_BATCH, N_PAGES // pps),
            in_specs=in_specs,
            out_specs=pl.BlockSpec((DEC_SEQ, N_HEADS * V_DIM), lambda b, j, pt: (b, 0)),
            scratch_shapes=[pltpu.VMEM((rows, 1), F32), pltpu.VMEM((rows, 1), F32),
                            pltpu.VMEM((rows, V_DIM), F32)],
        ),
        out_shape=jax.ShapeDtypeStruct((ROWS_S, N_HEADS * V_DIM), F32),
        compiler_params=_params(("arbitrary", "arbitrary"), vmem),
        name="attn_sample",
    )(page_table, qbd, k_s, v_s, lamv, g_subln, *([ck] * pps), *([cv] * (2 * pps)))


def _sgu_kernel(u_ref, v_ref, w_ref, b_ref, g_ref, yb_ref, vn_ref, *, t, last_only):
    v = v_ref[...]
    ms = jnp.mean(v * v, axis=-1, keepdims=True)
    vn = v * lax.rsqrt(ms + EPS) * g_ref[...]
    if last_only:
        @pl.when(pl.program_id(1) == pl.num_programs(1) - 1)
        def _():
            vn_ref[...] = vn
    else:
        vn_ref[...] = vn
    r = v.shape[0]
    row = lax.broadcasted_iota(jnp.int32, (r, r), 0)
    col = lax.broadcasted_iota(jnp.int32, (r, r), 1)
    keep = (col <= row) & (col >= row - (row & (t - 1)))
    for g in range(N_SGU_GROUPS):
        cs = slice(g * SGU_GROUP_W, (g + 1) * SGU_GROUP_W)
        w = jnp.where(keep, w_ref[g], 0.0).astype(BF16)
        mixed = jnp.dot(w, vn[:, cs].astype(BF16), preferred_element_type=F32) + b_ref[g]
        yb_ref[:, cs] = (u_ref[:, cs] * mixed).astype(yb_ref.dtype)


def _sgu_prompt(uv, w_spatial, b_spatial, g_sgu):
    nch = SEQ // CHUNK
    vmem = 2 * (3 * _nbytes((CHUNK, SGU_W), F32) + _nbytes((CHUNK, SGU_W), BF16)) + 6 * _nbytes((CHUNK, SGU_W), F32)
    return pl.pallas_call(
        functools.partial(_sgu_kernel, t=CHUNK, last_only=True),
        grid=(BATCH, nch),
        in_specs=[
            pl.BlockSpec((CHUNK, SGU_W), lambda b, c: (b * nch + c, 0)),
            pl.BlockSpec((CHUNK, SGU_W), lambda b, c: (b * nch + c, 1)),
            pl.BlockSpec((N_SGU_GROUPS, CHUNK, CHUNK), lambda b, c: (0, 0, 0)),
            pl.BlockSpec((N_SGU_GROUPS, CHUNK, 1), lambda b, c: (0, 0, 0)),
            pl.BlockSpec((1, SGU_W), lambda b, c: (0, 0)),
        ],
        out_specs=[pl.BlockSpec((CHUNK, SGU_W), lambda b, c: (b * nch + c, 0)),
                   pl.BlockSpec((None, CHUNK, SGU_W), lambda b, c: (b, 0, 0))],
        out_shape=[jax.ShapeDtypeStruct((ROWS_P, SGU_W), BF16),
                   jax.ShapeDtypeStruct((BATCH, CHUNK, SGU_W), F32)],
        compiler_params=_params(("arbitrary", "arbitrary"), vmem),
        name="sgu_prompt",
    )(uv, uv, w_spatial, b_spatial[:, :CHUNK, None], g_sgu.reshape(1, SGU_W))


def _sgu_sample(uv, w_spatial, b_spatial, g_sgu):
    reps = ROWS_S // DEC_SEQ
    w = jnp.tile(w_spatial[:, :DEC_SEQ, :DEC_SEQ], (1, reps, reps))
    b = jnp.tile(b_spatial[:, :DEC_SEQ], (1, reps))[:, :, None]
    vmem = 2 * (4 * _nbytes((ROWS_S, SGU_W), F32)) + 6 * _nbytes((ROWS_S, SGU_W), F32)
    return pl.pallas_call(
        functools.partial(_sgu_kernel, t=DEC_SEQ, last_only=False),
        grid=(1, 1),
        in_specs=[
            pl.BlockSpec((ROWS_S, SGU_W), lambda b, c: (0, 0)),
            pl.BlockSpec((ROWS_S, SGU_W), lambda b, c: (0, 1)),
            pl.BlockSpec((N_SGU_GROUPS, ROWS_S, ROWS_S), lambda b, c: (0, 0, 0)),
            pl.BlockSpec((N_SGU_GROUPS, ROWS_S, 1), lambda b, c: (0, 0, 0)),
            pl.BlockSpec((1, SGU_W), lambda b, c: (0, 0)),
        ],
        out_specs=[pl.BlockSpec((ROWS_S, SGU_W), lambda b, c: (0, 0)),
                   pl.BlockSpec((ROWS_S, SGU_W), lambda b, c: (0, 0))],
        out_shape=[jax.ShapeDtypeStruct((ROWS_S, SGU_W), BF16),
                   jax.ShapeDtypeStruct((ROWS_S, SGU_W), F32)],
        compiler_params=_params(("arbitrary", "arbitrary"), vmem),
        name="sgu_sample",
    )(uv, uv, w, b, g_sgu.reshape(1, SGU_W))


TAIL = 8


def _ffn_kernel(xp_ref, xs_ref, wg_ref, wu_ref, cw_ref, cb_ref, p1_ref, p2_ref,
                midp_ref, mids_ref, tail_ref, as_ref, wbg, wbu, carry):
    m = pl.program_id(1)

    @pl.when(m == 0)
    def _():
        _cast_weight(wg_ref, wbg)
        _cast_weight(wu_ref, wbu)

    def act(a, sh1, sh2, up):
        cw = cw_ref[...]
        c = cb_ref[...] + ((cw[0:1] * sh2 + cw[1:2] * sh1) + cw[2:3] * a)
        return jax.nn.silu(c) * up

    @pl.when(m < N_MP)
    def _():
        x = xp_ref[...]
        a = jnp.dot(x, wbg[...], preferred_element_type=F32)
        up = jnp.dot(x, wbu[...], preferred_element_type=F32)

        @pl.when(m % (SEQ // BM) == 0)
        def _():
            carry[...] = jnp.zeros(carry.shape, F32)

        prev = carry[...]
        row = lax.broadcasted_iota(jnp.int32, a.shape, 0)
        sh1 = jnp.where(row == 0, prev[TAIL - 1:TAIL], pltpu.roll(a, 1, 0))
        sh2 = jnp.where(row == 0, prev[TAIL - 2:TAIL - 1],
                        jnp.where(row == 1, prev[TAIL - 1:TAIL], pltpu.roll(a, 2, 0)))
        midp_ref[...] = act(a, sh1, sh2, up).astype(midp_ref.dtype)
        carry[...] = a[BM - TAIL:]
        tail_ref[...] = a[BM - TAIL:]

    @pl.when(m == N_MP)
    def _():
        x = xs_ref[...]
        a = jnp.dot(x, wbg[...], preferred_element_type=F32)
        up = jnp.dot(x, wbu[...], preferred_element_type=F32)
        t = lax.broadcasted_iota(jnp.int32, a.shape, 0) & (DEC_SEQ - 1)
        sh1 = jnp.where(t == 0, p1_ref[...], pltpu.roll(a, 1, 0))
        sh2 = jnp.where(t < 2, p2_ref[...], pltpu.roll(a, 2, 0))
        mids_ref[...] = act(a, sh1, sh2, up).astype(mids_ref.dtype)
        as_ref[...] = a


def _ffn_gate_up(hp, hs, w_gate, w_up, conv_w, conv_b, state):
    bn = 256
    prev1 = jnp.pad(state[:, 1:2], ((0, 0), (0, DEC_SEQ - 1), (0, 0))).reshape(ROWS_S, D_FF)
    prev2 = jnp.pad(state, ((0, 0), (0, DEC_SEQ - 2), (0, 0))).reshape(ROWS_S, D_FF)
    tiles_per_seq = SEQ // BM
    pm = lambda n, m: (jnp.minimum(m, N_MP - 1), 0)
    vmem = (2 * (_nbytes((BM + ROWS_S, D_MODEL), BF16))
            + 2 * (2 * _nbytes((D_MODEL, bn), F32) + _nbytes((D_MODEL, bn), BF16))
            + 2 * 4 * _nbytes((BM + ROWS_S, bn), F32) + 10 * _nbytes((BM, bn), F32))
    return pl.pallas_call(
        _ffn_kernel,
        grid=(D_FF // bn, N_MP + 1),
        in_specs=[
            pl.BlockSpec((BM, D_MODEL), pm),
            pl.BlockSpec((ROWS_S, D_MODEL), lambda n, m: (0, 0)),
            pl.BlockSpec((D_MODEL, bn), lambda n, m: (0, n)),
            pl.BlockSpec((D_MODEL, bn), lambda n, m: (0, n)),
            pl.BlockSpec((CONV_W, bn), lambda n, m: (0, n)),
            pl.BlockSpec((1, bn), lambda n, m: (0, n)),
            pl.BlockSpec((ROWS_S, bn), lambda n, m: (0, n)),
            pl.BlockSpec((ROWS_S, bn), lambda n, m: (0, n)),
        ],
        out_specs=[
            pl.BlockSpec((BM, bn), lambda n, m: (jnp.minimum(m, N_MP - 1), n)),
            pl.BlockSpec((ROWS_S, bn), lambda n, m: (0, n)),
            pl.BlockSpec((None, TAIL, bn), lambda n, m: (jnp.minimum(m, N_MP - 1) // tiles_per_seq, 0, n)),
            pl.BlockSpec((ROWS_S, bn), lambda n, m: (0, n)),
        ],
        out_shape=[
            jax.ShapeDtypeStruct((ROWS_P, D_FF), BF16),
            jax.ShapeDtypeStruct((ROWS_S, D_FF), BF16),
            jax.ShapeDtypeStruct((BATCH, TAIL, D_FF), F32),
            jax.ShapeDtypeStruct((ROWS_S, D_FF), F32),
        ],
        scratch_shapes=[pltpu.VMEM((D_MODEL, bn), BF16), pltpu.VMEM((D_MODEL, bn), BF16),
                        pltpu.VMEM((TAIL, bn), F32)],
        compiler_params=_params(("arbitrary", "arbitrary"), vmem),
        name="ffn_gate_up",
    )(hp, hs, w_gate, w_up, conv_w, conv_b.reshape(1, D_FF), prev1, prev2)


def _cast_kernel(x_ref, o_ref):
    o_ref[...] = x_ref[...].astype(o_ref.dtype)


def _cast_bf16(w, name):
    k, n = w.shape
    rows = 256
    vmem = 2 * (_nbytes((rows, n), F32) + _nbytes((rows, n), BF16)) + _nbytes((rows, n), F32)
    return pl.pallas_call(
        _cast_kernel,
        grid=(k // rows,),
        in_specs=[pl.BlockSpec((rows, n), lambda i: (i, 0))],
        out_specs=pl.BlockSpec((rows, n), lambda i: (i, 0)),
        out_shape=jax.ShapeDtypeStruct((k, n), BF16),
        compiler_params=_params(("arbitrary",), vmem),
        name=name,
    )(w)


def _down_kernel(x_ref, w_ref, r_ref, o_ref):
    o_ref[...] = r_ref[...] + jnp.dot(x_ref[...], w_ref[...], preferred_element_type=F32)


def _ffn_down(mid, wb, res, bm, name):
    rows = mid.shape[0]
    bn = 256
    vmem = (2 * (_nbytes((bm, D_FF), BF16) + _nbytes((D_FF, bn), BF16)) + 4 * _nbytes((bm, bn), F32)
            + 2 * _nbytes((bm, bn), F32))
    return pl.pallas_call(
        _down_kernel,
        grid=(rows // bm, D_MODEL // bn),
        in_specs=[
            pl.BlockSpec((bm, D_FF), lambda m, n: (m, 0)),
            pl.BlockSpec((D_FF, bn), lambda m, n: (0, n)),
            pl.BlockSpec((bm, bn), lambda m, n: (m, n)),
        ],
        out_specs=pl.BlockSpec((bm, bn), lambda m, n: (m, n)),
        out_shape=jax.ShapeDtypeStruct((rows, D_MODEL), F32),
        compiler_params=_params(("arbitrary", "arbitrary"), vmem),
        name=name,
    )(mid, wb, res)


def kernel(x_prompt, x_sample, cache_k, cache_v, state_conv, page_table, g_attn, w_in, lam_q1, lam_k1, lam_q2, lam_k2, g_subln, g_sgu, w_spatial, b_spatial, w_o_a, w_o_b, w_out, g_ffn, w_gate, conv_w, conv_b, w_up, w_down, g_final):
    xp = x_prompt.reshape(ROWS_P, D_MODEL)
    xs = x_sample.reshape(ROWS_S, D_MODEL)
    w_in0 = w_in[0]
    lamv = jnp.stack([lam_q1[0], lam_k1[0], lam_q2[0], lam_k2[0]])
    gsub = g_subln[0].reshape(1, V_DIM)

    hp = _rmsnorm_rows(xp, g_attn[0], BF16, "rms_attn_p")
    hs = _rmsnorm_rows(xs, g_attn[0], BF16, "rms_attn_s")

    bn = 512
    one = lambda f: (lambda accs, ex: [f(accs[0])])
    (q_p,), (q_s,) = _fused_matmul(hp, hs, [w_in0], [0], Q_W, bn, one(lambda a: a * SCALE), [BF16], name="proj_q")
    (k_p,), (k_s,) = _fused_matmul(hp, hs, [w_in0], [Q_W // bn], K_W, bn, one(lambda a: a), [F32], name="proj_k")
    (v_p,), (v_s,) = _fused_matmul(hp, hs, [w_in0], [(Q_W + K_W) // bn], V_W, bn, one(lambda a: a), [F32],
                                   name="proj_v")
    (uv_p,), (uv_s,) = _fused_matmul(hp, hs, [w_in0], [(Q_W + K_W + V_W) // bn], 2 * SGU_W, bn,
                                     one(_gelu_exact), [F32], name="proj_uv")
    (gt_p,), (gt_s,) = _fused_matmul(hp, hs, [w_in0], [(Q_W + K_W + V_W + 2 * SGU_W) // bn], 2 * D_MODEL, bn,
                                     one(jax.nn.sigmoid), [F32], name="proj_gates")

    ya_p = _attn_prompt(q_p, k_p, v_p, lamv, gsub)
    ya_s = _attn_sample(q_s, k_s, v_s, cache_k, cache_v, page_table, lamv, gsub)

    yb_p, vn_p = _sgu_prompt(uv_p, w_spatial[0], b_spatial[0], g_sgu[0])
    yb_s, vn_s = _sgu_sample(uv_s, w_spatial[0], b_spatial[0], g_sgu[0])

    (ta_p,), (ta_s,) = _fused_matmul(ya_p, ya_s, [w_o_a[0]], [0], D_MODEL, bn,
                                     lambda accs, ex: [ex[0] * accs[0]], [F32],
                                     extras=[(gt_p, gt_s)], ex_off=[0], name="merge_a")
    (z_p,), (z_s,) = _fused_matmul(yb_p, yb_s, [w_o_b[0]], [0], D_MODEL, bn,
                                   lambda accs, ex: [ex[1] + ex[0] * accs[0]], [BF16],
                                   extras=[(gt_p, gt_s), (ta_p, ta_s)], ex_off=[D_MODEL // bn, 0], name="merge_b")
    (x1_p,), (x1_s,) = _fused_matmul(z_p, z_s, [w_out[0]], [0], D_MODEL, bn,
                                     lambda accs, ex: [ex[0] + accs[0]], [F32],
                                     extras=[(xp, xs)], ex_off=[0], name="out_proj")

    h2_p = _rmsnorm_rows(x1_p, g_ffn[0], BF16, "rms_ffn_p")
    h2_s = _rmsnorm_rows(x1_s, g_ffn[0], BF16, "rms_ffn_s")
    mid_p, mid_s, tail_p, a_s = _ffn_gate_up(h2_p, h2_s, w_gate[0], w_up[0], conv_w[0], conv_b[0], state_conv[0])
    wdb = _cast_bf16(w_down[0], "cast_w_down")
    x2_p = _ffn_down(mid_p, wdb, x1_p, 512, "ffn_down_p")
    x2_s = _ffn_down(mid_s, wdb, x1_s, ROWS_S, "ffn_down_s")

    y_p = _rmsnorm_rows(x2_p, g_final, F32, "rms_final_p")
    y_s = _rmsnorm_rows(x2_s, g_final, F32, "rms_final_s")

    return (
        y_p.reshape(BATCH, SEQ, D_MODEL),
        y_s.reshape(DEC_BATCH, DEC_SEQ, D_MODEL),
        k_p.reshape(1, BATCH, SEQ, N_KV_HEADS, 2, HEAD_DIM),
        v_p.reshape(1, BATCH, SEQ, N_KV_HEADS, V_DIM),
        k_s.reshape(1, DEC_BATCH, DEC_SEQ, N_KV_HEADS, 2, HEAD_DIM),
        v_s.reshape(1, DEC_BATCH, DEC_SEQ, N_KV_HEADS, V_DIM),
        vn_p.reshape(1, BATCH, CHUNK, SGU_W),
        vn_s.reshape(1, DEC_BATCH, DEC_SEQ, SGU_W),
        tail_p[:, TAIL - (CONV_W - 1):].reshape(1, BATCH, CONV_W - 1, D_FF),
        a_s.reshape(DEC_BATCH, DEC_SEQ, D_FF)[:, DEC_SEQ - (CONV_W - 1):].reshape(1, DEC_BATCH, CONV_W - 1, D_FF),
    )
```

```python
import functools
import math

import jax
import jax.numpy as jnp
from jax import lax
from jax.experimental import pallas as pl
from jax.experimental.pallas import tpu as pltpu

D_MODEL = 4096
BATCH = 4
SEQ = 2048
DEC_BATCH = 32
DEC_SEQ = 8
PAST_LEN = 8192
PAGE_SIZE = 128
N_HEADS = 16
N_KV_HEADS = 8
GROUP = N_HEADS // N_KV_HEADS
HEAD_DIM = D_MODEL // (2 * N_HEADS)
V_DIM = 2 * HEAD_DIM
Q_W = N_HEADS * 2 * HEAD_DIM
K_W = N_KV_HEADS * 2 * HEAD_DIM
V_W = N_KV_HEADS * V_DIM
SGU_W = D_MODEL
N_SGU_GROUPS = 8
SGU_GROUP_W = SGU_W // N_SGU_GROUPS
CHUNK = 128
D_FF = 11008
CONV_W = 3
SCALE = 1.0 / math.sqrt(HEAD_DIM)
NEG = -1e30
EPS = 1e-6
LAMBDA_INIT = 0.8 - 0.6 * math.exp(-0.3 * 0)

ROWS_P = BATCH * SEQ
ROWS_S = DEC_BATCH * DEC_SEQ
N_PAGES = PAST_LEN // PAGE_SIZE

V7X_VMEM_BYTES = 64 * 1024 * 1024
VMEM_CAP_BYTES = 58 * 1024 * 1024

BM = 1024
N_MP = ROWS_P // BM

F32 = jnp.float32
BF16 = jnp.bfloat16
_NT = (((1,), (1,)), ((), ()))


def _nbytes(shape, dtype):
    return math.prod(shape) * jnp.dtype(dtype).itemsize


def _params(sem, vmem_bytes):
    return pltpu.CompilerParams(dimension_semantics=sem,
                                vmem_limit_bytes=int(min(vmem_bytes, VMEM_CAP_BYTES)))


def _rms_kernel(x_ref, g_ref, o_ref):
    x = x_ref[...].astype(F32)
    ms = jnp.mean(x * x, axis=-1, keepdims=True)
    o_ref[...] = (x * lax.rsqrt(ms + EPS) * g_ref[...]).astype(o_ref.dtype)


def _rmsnorm_rows(x, g, out_dtype, name):
    n, d = x.shape
    rows = min(n, 256)
    vmem = 2 * (_nbytes((rows, d), x.dtype) + _nbytes((rows, d), out_dtype)) + 4 * _nbytes((rows, d), F32)
    return pl.pallas_call(
        _rms_kernel,
        grid=(n // rows,),
        in_specs=[pl.BlockSpec((rows, d), lambda i: (i, 0)), pl.BlockSpec((1, d), lambda i: (0, 0))],
        out_specs=pl.BlockSpec((rows, d), lambda i: (i, 0)),
        out_shape=jax.ShapeDtypeStruct((n, d), out_dtype),
        compiler_params=_params(("arbitrary",), vmem),
        name=name,
    )(x, g.reshape(1, d))


def _cast_weight(w_ref, wb_ref):
    k_dim = w_ref.shape[0]
    chunk = 512

    def body(i, carry):
        r = pl.multiple_of(i * chunk, chunk)
        wb_ref[pl.ds(r, chunk), :] = w_ref[pl.ds(r, chunk), :].astype(BF16)
        return carry

    lax.fori_loop(0, k_dim // chunk, body, 0)


def _mm_kernel(*refs, n_w, n_ex, n_out, epilogue):
    it = iter(refs)
    xp_ref, xs_ref = next(it), next(it)
    w_refs = [next(it) for _ in range(n_w)]
    exp_refs = [next(it) for _ in range(n_ex)]
    exs_refs = [next(it) for _ in range(n_ex)]
    outp_refs = [next(it) for _ in range(n_out)]
    outs_refs = [next(it) for _ in range(n_out)]
    wb_refs = [next(it) for _ in range(n_w)]
    m = pl.program_id(1)

    @pl.when(m == 0)
    def _():
        for w_ref, wb_ref in zip(w_refs, wb_refs):
            _cast_weight(w_ref, wb_ref)

    def run(x_ref, ex_refs, out_refs):
        x = x_ref[...].astype(BF16)
        accs = [jnp.dot(x, wb[...], preferred_element_type=F32) for wb in wb_refs]
        vals = epilogue(accs, [r[...] for r in ex_refs])
        for o_ref, v in zip(out_refs, vals):
            o_ref[...] = v.astype(o_ref.dtype)

    @pl.when(m < N_MP)
    def _():
        run(xp_ref, exp_refs, outp_refs)

    @pl.when(m == N_MP)
    def _():
        run(xs_ref, exs_refs, outs_refs)


def _fused_matmul(xp, xs, ws, w_off, n_cols, bn, epilogue, out_dtypes, extras=(), ex_off=(), name="mm"):
    k_dim = xp.shape[1]
    n_w, n_ex, n_out = len(ws), len(extras), len(out_dtypes)
    pm = lambda n, m: (jnp.minimum(m, N_MP - 1), 0)
    in_specs = [pl.BlockSpec((BM, k_dim), pm), pl.BlockSpec((ROWS_S, k_dim), lambda n, m: (0, 0))]
    for off in w_off:
        in_specs.append(pl.BlockSpec((k_dim, bn), lambda n, m, off=off: (0, n + off)))
    for off in ex_off:
        in_specs.append(pl.BlockSpec((BM, bn), lambda n, m, off=off: (jnp.minimum(m, N_MP - 1), n + off)))
    for off in ex_off:
        in_specs.append(pl.BlockSpec((ROWS_S, bn), lambda n, m, off=off: (0, n + off)))
    out_specs = ([pl.BlockSpec((BM, bn), lambda n, m: (jnp.minimum(m, N_MP - 1), n))] * n_out
                 + [pl.BlockSpec((ROWS_S, bn), lambda n, m: (0, n))] * n_out)
    out_shape = ([jax.ShapeDtypeStruct((ROWS_P, n_cols), dt) for dt in out_dtypes]
                 + [jax.ShapeDtypeStruct((ROWS_S, n_cols), dt) for dt in out_dtypes])
    vmem = (2 * (_nbytes((BM, k_dim), xp.dtype) + _nbytes((ROWS_S, k_dim), xs.dtype))
            + n_w * (2 * _nbytes((k_dim, bn), F32) + _nbytes((k_dim, bn), BF16))
            + 2 * (n_ex + n_out) * _nbytes((BM + ROWS_S, bn), F32)
            + (n_w + 2) * _nbytes((BM, bn), F32))
    args = [xp, xs] + list(ws) + [e[0] for e in extras] + [e[1] for e in extras]
    res = pl.pallas_call(
        functools.partial(_mm_kernel, n_w=n_w, n_ex=n_ex, n_out=n_out, epilogue=epilogue),
        grid=(n_cols // bn, N_MP + 1),
        in_specs=in_specs,
        out_specs=out_specs,
        out_shape=out_shape,
        scratch_shapes=[pltpu.VMEM((k_dim, bn), BF16) for _ in range(n_w)],
        compiler_params=_params(("arbitrary", "arbitrary"), vmem),
        name=name,
    )(*args)
    return res[:n_out], res[n_out:]


def _gelu_exact(x):
    return 0.5 * x * (1.0 + lax.erf(x * math.sqrt(0.5)))


def _lambda_value(lv):
    a = jnp.sum(lv[0:1] * lv[1:2], axis=-1, keepdims=True)
    b = jnp.sum(lv[2:3] * lv[3:4], axis=-1, keepdims=True)
    return jnp.exp(a) - jnp.exp(b) + LAMBDA_INIT


def _head_norm(o, g):
    ms = jnp.mean(o * o, axis=-1, keepdims=True)
    return o * lax.rsqrt(ms + EPS) * g * (1.0 - LAMBDA_INIT)


def _attn_prompt_kernel(q_ref, k_ref, v_ref, lam_ref, g_ref, o_ref, kb, vb, *, bq):
    def cast_rows(i, carry):
        r = pl.multiple_of(i * 256, 256)
        kb[pl.ds(r, 256), :] = k_ref[pl.ds(r, 256), :].astype(BF16)
        vb[pl.ds(r, 256), :] = v_ref[pl.ds(r, 256), :].astype(BF16)
        return carry
    lax.fori_loop(0, SEQ // 256, cast_rows, 0)

    lam = _lambda_value(lam_ref[...])
    row = lax.broadcasted_iota(jnp.int32, (GROUP * bq, bq), 0) & (bq - 1)
    col = lax.broadcasted_iota(jnp.int32, (GROUP * bq, bq), 1)
    causal = col <= row
    for qi in range(SEQ // bq):
        keys = (qi + 1) * bq
        q = q_ref[qi * bq:(qi + 1) * bq, :]
        o_maps = []
        for c in range(2):
            qc = jnp.concatenate([q[:, (g * 2 + c) * HEAD_DIM:(g * 2 + c + 1) * HEAD_DIM] for g in range(GROUP)],
                                 axis=0)
            s = lax.dot_general(qc, kb[0:keys, c * HEAD_DIM:(c + 1) * HEAD_DIM], _NT, preferred_element_type=F32)
            diag = jnp.where(causal, s[:, keys - bq:], NEG)
            s = diag if qi == 0 else jnp.concatenate([s[:, :keys - bq], diag], axis=1)
            p = jnp.exp(s - jnp.max(s, axis=-1, keepdims=True))
            l = jnp.sum(p, axis=-1, keepdims=True)
            o_maps.append(jnp.dot(p.astype(BF16), vb[0:keys, :], preferred_element_type=F32) / l)
        o = o_maps[0] - lam * o_maps[1]
        for g in range(GROUP):
            o_ref[qi * bq:(qi + 1) * bq, g * V_DIM:(g + 1) * V_DIM] = _head_norm(
                o[g * bq:(g + 1) * bq], g_ref[...]).astype(o_ref.dtype)


def _attn_prompt(q, k, v, lamv, g_subln):
    bq = 256
    qw = GROUP * 2 * HEAD_DIM
    vmem = (2 * (2 * _nbytes((SEQ, qw), BF16) + 2 * _nbytes((SEQ, V_DIM), F32))
            + 2 * _nbytes((SEQ, V_DIM), BF16) + 10 * _nbytes((GROUP * bq, SEQ), F32))
    return pl.pallas_call(
        functools.partial(_attn_prompt_kernel, bq=bq),
        grid=(BATCH, N_KV_HEADS),
        in_specs=[
            pl.BlockSpec((SEQ, qw), lambda b, h: (b, h)),
            pl.BlockSpec((SEQ, 2 * HEAD_DIM), lambda b, h: (b, h)),
            pl.BlockSpec((SEQ, V_DIM), lambda b, h: (b, h)),
            pl.BlockSpec((4, HEAD_DIM), lambda b, h: (0, 0)),
            pl.BlockSpec((1, V_DIM), lambda b, h: (0, 0)),
        ],
        out_specs=pl.BlockSpec((SEQ, GROUP * V_DIM), lambda b, h: (b, h)),
        out_shape=jax.ShapeDtypeStruct((ROWS_P, N_HEADS * V_DIM), BF16),
        scratch_shapes=[pltpu.VMEM((SEQ, 2 * HEAD_DIM), BF16), pltpu.VMEM((SEQ, V_DIM), BF16)],
        compiler_params=_params(("arbitrary", "arbitrary"), vmem),
        name="attn_prompt",
    )(q, k, v, lamv, g_subln)


PAGES_PER_STEP = 8
QROWS = 2 * GROUP * DEC_SEQ
PAIRS = N_KV_HEADS // 2
PROWS = 2 * QROWS


def _attn_sample_kernel(pt_ref, qbd_ref, knew_ref, vnew_ref, lam_ref, g_ref, *rest):
    del pt_ref
    kp_refs = rest[:PAGES_PER_STEP]
    vp_refs = rest[PAGES_PER_STEP:3 * PAGES_PER_STEP]
    o_ref, m_s, l_s, acc_s = rest[3 * PAGES_PER_STEP:]
    j = pl.program_id(1)

    @pl.when(j == 0)
    def _():
        m_s[...] = jnp.full(m_s.shape, NEG, F32)
        l_s[...] = jnp.zeros(l_s.shape, F32)
        acc_s[...] = jnp.zeros(acc_s.shape, F32)

    def scores(k_tiles):
        cols = []
        for kt in k_tiles:
            rows = [lax.dot_general(qbd_ref[0, hp], kt(hp), _NT, preferred_element_type=F32)
                    for hp in range(PAIRS)]
            cols.append(jnp.concatenate(rows, axis=0))
        return cols[0] if len(cols) == 1 else jnp.concatenate(cols, axis=1)

    def update(s, v_tiles, width):
        m_prev = m_s[...]
        m_new = jnp.maximum(m_prev, jnp.max(s, axis=-1, keepdims=True))
        alpha = jnp.exp(m_prev - m_new)
        p = jnp.exp(s - m_new)
        l_s[...] = alpha * l_s[...] + jnp.sum(p, axis=-1, keepdims=True)
        pb = p.astype(BF16)
        pv = []
        for hp in range(PAIRS):
            acc = None
            for i, vt in enumerate(v_tiles):
                d = jnp.dot(pb[hp * PROWS:(hp + 1) * PROWS, i * width:(i + 1) * width], vt(hp),
                            preferred_element_type=F32)
                acc = d if acc is None else acc + d
            pv.append(acc)
        acc_s[...] = alpha * acc_s[...] + jnp.concatenate(pv, axis=0)
        m_s[...] = m_new

    def k_page(ref):
        def get(hp):
            maps = [ref[pl.ds(2 * hp + c, 2 * PAGE_SIZE, stride=N_KV_HEADS), :] for c in range(2)]
            return jnp.concatenate(maps, axis=1).astype(BF16)
        return get

    def v_page(ref_lo, ref_hi):
        def get(hp):
            halves = [r[pl.ds(hp, 2 * PAGE_SIZE, stride=PAIRS), :] for r in (ref_lo, ref_hi)]
            return jnp.concatenate(halves, axis=1).astype(BF16)
        return get

    def which_head(shape):
        return (lax.broadcasted_iota(jnp.int32, shape, 0) >> int(math.log2(QROWS))) & 1

    s = scores([k_page(r) for r in kp_refs])
    col = lax.broadcasted_iota(jnp.int32, s.shape, 1)
    s = jnp.where((col & 1) == which_head(s.shape), s, NEG)
    update(s, [v_page(vp_refs[2 * p], vp_refs[2 * p + 1]) for p in range(PAGES_PER_STEP)], 2 * PAGE_SIZE)

    @pl.when(j == pl.num_programs(1) - 1)
    def _():
        pad = jnp.zeros((PAGE_SIZE - 2 * DEC_SEQ, 2 * HEAD_DIM), F32)

        def new_tile(ref):
            def get(hp):
                parts = [ref[:, h * V_DIM:(h + 1) * V_DIM] for h in (hp, hp + PAIRS)]
                return jnp.concatenate(parts + [pad], axis=0).astype(BF16)
            return get

        s = scores([new_tile(knew_ref)])
        t = lax.broadcasted_iota(jnp.int32, s.shape, 0) & (DEC_SEQ - 1)
        col = lax.broadcasted_iota(jnp.int32, s.shape, 1)
        ok = ((col >> int(math.log2(DEC_SEQ))) == which_head(s.shape)) & ((col & (DEC_SEQ - 1)) <= t)
        update(jnp.where(ok, s, NEG), [new_tile(vnew_ref)], PAGE_SIZE)

        lam = _lambda_value(lam_ref[...])
        o = acc_s[...] / l_s[...]
        half = GROUP * DEC_SEQ
        for hp in range(PAIRS):
            for hi in range(2):
                r0 = hp * PROWS + hi * QROWS
                d = o[r0:r0 + half] - lam * o[r0 + half:r0 + QROWS]
                y = _head_norm(d, g_ref[...])
                for g in range(GROUP):
                    c0 = ((hi * PAIRS + hp) * GROUP + g) * V_DIM
                    o_ref[:, c0:c0 + V_DIM] = y[g * DEC_SEQ:(g + 1) * DEC_SEQ]


def _attn_sample(q_s, k_s, v_s, cache_k, cache_v, page_table, lamv, g_subln):
    q6 = q_s.reshape(DEC_BATCH, DEC_SEQ, N_KV_HEADS, GROUP, 2, HEAD_DIM).transpose(0, 2, 4, 3, 1, 5)
    zero = jnp.zeros_like(q6[:, :, 0])
    top = jnp.concatenate([q6[:, :, 0], zero], axis=-1).reshape(DEC_BATCH, N_KV_HEADS, QROWS // 2, 2 * HEAD_DIM)
    bot = jnp.concatenate([zero, q6[:, :, 1]], axis=-1).reshape(DEC_BATCH, N_KV_HEADS, QROWS // 2, 2 * HEAD_DIM)
    qbd = jnp.concatenate([top, bot], axis=2)
    qbd = qbd.reshape(DEC_BATCH, 2, PAIRS, QROWS, 2 * HEAD_DIM).transpose(0, 2, 1, 3, 4).reshape(
        DEC_BATCH, PAIRS, PROWS, 2 * HEAD_DIM)
    n_pool = cache_k.shape[1]
    ck = cache_k.reshape(n_pool, PAGE_SIZE * N_KV_HEADS * 2, HEAD_DIM)
    cv = cache_v.reshape(n_pool, PAGE_SIZE * N_KV_HEADS, V_DIM)
    pps = PAGES_PER_STEP

    def k_spec(p):
        return pl.BlockSpec((None, ck.shape[1], HEAD_DIM), lambda b, j, pt: (pt[b, j * pps + p], 0, 0))

    def v_spec(p, half):
        return pl.BlockSpec((None, cv.shape[1], HEAD_DIM), lambda b, j, pt: (pt[b, j * pps + p], 0, half))

    in_specs = [
        pl.BlockSpec((1, PAIRS, PROWS, 2 * HEAD_DIM), lambda b, j, pt: (b, 0, 0, 0)),
        pl.BlockSpec((DEC_SEQ, K_W), lambda b, j, pt: (b, 0)),
        pl.BlockSpec((DEC_SEQ, V_W), lambda b, j, pt: (b, 0)),
        pl.BlockSpec((4, HEAD_DIM), lambda b, j, pt: (0, 0)),
        pl.BlockSpec((1, V_DIM), lambda b, j, pt: (0, 0)),
    ] + [k_spec(p) for p in range(pps)] + [v_spec(p, half) for p in range(pps) for half in range(2)]
    rows = N_KV_HEADS * QROWS
    vmem = (2 * 2 * pps * _nbytes((PAGE_SIZE, K_W), F32) + 2 * pps * _nbytes((PAGE_SIZE, K_W), BF16)
            + 6 * _nbytes((rows, pps * 2 * PAGE_SIZE), F32) + 4 * _nbytes((rows, V_DIM), F32)
            + 4 * _nbytes((DEC_SEQ, Q_W), F32))
    return pl.pallas_call(
        _attn_sample_kernel,
        grid_spec=pltpu.PrefetchScalarGridSpec(
            num_scalar_prefetch=1,
            grid=(DEC_BATCH, N_PAGES // pps),
            in_specs=in_specs,
            out_specs=pl.BlockSpec((DEC_SEQ, N_HEADS * V_DIM), lambda b, j, pt: (b, 0)),
            scratch_shapes=[pltpu.VMEM((rows, 1), F32), pltpu.VMEM((rows, 1), F32),
                            pltpu.VMEM((rows, V_DIM), F32)],
        ),
        out_shape=jax.ShapeDtypeStruct((ROWS_S, N_HEADS * V_DIM), F32),
        compiler_params=_params(("arbitrary", "arbitrary"), vmem),
        name="attn_sample",
    )(page_table, qbd, k_s, v_s, lamv, g_subln, *([ck] * pps), *([cv] * (2 * pps)))


def _sgu_kernel(u_ref, v_ref, w_ref, b_ref, g_ref, yb_ref, vn_ref, *, t, last_only):
    v = v_ref[...]
    ms = jnp.mean(v * v, axis=-1, keepdims=True)
    vn = v * lax.rsqrt(ms + EPS) * g_ref[...]
    if last_only:
        @pl.when(pl.program_id(1) == pl.num_programs(1) - 1)
        def _():
            vn_ref[...] = vn
    else:
        vn_ref[...] = vn
    r = v.shape[0]
    row = lax.broadcasted_iota(jnp.int32, (r, r), 0)
    col = lax.broadcasted_iota(jnp.int32, (r, r), 1)
    keep = (col <= row) & (col >= row - (row & (t - 1)))
    for g in range(N_SGU_GROUPS):
        cs = slice(g * SGU_GROUP_W, (g + 1) * SGU_GROUP_W)
        w = jnp.where(keep, w_ref[g], 0.0).astype(BF16)
        mixed = jnp.dot(w, vn[:, cs].astype(BF16), preferred_element_type=F32) + b_ref[g]
        yb_ref[:, cs] = (u_ref[:, cs] * mixed).astype(yb_ref.dtype)


def _sgu_prompt(uv, w_spatial, b_spatial, g_sgu):
    nch = SEQ // CHUNK
    vmem = 2 * (3 * _nbytes((CHUNK, SGU_W), F32) + _nbytes((CHUNK, SGU_W), BF16)) + 6 * _nbytes((CHUNK, SGU_W), F32)
    return pl.pallas_call(
        functools.partial(_sgu_kernel, t=CHUNK, last_only=True),
        grid=(BATCH, nch),
        in_specs=[
            pl.BlockSpec((CHUNK, SGU_W), lambda b, c: (b * nch + c, 0)),
            pl.BlockSpec((CHUNK, SGU_W), lambda b, c: (b * nch + c, 1)),
            pl.BlockSpec((N_SGU_GROUPS, CHUNK, CHUNK), lambda b, c: (0, 0, 0)),
            pl.BlockSpec((N_SGU_GROUPS, CHUNK, 1), lambda b, c: (0, 0, 0)),
            pl.BlockSpec((1, SGU_W), lambda b, c: (0, 0)),
        ],
        out_specs=[pl.BlockSpec((CHUNK, SGU_W), lambda b, c: (b * nch + c, 0)),
                   pl.BlockSpec((None, CHUNK, SGU_W), lambda b, c: (b, 0, 0))],
        out_shape=[jax.ShapeDtypeStruct((ROWS_P, SGU_W), BF16),
                   jax.ShapeDtypeStruct((BATCH, CHUNK, SGU_W), F32)],
        compiler_params=_params(("arbitrary", "arbitrary"), vmem),
        name="sgu_prompt",
    )(uv, uv, w_spatial, b_spatial[:, :CHUNK, None], g_sgu.reshape(1, SGU_W))


def _sgu_sample(uv, w_spatial, b_spatial, g_sgu):
    reps = ROWS_S // DEC_SEQ
    w = jnp.tile(w_spatial[:, :DEC_SEQ, :DEC_SEQ], (1, reps, reps))
    b = jnp.tile(b_spatial[:, :DEC_SEQ], (1, reps))[:, :, None]
    vmem = 2 * (4 * _nbytes((ROWS_S, SGU_W), F32)) + 6 * _nbytes((ROWS_S, SGU_W), F32)
    return pl.pallas_call(
        functools.partial(_sgu_kernel, t=DEC_SEQ, last_only=False),
        grid=(1, 1),
        in_specs=[
            pl.BlockSpec((ROWS_S, SGU_W), lambda b, c: (0, 0)),
            pl.BlockSpec((ROWS_S, SGU_W), lambda b, c: (0, 1)),
            pl.BlockSpec((N_SGU_GROUPS, ROWS_S, ROWS_S), lambda b, c: (0, 0, 0)),
            pl.BlockSpec((N_SGU_GROUPS, ROWS_S, 1), lambda b, c: (0, 0, 0)),
            pl.BlockSpec((1, SGU_W), lambda b, c: (0, 0)),
        ],
        out_specs=[pl.BlockSpec((ROWS_S, SGU_W), lambda b, c: (0, 0)),
                   pl.BlockSpec((ROWS_S, SGU_W), lambda b, c: (0, 0))],
        out_shape=[jax.ShapeDtypeStruct((ROWS_S, SGU_W), BF16),
                   jax.ShapeDtypeStruct((ROWS_S, SGU_W), F32)],
        compiler_params=_params(("arbitrary", "arbitrary"), vmem),
        name="sgu_sample",
    )(uv, uv, w, b, g_sgu.reshape(1, SGU_W))


TAIL = 8


def _ffn_kernel(xp_ref, xs_ref, wg_ref, wu_ref, cw_ref, cb_ref, p1_ref, p2_ref,
                midp_ref, mids_ref, tail_ref, as_ref, wbg, wbu, carry):
    m = pl.program_id(1)

    @pl.when(m == 0)
    def _():
        _cast_weight(wg_ref, wbg)
        _cast_weight(wu_ref, wbu)

    def act(a, sh1, sh2, up):
        cw = cw_ref[...]
        c = cb_ref[...] + ((cw[0:1] * sh2 + cw[1:2] * sh1) + cw[2:3] * a)
        return jax.nn.silu(c) * up

    @pl.when(m < N_MP)
    def _():
        x = xp_ref[...]
        a = jnp.dot(x, wbg[...], preferred_element_type=F32)
        up = jnp.dot(x, wbu[...], preferred_element_type=F32)

        @pl.when(m % (SEQ // BM) == 0)
        def _():
            carry[...] = jnp.zeros(carry.shape, F32)

        prev = carry[...]
        row = lax.broadcasted_iota(jnp.int32, a.shape, 0)
        sh1 = jnp.where(row == 0, prev[TAIL - 1:TAIL], pltpu.roll(a, 1, 0))
        sh2 = jnp.where(row == 0, prev[TAIL - 2:TAIL - 1],
                        jnp.where(row == 1, prev[TAIL - 1:TAIL], pltpu.roll(a, 2, 0)))
        midp_ref[...] = act(a, sh1, sh2, up).astype(midp_ref.dtype)
        carry[...] = a[BM - TAIL:]
        tail_ref[...] = a[BM - TAIL:]

    @pl.when(m == N_MP)
    def _():
        x = xs_ref[...]
        a = jnp.dot(x, wbg[...], preferred_element_type=F32)
        up = jnp.dot(x, wbu[...], preferred_element_type=F32)
        t = lax.broadcasted_iota(jnp.int32, a.shape, 0) & (DEC_SEQ - 1)
        sh1 = jnp.where(t == 0, p1_ref[...], pltpu.roll(a, 1, 0))
        sh2 = jnp.where(t < 2, p2_ref[...], pltpu.roll(a, 2, 0))
        mids_ref[...] = act(a, sh1, sh2, up).astype(mids_ref.dtype)
        as_ref[...] = a


def _ffn_gate_up(hp, hs, w_gate, w_up, conv_w, conv_b, state):
    bn = 256
    prev1 = jnp.pad(state[:, 1:2], ((0, 0), (0, DEC_SEQ - 1), (0, 0))).reshape(ROWS_S, D_FF)
    prev2 = jnp.pad(state, ((0, 0), (0, DEC_SEQ - 2), (0, 0))).reshape(ROWS_S, D_FF)
    tiles_per_seq = SEQ // BM
    pm = lambda n, m: (jnp.minimum(m, N_MP - 1), 0)
    vmem = (2 * (_nbytes((BM + ROWS_S, D_MODEL), BF16))
            + 2 * (2 * _nbytes((D_MODEL, bn), F32) + _nbytes((D_MODEL, bn), BF16))
            + 2 * 4 * _nbytes((BM + ROWS_S, bn), F32) + 10 * _nbytes((BM, bn), F32))
    return pl.pallas_call(
        _ffn_kernel,
        grid=(D_FF // bn, N_MP + 1),
        in_specs=[
            pl.BlockSpec((BM, D_MODEL), pm),
            pl.BlockSpec((ROWS_S, D_MODEL), lambda n, m: (0, 0)),
            pl.BlockSpec((D_MODEL, bn), lambda n, m: (0, n)),
            pl.BlockSpec((D_MODEL, bn), lambda n, m: (0, n)),
            pl.BlockSpec((CONV_W, bn), lambda n, m: (0, n)),
            pl.BlockSpec((1, bn), lambda n, m: (0, n)),
            pl.BlockSpec((ROWS_S, bn), lambda n, m: (0, n)),
            pl.BlockSpec((ROWS_S, bn), lambda n, m: (0, n)),
        ],
        out_specs=[
            pl.BlockSpec((BM, bn), lambda n, m: (jnp.minimum(m, N_MP - 1), n)),
            pl.BlockSpec((ROWS_S, bn), lambda n, m: (0, n)),
            pl.BlockSpec((None, TAIL, bn), lambda n, m: (jnp.minimum(m, N_MP - 1) // tiles_per_seq, 0, n)),
            pl.BlockSpec((ROWS_S, bn), lambda n, m: (0, n)),
        ],
        out_shape=[
            jax.ShapeDtypeStruct((ROWS_P, D_FF), BF16),
            jax.ShapeDtypeStruct((ROWS_S, D_FF), BF16),
            jax.ShapeDtypeStruct((BATCH, TAIL, D_FF), F32),
            jax.ShapeDtypeStruct((ROWS_S, D_FF), F32),
        ],
        scratch_shapes=[pltpu.VMEM((D_MODEL, bn), BF16), pltpu.VMEM((D_MODEL, bn), BF16),
                        pltpu.VMEM((TAIL, bn), F32)],
        compiler_params=_params(("arbitrary", "arbitrary"), vmem),
        name="ffn_gate_up",
    )(hp, hs, w_gate, w_up, conv_w, conv_b.reshape(1, D_FF), prev1, prev2)


def _cast_kernel(x_ref, o_ref):
    o_ref[...] = x_ref[...].astype(o_ref.dtype)


def _cast_bf16(w, name):
    k, n = w.shape
    rows = 256
    vmem = 2 * (_nbytes((rows, n), F32) + _nbytes((rows, n), BF16)) + _nbytes((rows, n), F32)
    return pl.pallas_call(
        _cast_kernel,
        grid=(k // rows,),
        in_specs=[pl.BlockSpec((rows, n), lambda i: (i, 0))],
        out_specs=pl.BlockSpec((rows, n), lambda i: (i, 0)),
        out_shape=jax.ShapeDtypeStruct((k, n), BF16),
        compiler_params=_params(("arbitrary",), vmem),
        name=name,
    )(w)


def _down_kernel(x_ref, w_ref, r_ref, o_ref):
    o_ref[...] = r_ref[...] + jnp.dot(x_ref[...], w_ref[...], preferred_element_type=F32)


def _ffn_down(mid, wb, res, bm, name):
    rows = mid.shape[0]
    bn = 256
    vmem = (2 * (_nbytes((bm, D_FF), BF16) + _nbytes((D_FF, bn), BF16)) + 4 * _nbytes((bm, bn), F32)
            + 2 * _nbytes((bm, bn), F32))
    return pl.pallas_call(
        _down_kernel,
        grid=(rows // bm, D_MODEL // bn),
        in_specs=[
            pl.BlockSpec((bm, D_FF), lambda m, n: (m, 0)),
            pl.BlockSpec((D_FF, bn), lambda m, n: (0, n)),
            pl.BlockSpec((bm, bn), lambda m, n: (m, n)),
        ],
        out_specs=pl.BlockSpec((bm, bn), lambda m, n: (m, n)),
        out_shape=jax.ShapeDtypeStruct((rows, D_MODEL), F32),
        compiler_params=_params(("arbitrary", "arbitrary"), vmem),
        name=name,
    )(mid, wb, res)


def kernel(x_prompt, x_sample, cache_k, cache_v, state_conv, page_table, g_attn, w_in, lam_q1, lam_k1, lam_q2, lam_k2, g_subln, g_sgu, w_spatial, b_spatial, w_o_a, w_o_b, w_out, g_ffn, w_gate, conv_w, conv_b, w_up, w_down, g_final):
    xp = x_prompt.reshape(ROWS_P, D_MODEL)
    xs = x_sample.reshape(ROWS_S, D_MODEL)
    w_in0 = w_in[0]
    lamv = jnp.stack([lam_q1[0], lam_k1[0], lam_q2[0], lam_k2[0]])
    gsub = g_subln[0].reshape(1, V_DIM)

    hp = _rmsnorm_rows(xp, g_attn[0], BF16, "rms_attn_p")
    hs = _rmsnorm_rows(xs, g_attn[0], BF16, "rms_attn_s")

    bn = 512
    one = lambda f: (lambda accs, ex: [f(accs[0])])
    (q_p,), (q_s,) = _fused_matmul(hp, hs, [w_in0], [0], Q_W, bn, one(lambda a: a * SCALE), [BF16], name="proj_q")
    (k_p,), (k_s,) = _fused_matmul(hp, hs, [w_in0], [Q_W // bn], K_W, bn, one(lambda a: a), [F32], name="proj_k")
    (v_p,), (v_s,) = _fused_matmul(hp, hs, [w_in0], [(Q_W + K_W) // bn], V_W, bn, one(lambda a: a), [F32],
                                   name="proj_v")
    (uv_p,), (uv_s,) = _fused_matmul(hp, hs, [w_in0], [(Q_W + K_W + V_W) // bn], 2 * SGU_W, bn,
                                     one(_gelu_exact), [F32], name="proj_uv")
    (gt_p,), (gt_s,) = _fused_matmul(hp, hs, [w_in0], [(Q_W + K_W + V_W + 2 * SGU_W) // bn], 2 * D_MODEL, bn,
                                     one(jax.nn.sigmoid), [F32], name="proj_gates")

    ya_p = _attn_prompt(q_p, k_p, v_p, lamv, gsub)
    ya_s = _attn_sample(q_s, k_s, v_s, cache_k, cache_v, page_table, lamv, gsub)

    yb_p, vn_p = _sgu_prompt(uv_p, w_spatial[0], b_spatial[0], g_sgu[0])
    yb_s, vn_s = _sgu_sample(uv_s, w_spatial[0], b_spatial[0], g_sgu[0])

    (ta_p,), (ta_s,) = _fused_matmul(ya_p, ya_s, [w_o_a[0]], [0], D_MODEL, bn,
                                     lambda accs, ex: [ex[0] * accs[0]], [F32],
                                     extras=[(gt_p, gt_s)], ex_off=[0], name="merge_a")
    (z_p,), (z_s,) = _fused_matmul(yb_p, yb_s, [w_o_b[0]], [0], D_MODEL, bn,
                                   lambda accs, ex: [ex[1] + ex[0] * accs[0]], [BF16],
                                   extras=[(gt_p, gt_s), (ta_p, ta_s)], ex_off=[D_MODEL // bn, 0], name="merge_b")
    (x1_p,), (x1_s,) = _fused_matmul(z_p, z_s, [w_out[0]], [0], D_MODEL, bn,
                                     lambda accs, ex: [ex[0] + accs[0]], [F32],
                                     extras=[(xp, xs)], ex_off=[0], name="out_proj")

    h2_p = _rmsnorm_rows(x1_p, g_ffn[0], BF16, "rms_ffn_p")
    h2_s = _rmsnorm_rows(x1_s, g_ffn[0], BF16, "rms_ffn_s")
    mid_p, mid_s, tail_p, a_s = _ffn_gate_up(h2_p, h2_s, w_gate[0], w_up[0], conv_w[0], conv_b[0], state_conv[0])
    wdb = _cast_bf16(w_down[0], "cast_w_down")
    x2_p = _ffn_down(mid_p, wdb, x1_p, 512, "ffn_down_p")
    x2_s = _ffn_down(mid_s, wdb, x1_s, ROWS_S, "ffn_down_s")

    y_p = _rmsnorm_rows(x2_p, g_final, F32, "rms_final_p")
    y_s = _rmsnorm_rows(x2_s, g_final, F32, "rms_final_s")

    return (
        y_p.reshape(BATCH, SEQ, D_MODEL),
        y_s.reshape(DEC_BATCH, DEC_SEQ, D_MODEL),
        k_p.reshape(1, BATCH, SEQ, N_KV_HEADS, 2, HEAD_DIM),
        v_p.reshape(1, BATCH, SEQ, N_KV_HEADS, V_DIM),
        k_s.reshape(1, DEC_BATCH, DEC_SEQ, N_KV_HEADS, 2, HEAD_DIM),
        v_s.reshape(1, DEC_BATCH, DEC_SEQ, N_KV_HEADS, V_DIM),
        vn_p.reshape(1, BATCH, CHUNK, SGU_W),
        vn_s.reshape(1, DEC_BATCH, DEC_SEQ, SGU_W),
        tail_p[:, TAIL - (CONV_W - 1):].reshape(1, BATCH, CONV_W - 1, D_FF),
        a_s.reshape(DEC_BATCH, DEC_SEQ, D_FF)[:, DEC_SEQ - (CONV_W - 1):].reshape(1, DEC_BATCH, CONV_W - 1, D_FF),
    )
```

```python
import functools
import math

import jax
import jax.numpy as jnp
from jax import lax
from jax.experimental import pallas as pl
from jax.experimental.pallas import tpu as pltpu

D_MODEL = 4096
BATCH = 4
SEQ = 2048
DEC_BATCH = 32
DEC_SEQ = 8
PAST_LEN = 8192
PAGE_SIZE = 128
N_HEADS = 16
N_KV_HEADS = 8
GROUP = N_HEADS // N_KV_HEADS
HEAD_DIM = D_MODEL // (2 * N_HEADS)
V_DIM = 2 * HEAD_DIM
Q_W = N_HEADS * 2 * HEAD_DIM
K_W = N_KV_HEADS * 2 * HEAD_DIM
V_W = N_KV_HEADS * V_DIM
SGU_W = D_MODEL
N_SGU_GROUPS = 8
SGU_GROUP_W = SGU_W // N_SGU_GROUPS
CHUNK = 128
D_FF = 11008
CONV_W = 3
SCALE = 1.0 / math.sqrt(HEAD_DIM)
NEG = -1e30
EPS = 1e-6
LAMBDA_INIT = 0.8 - 0.6 * math.exp(-0.3 * 0)

ROWS_P = BATCH * SEQ
ROWS_S = DEC_BATCH * DEC_SEQ
N_PAGES = PAST_LEN // PAGE_SIZE

V7X_VMEM_BYTES = 64 * 1024 * 1024
VMEM_CAP_BYTES = 58 * 1024 * 1024

BM = 1024
N_MP = ROWS_P // BM

F32 = jnp.float32
BF16 = jnp.bfloat16
_NT = (((1,), (1,)), ((), ()))


def _nbytes(shape, dtype):
    return math.prod(shape) * jnp.dtype(dtype).itemsize


def _params(sem, vmem_bytes):
    return pltpu.CompilerParams(dimension_semantics=sem,
                                vmem_limit_bytes=int(min(vmem_bytes, VMEM_CAP_BYTES)))


def _rms_kernel(x_ref, g_ref, o_ref):
    x = x_ref[...].astype(F32)
    ms = jnp.mean(x * x, axis=-1, keepdims=True)
    o_ref[...] = (x * lax.rsqrt(ms + EPS) * g_ref[...]).astype(o_ref.dtype)


def _rmsnorm_rows(x, g, out_dtype, name):
    n, d = x.shape
    rows = min(n, 256)
    vmem = 2 * (_nbytes((rows, d), x.dtype) + _nbytes((rows, d), out_dtype)) + 4 * _nbytes((rows, d), F32)
    return pl.pallas_call(
        _rms_kernel,
        grid=(n // rows,),
        in_specs=[pl.BlockSpec((rows, d), lambda i: (i, 0)), pl.BlockSpec((1, d), lambda i: (0, 0))],
        out_specs=pl.BlockSpec((rows, d), lambda i: (i, 0)),
        out_shape=jax.ShapeDtypeStruct((n, d), out_dtype),
        compiler_params=_params(("arbitrary",), vmem),
        name=name,
    )(x, g.reshape(1, d))


def _cast_weight(w_ref, wb_ref):
    k_dim = w_ref.shape[0]
    chunk = 512

    def body(i, carry):
        r = pl.multiple_of(i * chunk, chunk)
        wb_ref[pl.ds(r, chunk), :] = w_ref[pl.ds(r, chunk), :].astype(BF16)
        return carry

    lax.fori_loop(0, k_dim // chunk, body, 0)


def _mm_kernel(*refs, n_x, n_w, n_ex, n_out, n_mp, epilogue):
    it = iter(refs)
    xp_refs = [next(it) for _ in range(n_x)]
    xs_ref = next(it)
    w_refs = [next(it) for _ in range(n_w)]
    exp_refs = [next(it) for _ in range(n_ex)]
    exs_refs = [next(it) for _ in range(n_ex)]
    outp_refs = [next(it) for _ in range(n_out)]
    outs_refs = [next(it) for _ in range(n_out)]
    wb_refs = [next(it) for _ in range(n_w)]
    m = pl.program_id(1)

    @pl.when(m == 0)
    def _():
        for w_ref, wb_ref in zip(w_refs, wb_refs):
            _cast_weight(w_ref, wb_ref)

    def run(x_refs, ex_refs, out_refs):
        kp = wb_refs[0].shape[0] // len(x_refs)
        accs = []
        for wb in wb_refs:
            acc = None
            for i, x_ref in enumerate(x_refs):
                d = jnp.dot(x_ref[...].astype(BF16), wb[i * kp:(i + 1) * kp, :], preferred_element_type=F32)
                acc = d if acc is None else acc + d
            accs.append(acc)
        vals = epilogue(accs, [r[...] for r in ex_refs])
        for o_ref, v in zip(out_refs, vals):
            o_ref[...] = v.astype(o_ref.dtype)

    @pl.when(m < n_mp)
    def _():
        run(xp_refs, exp_refs, outp_refs)

    @pl.when(m == n_mp)
    def _():
        run([xs_ref], exs_refs, outs_refs)


def _fused_matmul(xp, xs, ws, w_off, n_cols, bn, epilogue, out_dtypes, extras=(), ex_off=(), name="mm",
                  bm=BM, x_split=1):
    k_dim = xp.shape[1]
    n_mp = ROWS_P // bm
    n_w, n_ex, n_out = len(ws), len(extras), len(out_dtypes)
    row = lambda m: jnp.minimum(m, n_mp - 1)
    in_specs = [pl.BlockSpec((bm, k_dim // x_split), lambda n, m, i=i: (row(m), i)) for i in range(x_split)]
    in_specs.append(pl.BlockSpec((ROWS_S, k_dim), lambda n, m: (0, 0)))
    for off in w_off:
        in_specs.append(pl.BlockSpec((k_dim, bn), lambda n, m, off=off: (0, n + off)))
    for off in ex_off:
        in_specs.append(pl.BlockSpec((bm, bn), lambda n, m, off=off: (row(m), n + off)))
    for off in ex_off:
        in_specs.append(pl.BlockSpec((ROWS_S, bn), lambda n, m, off=off: (0, n + off)))
    out_specs = ([pl.BlockSpec((bm, bn), lambda n, m: (row(m), n))] * n_out
                 + [pl.BlockSpec((ROWS_S, bn), lambda n, m: (0, n))] * n_out)
    out_shape = ([jax.ShapeDtypeStruct((ROWS_P, n_cols), dt) for dt in out_dtypes]
                 + [jax.ShapeDtypeStruct((ROWS_S, n_cols), dt) for dt in out_dtypes])
    vmem = (2 * (_nbytes((bm, k_dim), xp.dtype) + _nbytes((ROWS_S, k_dim), xs.dtype))
            + n_w * (2 * _nbytes((k_dim, bn), F32) + _nbytes((k_dim, bn), BF16))
            + 2 * (n_ex + n_out) * _nbytes((bm + ROWS_S, bn), F32)
            + (n_w + 1 + x_split) * _nbytes((bm, bn), F32))
    args = [xp] * x_split + [xs] + list(ws) + [e[0] for e in extras] + [e[1] for e in extras]
    res = pl.pallas_call(
        functools.partial(_mm_kernel, n_x=x_split, n_w=n_w, n_ex=n_ex, n_out=n_out, n_mp=n_mp, epilogue=epilogue),
        grid=(n_cols // bn, n_mp + 1),
        in_specs=in_specs,
        out_specs=out_specs,
        out_shape=out_shape,
        scratch_shapes=[pltpu.VMEM((k_dim, bn), BF16) for _ in range(n_w)],
        compiler_params=_params(("arbitrary", "arbitrary"), vmem),
        name=name,
    )(*args)
    return res[:n_out], res[n_out:]


def _gelu_exact(x):
    return 0.5 * x * (1.0 + lax.erf(x * math.sqrt(0.5)))


def _lambda_value(lv):
    a = jnp.sum(lv[0:1] * lv[1:2], axis=-1, keepdims=True)
    b = jnp.sum(lv[2:3] * lv[3:4], axis=-1, keepdims=True)
    return jnp.exp(a) - jnp.exp(b) + LAMBDA_INIT


def _head_norm(o, g):
    ms = jnp.mean(o * o, axis=-1, keepdims=True)
    return o * lax.rsqrt(ms + EPS) * g * (1.0 - LAMBDA_INIT)


def _attn_prompt_kernel(q_ref, k_ref, v_ref, lam_ref, g_ref, o_ref, kb, vb, *, bq):
    def cast_rows(i, carry):
        r = pl.multiple_of(i * 256, 256)
        kb[pl.ds(r, 256), :] = k_ref[pl.ds(r, 256), :].astype(BF16)
        vb[pl.ds(r, 256), :] = v_ref[pl.ds(r, 256), :].astype(BF16)
        return carry
    lax.fori_loop(0, SEQ // 256, cast_rows, 0)

    lam = _lambda_value(lam_ref[...])
    row = lax.broadcasted_iota(jnp.int32, (GROUP * bq, bq), 0) & (bq - 1)
    col = lax.broadcasted_iota(jnp.int32, (GROUP * bq, bq), 1)
    causal = col <= row
    for qi in range(SEQ // bq):
        keys = (qi + 1) * bq
        q = q_ref[qi * bq:(qi + 1) * bq, :]
        o_maps = []
        for c in range(2):
            qc = jnp.concatenate([q[:, (g * 2 + c) * HEAD_DIM:(g * 2 + c + 1) * HEAD_DIM] for g in range(GROUP)],
                                 axis=0)
            s = lax.dot_general(qc, kb[0:keys, c * HEAD_DIM:(c + 1) * HEAD_DIM], _NT, preferred_element_type=F32)
            diag = jnp.where(causal, s[:, keys - bq:], NEG)
            s = diag if qi == 0 else jnp.concatenate([s[:, :keys - bq], diag], axis=1)
            p = jnp.exp(s - jnp.max(s, axis=-1, keepdims=True))
            l = jnp.sum(p, axis=-1, keepdims=True)
            o_maps.append(jnp.dot(p.astype(BF16), vb[0:keys, :], preferred_element_type=F32) / l)
        o = o_maps[0] - lam * o_maps[1]
        for g in range(GROUP):
            o_ref[qi * bq:(qi + 1) * bq, g * V_DIM:(g + 1) * V_DIM] = _head_norm(
                o[g * bq:(g + 1) * bq], g_ref[...]).astype(o_ref.dtype)


def _attn_prompt(q, k, v, lamv, g_subln):
    bq = 256
    qw = GROUP * 2 * HEAD_DIM
    vmem = (2 * (2 * _nbytes((SEQ, qw), BF16) + 2 * _nbytes((SEQ, V_DIM), F32))
            + 2 * _nbytes((SEQ, V_DIM), BF16) + 10 * _nbytes((GROUP * bq, SEQ), F32))
    return pl.pallas_call(
        functools.partial(_attn_prompt_kernel, bq=bq),
        grid=(BATCH, N_KV_HEADS),
        in_specs=[
            pl.BlockSpec((SEQ, qw), lambda b, h: (b, h)),
            pl.BlockSpec((SEQ, 2 * HEAD_DIM), lambda b, h: (b, h)),
            pl.BlockSpec((SEQ, V_DIM), lambda b, h: (b, h)),
            pl.BlockSpec((4, HEAD_DIM), lambda b, h: (0, 0)),
            pl.BlockSpec((1, V_DIM), lambda b, h: (0, 0)),
        ],
        out_specs=pl.BlockSpec((SEQ, GROUP * V_DIM), lambda b, h: (b, h)),
        out_shape=jax.ShapeDtypeStruct((ROWS_P, N_HEADS * V_DIM), BF16),
        scratch_shapes=[pltpu.VMEM((SEQ, 2 * HEAD_DIM), BF16), pltpu.VMEM((SEQ, V_DIM), BF16)],
        compiler_params=_params(("arbitrary", "arbitrary"), vmem),
        name="attn_prompt",
    )(q, k, v, lamv, g_subln)


PAGES_PER_STEP = 8
QROWS = 2 * GROUP * DEC_SEQ
PAIRS = N_KV_HEADS // 2
PROWS = 2 * QROWS


def _attn_sample_kernel(pt_ref, qbd_ref, knew_ref, vnew_ref, lam_ref, g_ref, *rest):
    del pt_ref
    kp_refs = rest[:PAGES_PER_STEP]
    vp_refs = rest[PAGES_PER_STEP:3 * PAGES_PER_STEP]
    o_ref, m_s, l_s, acc_s = rest[3 * PAGES_PER_STEP:]
    j = pl.program_id(1)

    @pl.when(j == 0)
    def _():
        m_s[...] = jnp.full(m_s.shape, NEG, F32)
        l_s[...] = jnp.zeros(l_s.shape, F32)
        acc_s[...] = jnp.zeros(acc_s.shape, F32)

    def scores(k_tiles):
        cols = []
        for kt in k_tiles:
            rows = [lax.dot_general(qbd_ref[0, hp], kt(hp), _NT, preferred_element_type=F32)
                    for hp in range(PAIRS)]
            cols.append(jnp.concatenate(rows, axis=0))
        return cols[0] if len(cols) == 1 else jnp.concatenate(cols, axis=1)

    def update(s, v_tiles, width):
        m_prev = m_s[...]
        m_new = jnp.maximum(m_prev, jnp.max(s, axis=-1, keepdims=True))
        alpha = jnp.exp(m_prev - m_new)
        p = jnp.exp(s - m_new)
        l_s[...] = alpha * l_s[...] + jnp.sum(p, axis=-1, keepdims=True)
        pb = p.astype(BF16)
        pv = []
        for hp in range(PAIRS):
            acc = None
            for i, vt in enumerate(v_tiles):
                d = jnp.dot(pb[hp * PROWS:(hp + 1) * PROWS, i * width:(i + 1) * width], vt(hp),
                            preferred_element_type=F32)
                acc = d if acc is None else acc + d
            pv.append(acc)
        acc_s[...] = alpha * acc_s[...] + jnp.concatenate(pv, axis=0)
        m_s[...] = m_new

    def k_page(ref):
        def get(hp):
            maps = [ref[pl.ds(2 * hp + c, 2 * PAGE_SIZE, stride=N_KV_HEADS), :] for c in range(2)]
            return jnp.concatenate(maps, axis=1).astype(BF16)
        return get

    def v_page(ref_lo, ref_hi):
        def get(hp):
            halves = [r[pl.ds(hp, 2 * PAGE_SIZE, stride=PAIRS), :] for r in (ref_lo, ref_hi)]
            return jnp.concatenate(halves, axis=1).astype(BF16)
        return get

    def which_head(shape):
        return (lax.broadcasted_iota(jnp.int32, shape, 0) >> int(math.log2(QROWS))) & 1

    s = scores([k_page(r) for r in kp_refs])
    col = lax.broadcasted_iota(jnp.int32, s.shape, 1)
    s = jnp.where((col & 1) == which_head(s.shape), s, NEG)
    update(s, [v_page(vp_refs[2 * p], vp_refs[2 * p + 1]) for p in range(PAGES_PER_STEP)], 2 * PAGE_SIZE)

    @pl.when(j == pl.num_programs(1) - 1)
    def _():
        pad = jnp.zeros((PAGE_SIZE - 2 * DEC_SEQ, 2 * HEAD_DIM), F32)

        def new_tile(ref):
            def get(hp):
                parts = [ref[:, h * V_DIM:(h + 1) * V_DIM] for h in (hp, hp + PAIRS)]
                return jnp.concatenate(parts + [pad], axis=0).astype(BF16)
            return get

        s = scores([new_tile(knew_ref)])
        t = lax.broadcasted_iota(jnp.int32, s.shape, 0) & (DEC_SEQ - 1)
        col = lax.broadcasted_iota(jnp.int32, s.shape, 1)
        ok = ((col >> int(math.log2(DEC_SEQ))) == which_head(s.shape)) & ((col & (DEC_SEQ - 1)) <= t)
        update(jnp.where(ok, s, NEG), [new_tile(vnew_ref)], PAGE_SIZE)

        lam = _lambda_value(lam_ref[...])
        o = acc_s[...] / l_s[...]
        half = GROUP * DEC_SEQ
        for hp in range(PAIRS):
            for hi in range(2):
                r0 = hp * PROWS + hi * QROWS
                d = o[r0:r0 + half] - lam * o[r0 + half:r0 + QROWS]
                y = _head_norm(d, g_ref[...])
                for g in range(GROUP):
                    c0 = ((hi * PAIRS + hp) * GROUP + g) * V_DIM
                    o_ref[:, c0:c0 + V_DIM] = y[g * DEC_SEQ:(g + 1) * DEC_SEQ]


def _attn_sample(q_s, k_s, v_s, cache_k, cache_v, page_table, lamv, g_subln):
    q6 = q_s.reshape(DEC_BATCH, DEC_SEQ, N_KV_HEADS, GROUP, 2, HEAD_DIM).transpose(0, 2, 4, 3, 1, 5)
    zero = jnp.zeros_like(q6[:, :, 0])
    top = jnp.concatenate([q6[:, :, 0], zero], axis=-1).reshape(DEC_BATCH, N_KV_HEADS, QROWS // 2, 2 * HEAD_DIM)
    bot = jnp.concatenate([zero, q6[:, :, 1]], axis=-1).reshape(DEC_BATCH, N_KV_HEADS, QROWS // 2, 2 * HEAD_DIM)
    qbd = jnp.concatenate([top, bot], axis=2)
    qbd = qbd.reshape(DEC_BATCH, 2, PAIRS, QROWS, 2 * HEAD_DIM).transpose(0, 2, 1, 3, 4).reshape(
        DEC_BATCH, PAIRS, PROWS, 2 * HEAD_DIM)
    n_pool = cache_k.shape[1]
    ck = cache_k.reshape(n_pool, PAGE_SIZE * N_KV_HEADS * 2, HEAD_DIM)
    cv = cache_v.reshape(n_pool, PAGE_SIZE * N_KV_HEADS, V_DIM)
    pps = PAGES_PER_STEP

    def k_spec(p):
        return pl.BlockSpec((None, ck.shape[1], HEAD_DIM), lambda b, j, pt: (pt[b, j * pps + p], 0, 0))

    def v_spec(p, half):
        return pl.BlockSpec((None, cv.shape[1], HEAD_DIM), lambda b, j, pt: (pt[b, j * pps + p], 0, half))

    in_specs = [
        pl.BlockSpec((1, PAIRS, PROWS, 2 * HEAD_DIM), lambda b, j, pt: (b, 0, 0, 0)),
        pl.BlockSpec((DEC_SEQ, K_W), lambda b, j, pt: (b, 0)),
        pl.BlockSpec((DEC_SEQ, V_W), lambda b, j, pt: (b, 0)),
        pl.BlockSpec((4, HEAD_DIM), lambda b, j, pt: (0, 0)),
        pl.BlockSpec((1, V_DIM), lambda b, j, pt: (0, 0)),
    ] + [k_spec(p) for p in range(pps)] + [v_spec(p, half) for p in range(pps) for half in range(2)]
    rows = N_KV_HEADS * QROWS
    vmem = (2 * 2 * pps * _nbytes((PAGE_SIZE, K_W), F32) + 2 * pps * _nbytes((PAGE_SIZE, K_W), BF16)
            + 6 * _nbytes((rows, pps * 2 * PAGE_SIZE), F32) + 4 * _nbytes((rows, V_DIM), F32)
            + 4 * _nbytes((DEC_SEQ, Q_W), F32))
    return pl.pallas_call(
        _attn_sample_kernel,
        grid_spec=pltpu.PrefetchScalarGridSpec(
            num_scalar_prefetch=1,
            grid=(DEC_BATCH, N_PAGES // pps),
            in_specs=in_specs,
            out_specs=pl.BlockSpec((DEC_SEQ, N_HEADS * V_DIM), lambda b, j, pt: (b, 0)),
            scratch_shapes=[pltpu.VMEM((rows, 1), F32), pltpu.VMEM((rows, 1), F32),
                            pltpu.VMEM((rows, V_DIM), F32)],
        ),
        out_shape=jax.ShapeDtypeStruct((ROWS_S, N_HEADS * V_DIM), F32),
        compiler_params=_params(("arbitrary", "arbitrary"), vmem),
        name="attn_sample",
    )(page_table, qbd, k_s, v_s, lamv, g_subln, *([ck] * pps), *([cv] * (2 * pps)))


def _sgu_kernel(u_ref, v_ref, w_ref, b_ref, g_ref, yb_ref, vn_ref, *, t, last_only):
    v = v_ref[...]
    ms = jnp.mean(v * v, axis=-1, keepdims=True)
    vn = v * lax.rsqrt(ms + EPS) * g_ref[...]
    if last_only:
        @pl.when(pl.program_id(1) == pl.num_programs(1) - 1)
        def _():
            vn_ref[...] = vn
    else:
        vn_ref[...] = vn
    r = v.shape[0]
    row = lax.broadcasted_iota(jnp.int32, (r, r), 0)
    col = lax.broadcasted_iota(jnp.int32, (r, r), 1)
    keep = (col <= row) & (col >= row - (row & (t - 1)))
    for g in range(N_SGU_GROUPS):
        cs = slice(g * SGU_GROUP_W, (g + 1) * SGU_GROUP_W)
        w = jnp.where(keep, w_ref[g], 0.0).astype(BF16)
        mixed = jnp.dot(w, vn[:, cs].astype(BF16), preferred_element_type=F32) + b_ref[g]
        yb_ref[:, cs] = (u_ref[:, cs] * mixed).astype(yb_ref.dtype)


def _sgu_prompt(uv, w_spatial, b_spatial, g_sgu):
    nch = SEQ // CHUNK
    vmem = 2 * (3 * _nbytes((CHUNK, SGU_W), F32) + _nbytes((CHUNK, SGU_W), BF16)) + 6 * _nbytes((CHUNK, SGU_W), F32)
    return pl.pallas_call(
        functools.partial(_sgu_kernel, t=CHUNK, last_only=True),
        grid=(BATCH, nch),
        in_specs=[
            pl.BlockSpec((CHUNK, SGU_W), lambda b, c: (b * nch + c, 0)),
            pl.BlockSpec((CHUNK, SGU_W), lambda b, c: (b * nch + c, 1)),
            pl.BlockSpec((N_SGU_GROUPS, CHUNK, CHUNK), lambda b, c: (0, 0, 0)),
            pl.BlockSpec((N_SGU_GROUPS, CHUNK, 1), lambda b, c: (0, 0, 0)),
            pl.BlockSpec((1, SGU_W), lambda b, c: (0, 0)),
        ],
        out_specs=[pl.BlockSpec((CHUNK, SGU_W), lambda b, c: (b * nch + c, 0)),
                   pl.BlockSpec((None, CHUNK, SGU_W), lambda b, c: (b, 0, 0))],
        out_shape=[jax.ShapeDtypeStruct((ROWS_P, SGU_W), BF16),
                   jax.ShapeDtypeStruct((BATCH, CHUNK, SGU_W), F32)],
        compiler_params=_params(("arbitrary", "arbitrary"), vmem),
        name="sgu_prompt",
    )(uv, uv, w_spatial, b_spatial[:, :CHUNK, None], g_sgu.reshape(1, SGU_W))


def _sgu_sample(uv, w_spatial, b_spatial, g_sgu):
    reps = ROWS_S // DEC_SEQ
    w = jnp.tile(w_spatial[:, :DEC_SEQ, :DEC_SEQ], (1, reps, reps))
    b = jnp.tile(b_spatial[:, :DEC_SEQ], (1, reps))[:, :, None]
    vmem = 2 * (4 * _nbytes((ROWS_S, SGU_W), F32)) + 6 * _nbytes((ROWS_S, SGU_W), F32)
    return pl.pallas_call(
        functools.partial(_sgu_kernel, t=DEC_SEQ, last_only=False),
        grid=(1, 1),
        in_specs=[
            pl.BlockSpec((ROWS_S, SGU_W), lambda b, c: (0, 0)),
            pl.BlockSpec((ROWS_S, SGU_W), lambda b, c: (0, 1)),
            pl.BlockSpec((N_SGU_GROUPS, ROWS_S, ROWS_S), lambda b, c: (0, 0, 0)),
            pl.BlockSpec((N_SGU_GROUPS, ROWS_S, 1), lambda b, c: (0, 0, 0)),
            pl.BlockSpec((1, SGU_W), lambda b, c: (0, 0)),
        ],
        out_specs=[pl.BlockSpec((ROWS_S, SGU_W), lambda b, c: (0, 0)),
                   pl.BlockSpec((ROWS_S, SGU_W), lambda b, c: (0, 0))],
        out_shape=[jax.ShapeDtypeStruct((ROWS_S, SGU_W), BF16),
                   jax.ShapeDtypeStruct((ROWS_S, SGU_W), F32)],
        compiler_params=_params(("arbitrary", "arbitrary"), vmem),
        name="sgu_sample",
    )(uv, uv, w, b, g_sgu.reshape(1, SGU_W))


TAIL = 8


def _ffn_kernel(xp_ref, xs_ref, wg_ref, wu_ref, cw_ref, cb_ref, p1_ref, p2_ref,
                midp_ref, mids_ref, tail_ref, as_ref, wbg, wbu, carry):
    m = pl.program_id(1)

    @pl.when(m == 0)
    def _():
        _cast_weight(wg_ref, wbg)
        _cast_weight(wu_ref, wbu)

    def act(a, sh1, sh2, up):
        cw = cw_ref[...]
        c = cb_ref[...] + ((cw[0:1] * sh2 + cw[1:2] * sh1) + cw[2:3] * a)
        return jax.nn.silu(c) * up

    @pl.when(m < N_MP)
    def _():
        x = xp_ref[...]
        a = jnp.dot(x, wbg[...], preferred_element_type=F32)
        up = jnp.dot(x, wbu[...], preferred_element_type=F32)

        @pl.when(m % (SEQ // BM) == 0)
        def _():
            carry[...] = jnp.zeros(carry.shape, F32)

        prev = carry[...]
        row = lax.broadcasted_iota(jnp.int32, a.shape, 0)
        sh1 = jnp.where(row == 0, prev[TAIL - 1:TAIL], pltpu.roll(a, 1, 0))
        sh2 = jnp.where(row == 0, prev[TAIL - 2:TAIL - 1],
                        jnp.where(row == 1, prev[TAIL - 1:TAIL], pltpu.roll(a, 2, 0)))
        midp_ref[...] = act(a, sh1, sh2, up).astype(midp_ref.dtype)
        carry[...] = a[BM - TAIL:]
        tail_ref[...] = a[BM - TAIL:]

    @pl.when(m == N_MP)
    def _():
        x = xs_ref[...]
        a = jnp.dot(x, wbg[...], preferred_element_type=F32)
        up = jnp.dot(x, wbu[...], preferred_element_type=F32)
        t = lax.broadcasted_iota(jnp.int32, a.shape, 0) & (DEC_SEQ - 1)
        sh1 = jnp.where(t == 0, p1_ref[...], pltpu.roll(a, 1, 0))
        sh2 = jnp.where(t < 2, p2_ref[...], pltpu.roll(a, 2, 0))
        mids_ref[...] = act(a, sh1, sh2, up).astype(mids_ref.dtype)
        as_ref[...] = a


def _ffn_gate_up(hp, hs, w_gate, w_up, conv_w, conv_b, state):
    bn = 256
    prev1 = jnp.pad(state[:, 1:2], ((0, 0), (0, DEC_SEQ - 1), (0, 0))).reshape(ROWS_S, D_FF)
    prev2 = jnp.pad(state, ((0, 0), (0, DEC_SEQ - 2), (0, 0))).reshape(ROWS_S, D_FF)
    tiles_per_seq = SEQ // BM
    pm = lambda n, m: (jnp.minimum(m, N_MP - 1), 0)
    vmem = (2 * (_nbytes((BM + ROWS_S, D_MODEL), BF16))
            + 2 * (2 * _nbytes((D_MODEL, bn), F32) + _nbytes((D_MODEL, bn), BF16))
            + 2 * 4 * _nbytes((BM + ROWS_S, bn), F32) + 10 * _nbytes((BM, bn), F32))
    return pl.pallas_call(
        _ffn_kernel,
        grid=(D_FF // bn, N_MP + 1),
        in_specs=[
            pl.BlockSpec((BM, D_MODEL), pm),
            pl.BlockSpec((ROWS_S, D_MODEL), lambda n, m: (0, 0)),
            pl.BlockSpec((D_MODEL, bn), lambda n, m: (0, n)),
            pl.BlockSpec((D_MODEL, bn), lambda n, m: (0, n)),
            pl.BlockSpec((CONV_W, bn), lambda n, m: (0, n)),
            pl.BlockSpec((1, bn), lambda n, m: (0, n)),
            pl.BlockSpec((ROWS_S, bn), lambda n, m: (0, n)),
            pl.BlockSpec((ROWS_S, bn), lambda n, m: (0, n)),
        ],
        out_specs=[
            pl.BlockSpec((BM, bn), lambda n, m: (jnp.minimum(m, N_MP - 1), n)),
            pl.BlockSpec((ROWS_S, bn), lambda n, m: (0, n)),
            pl.BlockSpec((None, TAIL, bn), lambda n, m: (jnp.minimum(m, N_MP - 1) // tiles_per_seq, 0, n)),
            pl.BlockSpec((ROWS_S, bn), lambda n, m: (0, n)),
        ],
        out_shape=[
            jax.ShapeDtypeStruct((ROWS_P, D_FF), BF16),
            jax.ShapeDtypeStruct((ROWS_S, D_FF), BF16),
            jax.ShapeDtypeStruct((BATCH, TAIL, D_FF), F32),
            jax.ShapeDtypeStruct((ROWS_S, D_FF), F32),
        ],
        scratch_shapes=[pltpu.VMEM((D_MODEL, bn), BF16), pltpu.VMEM((D_MODEL, bn), BF16),
                        pltpu.VMEM((TAIL, bn), F32)],
        compiler_params=_params(("arbitrary", "arbitrary"), vmem),
        name="ffn_gate_up",
    )(hp, hs, w_gate, w_up, conv_w, conv_b.reshape(1, D_FF), prev1, prev2)


def _cast_kernel(x_ref, o_ref):
    o_ref[...] = x_ref[...].astype(o_ref.dtype)


def _cast_bf16(w, name):
    k, n = w.shape
    rows = 256
    vmem = 2 * (_nbytes((rows, n), F32) + _nbytes((rows, n), BF16)) + _nbytes((rows, n), F32)
    return pl.pallas_call(
        _cast_kernel,
        grid=(k // rows,),
        in_specs=[pl.BlockSpec((rows, n), lambda i: (i, 0))],
        out_specs=pl.BlockSpec((rows, n), lambda i: (i, 0)),
        out_shape=jax.ShapeDtypeStruct((k, n), BF16),
        compiler_params=_params(("arbitrary",), vmem),
        name=name,
    )(w)


def _down_kernel(x_ref, w_ref, r_ref, o_ref):
    o_ref[...] = r_ref[...] + jnp.dot(x_ref[...], w_ref[...], preferred_element_type=F32)


def _ffn_down(mid, wb, res, bm, name):
    rows = mid.shape[0]
    bn = 512
    vmem = (2 * (_nbytes((bm, D_FF), BF16) + _nbytes((D_FF, bn), BF16)) + 4 * _nbytes((bm, bn), F32)
            + 2 * _nbytes((bm, bn), F32))
    return pl.pallas_call(
        _down_kernel,
        grid=(rows // bm, D_MODEL // bn),
        in_specs=[
            pl.BlockSpec((bm, D_FF), lambda m, n: (m, 0)),
            pl.BlockSpec((D_FF, bn), lambda m, n: (0, n)),
            pl.BlockSpec((bm, bn), lambda m, n: (m, n)),
        ],
        out_specs=pl.BlockSpec((bm, bn), lambda m, n: (m, n)),
        out_shape=jax.ShapeDtypeStruct((rows, D_MODEL), F32),
        compiler_params=_params(("arbitrary", "arbitrary"), vmem),
        name=name,
    )(mid, wb, res)


def kernel(x_prompt, x_sample, cache_k, cache_v, state_conv, page_table, g_attn, w_in, lam_q1, lam_k1, lam_q2, lam_k2, g_subln, g_sgu, w_spatial, b_spatial, w_o_a, w_o_b, w_out, g_ffn, w_gate, conv_w, conv_b, w_up, w_down, g_final):
    xp = x_prompt.reshape(ROWS_P, D_MODEL)
    xs = x_sample.reshape(ROWS_S, D_MODEL)
    w_in0 = w_in[0]
    lamv = jnp.stack([lam_q1[0], lam_k1[0], lam_q2[0], lam_k2[0]])
    gsub = g_subln[0].reshape(1, V_DIM)

    hp = _rmsnorm_rows(xp, g_attn[0], BF16, "rms_attn_p")
    hs = _rmsnorm_rows(xs, g_attn[0], BF16, "rms_attn_s")

    bn = 512
    one = lambda f: (lambda accs, ex: [f(accs[0])])
    (q_p,), (q_s,) = _fused_matmul(hp, hs, [w_in0], [0], Q_W, bn, one(lambda a: a * SCALE), [BF16], name="proj_q")
    (k_p,), (k_s,) = _fused_matmul(hp, hs, [w_in0], [Q_W // bn], K_W, bn, one(lambda a: a), [F32], name="proj_k",
                                   x_split=2)
    (v_p,), (v_s,) = _fused_matmul(hp, hs, [w_in0], [(Q_W + K_W) // bn], V_W, bn, one(lambda a: a), [F32],
                                   name="proj_v")
    (uv_p,), (uv_s,) = _fused_matmul(hp, hs, [w_in0], [(Q_W + K_W + V_W) // bn], 2 * SGU_W, bn,
                                     one(_gelu_exact), [F32], name="proj_uv")
    (gt_p,), (gt_s,) = _fused_matmul(hp, hs, [w_in0], [(Q_W + K_W + V_W + 2 * SGU_W) // bn], 2 * D_MODEL, bn,
                                     one(jax.nn.sigmoid), [F32], name="proj_gates", x_split=4)

    ya_p = _attn_prompt(q_p, k_p, v_p, lamv, gsub)
    ya_s = _attn_sample(q_s, k_s, v_s, cache_k, cache_v, page_table, lamv, gsub)

    yb_p, vn_p = _sgu_prompt(uv_p, w_spatial[0], b_spatial[0], g_sgu[0])
    yb_s, vn_s = _sgu_sample(uv_s, w_spatial[0], b_spatial[0], g_sgu[0])

    (ta_p,), (ta_s,) = _fused_matmul(ya_p, ya_s, [w_o_a[0]], [0], D_MODEL, bn,
                                     lambda accs, ex: [ex[0] * accs[0]], [F32],
                                     extras=[(gt_p, gt_s)], ex_off=[0], name="merge_a", bm=512)
    (z_p,), (z_s,) = _fused_matmul(yb_p, yb_s, [w_o_b[0]], [0], D_MODEL, bn,
                                   lambda accs, ex: [ex[1] + ex[0] * accs[0]], [BF16],
                                   extras=[(gt_p, gt_s), (ta_p, ta_s)], ex_off=[D_MODEL // bn, 0], name="merge_b")
    (x1_p,), (x1_s,) = _fused_matmul(z_p, z_s, [w_out[0]], [0], D_MODEL, bn,
                                     lambda accs, ex: [ex[0] + accs[0]], [F32],
                                     extras=[(xp, xs)], ex_off=[0], name="out_proj")

    h2_p = _rmsnorm_rows(x1_p, g_ffn[0], BF16, "rms_ffn_p")
    h2_s = _rmsnorm_rows(x1_s, g_ffn[0], BF16, "rms_ffn_s")
    mid_p, mid_s, tail_p, a_s = _ffn_gate_up(h2_p, h2_s, w_gate[0], w_up[0], conv_w[0], conv_b[0], state_conv[0])
    wdb = _cast_bf16(w_down[0], "cast_w_down")
    x2_p = _ffn_down(mid_p, wdb, x1_p, 512, "ffn_down_p")
    x2_s = _ffn_down(mid_s, wdb, x1_s, ROWS_S, "ffn_down_s")

    y_p = _rmsnorm_rows(x2_p, g_final, F32, "rms_final_p")
    y_s = _rmsnorm_rows(x2_s, g_final, F32, "rms_final_s")

    return (
        y_p.reshape(BATCH, SEQ, D_MODEL),
        y_s.reshape(DEC_BATCH, DEC_SEQ, D_MODEL),
        k_p.reshape(1, BATCH, SEQ, N_KV_HEADS, 2, HEAD_DIM),
        v_p.reshape(1, BATCH, SEQ, N_KV_HEADS, V_DIM),
        k_s.reshape(1, DEC_BATCH, DEC_SEQ, N_KV_HEADS, 2, HEAD_DIM),
        v_s.reshape(1, DEC_BATCH, DEC_SEQ, N_KV_HEADS, V_DIM),
        vn_p.reshape(1, BATCH, CHUNK, SGU_W),
        vn_s.reshape(1, DEC_BATCH, DEC_SEQ, SGU_W),
        tail_p[:, TAIL - (CONV_W - 1):].reshape(1, BATCH, CONV_W - 1, D_FF),
        a_s.reshape(DEC_BATCH, DEC_SEQ, D_FF)[:, DEC_SEQ - (CONV_W - 1):].reshape(1, DEC_BATCH, CONV_W - 1, D_FF),
    )
```

```python
import functools
import math

import jax
import jax.numpy as jnp
from jax import lax
from jax.experimental import pallas as pl
from jax.experimental.pallas import tpu as pltpu

D_MODEL = 4096
BATCH = 4
SEQ = 2048
DEC_BATCH = 32
DEC_SEQ = 8
PAST_LEN = 8192
PAGE_SIZE = 128
N_HEADS = 16
N_KV_HEADS = 8
GROUP = N_HEADS // N_KV_HEADS
HEAD_DIM = D_MODEL // (2 * N_HEADS)
V_DIM = 2 * HEAD_DIM
Q_W = N_HEADS * 2 * HEAD_DIM
K_W = N_KV_HEADS * 2 * HEAD_DIM
V_W = N_KV_HEADS * V_DIM
SGU_W = D_MODEL
N_SGU_GROUPS = 8
SGU_GROUP_W = SGU_W // N_SGU_GROUPS
CHUNK = 128
D_FF = 11008
CONV_W = 3
SCALE = 1.0 / math.sqrt(HEAD_DIM)
NEG = -1e30
EPS = 1e-6
LAMBDA_INIT = 0.8 - 0.6 * math.exp(-0.3 * 0)

ROWS_P = BATCH * SEQ
ROWS_S = DEC_BATCH * DEC_SEQ
N_PAGES = PAST_LEN // PAGE_SIZE

V7X_VMEM_BYTES = 64 * 1024 * 1024
VMEM_CAP_BYTES = 58 * 1024 * 1024

BM = 1024
N_MP = ROWS_P // BM

F32 = jnp.float32
BF16 = jnp.bfloat16
_NT = (((1,), (1,)), ((), ()))


def _nbytes(shape, dtype):
    return math.prod(shape) * jnp.dtype(dtype).itemsize


def _params(sem, vmem_bytes):
    return pltpu.CompilerParams(dimension_semantics=sem,
                                vmem_limit_bytes=int(min(vmem_bytes, VMEM_CAP_BYTES)))


def _rms_kernel(x_ref, g_ref, o_ref):
    x = x_ref[...].astype(F32)
    ms = jnp.mean(x * x, axis=-1, keepdims=True)
    o_ref[...] = (x * lax.rsqrt(ms + EPS) * g_ref[...]).astype(o_ref.dtype)


def _rmsnorm_rows(x, g, out_dtype, name):
    n, d = x.shape
    rows = min(n, 256)
    vmem = 2 * (_nbytes((rows, d), x.dtype) + _nbytes((rows, d), out_dtype)) + 4 * _nbytes((rows, d), F32)
    return pl.pallas_call(
        _rms_kernel,
        grid=(n // rows,),
        in_specs=[pl.BlockSpec((rows, d), lambda i: (i, 0)), pl.BlockSpec((1, d), lambda i: (0, 0))],
        out_specs=pl.BlockSpec((rows, d), lambda i: (i, 0)),
        out_shape=jax.ShapeDtypeStruct((n, d), out_dtype),
        compiler_params=_params(("arbitrary",), vmem),
        name=name,
    )(x, g.reshape(1, d))


def _cast_weight(w_ref, wb_ref):
    k_dim = w_ref.shape[0]
    chunk = 512

    def body(i, carry):
        r = pl.multiple_of(i * chunk, chunk)
        wb_ref[pl.ds(r, chunk), :] = w_ref[pl.ds(r, chunk), :].astype(BF16)
        return carry

    lax.fori_loop(0, k_dim // chunk, body, 0)


def _mm_kernel(*refs, n_x, n_w, n_ex, n_out, n_mp, epilogue):
    it = iter(refs)
    xp_refs = [next(it) for _ in range(n_x)]
    xs_ref = next(it)
    w_refs = [next(it) for _ in range(n_w)]
    exp_refs = [next(it) for _ in range(n_ex)]
    exs_refs = [next(it) for _ in range(n_ex)]
    outp_refs = [next(it) for _ in range(n_out)]
    outs_refs = [next(it) for _ in range(n_out)]
    wb_refs = [next(it) for _ in range(n_w)]
    m = pl.program_id(1)

    @pl.when(m == 0)
    def _():
        for w_ref, wb_ref in zip(w_refs, wb_refs):
            _cast_weight(w_ref, wb_ref)

    def run(x_refs, ex_refs, out_refs):
        kp = wb_refs[0].shape[0] // len(x_refs)
        accs = []
        for wb in wb_refs:
            acc = None
            for i, x_ref in enumerate(x_refs):
                d = jnp.dot(x_ref[...].astype(BF16), wb[i * kp:(i + 1) * kp, :], preferred_element_type=F32)
                acc = d if acc is None else acc + d
            accs.append(acc)
        vals = epilogue(accs, [r[...] for r in ex_refs])
        for o_ref, v in zip(out_refs, vals):
            o_ref[...] = v.astype(o_ref.dtype)

    @pl.when(m < n_mp)
    def _():
        run(xp_refs, exp_refs, outp_refs)

    @pl.when(m == n_mp)
    def _():
        run([xs_ref], exs_refs, outs_refs)


def _fused_matmul(xp, xs, ws, w_off, n_cols, bn, epilogue, out_dtypes, extras=(), ex_off=(), name="mm",
                  bm=BM, x_split=1):
    k_dim = xp.shape[1]
    n_mp = ROWS_P // bm
    n_w, n_ex, n_out = len(ws), len(extras), len(out_dtypes)
    row = lambda m: jnp.minimum(m, n_mp - 1)
    in_specs = [pl.BlockSpec((bm, k_dim // x_split), lambda n, m, i=i: (row(m), i)) for i in range(x_split)]
    in_specs.append(pl.BlockSpec((ROWS_S, k_dim), lambda n, m: (0, 0)))
    for off in w_off:
        in_specs.append(pl.BlockSpec((k_dim, bn), lambda n, m, off=off: (0, n + off)))
    for off in ex_off:
        in_specs.append(pl.BlockSpec((bm, bn), lambda n, m, off=off: (row(m), n + off)))
    for off in ex_off:
        in_specs.append(pl.BlockSpec((ROWS_S, bn), lambda n, m, off=off: (0, n + off)))
    out_specs = ([pl.BlockSpec((bm, bn), lambda n, m: (row(m), n))] * n_out
                 + [pl.BlockSpec((ROWS_S, bn), lambda n, m: (0, n))] * n_out)
    out_shape = ([jax.ShapeDtypeStruct((ROWS_P, n_cols), dt) for dt in out_dtypes]
                 + [jax.ShapeDtypeStruct((ROWS_S, n_cols), dt) for dt in out_dtypes])
    vmem = (2 * (_nbytes((bm, k_dim), xp.dtype) + _nbytes((ROWS_S, k_dim), xs.dtype))
            + n_w * (2 * _nbytes((k_dim, bn), F32) + _nbytes((k_dim, bn), BF16))
            + 2 * (n_ex + n_out) * _nbytes((bm + ROWS_S, bn), F32)
            + (n_w + 1 + x_split) * _nbytes((bm, bn), F32))
    args = [xp] * x_split + [xs] + list(ws) + [e[0] for e in extras] + [e[1] for e in extras]
    res = pl.pallas_call(
        functools.partial(_mm_kernel, n_x=x_split, n_w=n_w, n_ex=n_ex, n_out=n_out, n_mp=n_mp, epilogue=epilogue),
        grid=(n_cols // bn, n_mp + 1),
        in_specs=in_specs,
        out_specs=out_specs,
        out_shape=out_shape,
        scratch_shapes=[pltpu.VMEM((k_dim, bn), BF16) for _ in range(n_w)],
        compiler_params=_params(("arbitrary", "arbitrary"), vmem),
        name=name,
    )(*args)
    return res[:n_out], res[n_out:]


def _mms_kernel(*refs, n_ex, n_out, epilogue):
    it = iter(refs)
    xp_ref, xs_ref, w_ref = next(it), next(it), next(it)
    exp_refs = [next(it) for _ in range(n_ex)]
    exs_refs = [next(it) for _ in range(n_ex)]
    outp_refs = [next(it) for _ in range(n_out)]
    outs_refs = [next(it) for _ in range(n_out)]

    def run(x_ref, ex_refs, out_refs):
        acc = jnp.dot(x_ref[...].astype(BF16), w_ref[...], preferred_element_type=F32)
        vals = epilogue([acc], [r[...] for r in ex_refs])
        for o_ref, v in zip(out_refs, vals):
            o_ref[...] = v.astype(o_ref.dtype)

    run(xp_ref, exp_refs, outp_refs)

    @pl.when(pl.program_id(0) == N_MP - 1)
    def _():
        run(xs_ref, exs_refs, outs_refs)


def _stream_matmul(xp, xs, wb, n_cols, bn, epilogue, out_dtypes, extras=(), ex_off=(), name="mms"):
    k_dim = xp.shape[1]
    n_ex, n_out = len(extras), len(out_dtypes)
    last = lambda m, n: jnp.where(m == N_MP - 1, n, 0)
    in_specs = [pl.BlockSpec((BM, k_dim), lambda m, n: (m, 0)),
                pl.BlockSpec((ROWS_S, k_dim), lambda m, n: (0, 0)),
                pl.BlockSpec((k_dim, bn), lambda m, n: (0, n))]
    for off in ex_off:
        in_specs.append(pl.BlockSpec((BM, bn), lambda m, n, off=off: (m, n + off)))
    for off in ex_off:
        in_specs.append(pl.BlockSpec((ROWS_S, bn), lambda m, n, off=off: (0, last(m, n) + off)))
    out_specs = ([pl.BlockSpec((BM, bn), lambda m, n: (m, n))] * n_out
                 + [pl.BlockSpec((ROWS_S, bn), lambda m, n: (0, last(m, n)))] * n_out)
    out_shape = ([jax.ShapeDtypeStruct((ROWS_P, n_cols), dt) for dt in out_dtypes]
                 + [jax.ShapeDtypeStruct((ROWS_S, n_cols), dt) for dt in out_dtypes])
    vmem = (2 * (_nbytes((BM, k_dim), xp.dtype) + _nbytes((ROWS_S, k_dim), xs.dtype))
            + 2 * _nbytes((k_dim, bn), BF16)
            + 2 * (n_ex + n_out) * _nbytes((BM + ROWS_S, bn), F32)
            + 3 * _nbytes((BM, bn), F32))
    args = [xp, xs, wb] + [e[0] for e in extras] + [e[1] for e in extras]
    res = pl.pallas_call(
        functools.partial(_mms_kernel, n_ex=n_ex, n_out=n_out, epilogue=epilogue),
        grid=(N_MP, n_cols // bn),
        in_specs=in_specs,
        out_specs=out_specs,
        out_shape=out_shape,
        compiler_params=_params(("arbitrary", "arbitrary"), vmem),
        name=name,
    )(*args)
    return res[:n_out], res[n_out:]


def _gelu_exact(x):
    return 0.5 * x * (1.0 + lax.erf(x * math.sqrt(0.5)))


def _lambda_value(lv):
    a = jnp.sum(lv[0:1] * lv[1:2], axis=-1, keepdims=True)
    b = jnp.sum(lv[2:3] * lv[3:4], axis=-1, keepdims=True)
    return jnp.exp(a) - jnp.exp(b) + LAMBDA_INIT


def _head_norm(o, g):
    ms = jnp.mean(o * o, axis=-1, keepdims=True)
    return o * lax.rsqrt(ms + EPS) * g * (1.0 - LAMBDA_INIT)


def _attn_prompt_kernel(q_ref, k_ref, v_ref, lam_ref, g_ref, o_ref, kb, vb, *, bq):
    def cast_rows(i, carry):
        r = pl.multiple_of(i * 256, 256)
        kb[pl.ds(r, 256), :] = k_ref[pl.ds(r, 256), :].astype(BF16)
        vb[pl.ds(r, 256), :] = v_ref[pl.ds(r, 256), :].astype(BF16)
        return carry
    lax.fori_loop(0, SEQ // 256, cast_rows, 0)

    lam = _lambda_value(lam_ref[...])
    row = lax.broadcasted_iota(jnp.int32, (GROUP * bq, bq), 0) & (bq - 1)
    col = lax.broadcasted_iota(jnp.int32, (GROUP * bq, bq), 1)
    causal = col <= row
    for qi in range(SEQ // bq):
        keys = (qi + 1) * bq
        q = q_ref[qi * bq:(qi + 1) * bq, :]
        o_maps = []
        for c in range(2):
            qc = jnp.concatenate([q[:, (g * 2 + c) * HEAD_DIM:(g * 2 + c + 1) * HEAD_DIM] for g in range(GROUP)],
                                 axis=0)
            s = lax.dot_general(qc, kb[0:keys, c * HEAD_DIM:(c + 1) * HEAD_DIM], _NT, preferred_element_type=F32)
            diag = jnp.where(causal, s[:, keys - bq:], NEG)
            s = diag if qi == 0 else jnp.concatenate([s[:, :keys - bq], diag], axis=1)
            p = jnp.exp(s - jnp.max(s, axis=-1, keepdims=True))
            l = jnp.sum(p, axis=-1, keepdims=True)
            o_maps.append(jnp.dot(p.astype(BF16), vb[0:keys, :], preferred_element_type=F32) / l)
        o = o_maps[0] - lam * o_maps[1]
        for g in range(GROUP):
            o_ref[qi * bq:(qi + 1) * bq, g * V_DIM:(g + 1) * V_DIM] = _head_norm(
                o[g * bq:(g + 1) * bq], g_ref[...]).astype(o_ref.dtype)


def _attn_prompt(q, k, v, lamv, g_subln):
    bq = 256
    qw = GROUP * 2 * HEAD_DIM
    vmem = (2 * (2 * _nbytes((SEQ, qw), BF16) + 2 * _nbytes((SEQ, V_DIM), F32))
            + 2 * _nbytes((SEQ, V_DIM), BF16) + 10 * _nbytes((GROUP * bq, SEQ), F32))
    return pl.pallas_call(
        functools.partial(_attn_prompt_kernel, bq=bq),
        grid=(BATCH, N_KV_HEADS),
        in_specs=[
            pl.BlockSpec((SEQ, qw), lambda b, h: (b, h)),
            pl.BlockSpec((SEQ, 2 * HEAD_DIM), lambda b, h: (b, h)),
            pl.BlockSpec((SEQ, V_DIM), lambda b, h: (b, h)),
            pl.BlockSpec((4, HEAD_DIM), lambda b, h: (0, 0)),
            pl.BlockSpec((1, V_DIM), lambda b, h: (0, 0)),
        ],
        out_specs=pl.BlockSpec((SEQ, GROUP * V_DIM), lambda b, h: (b, h)),
        out_shape=jax.ShapeDtypeStruct((ROWS_P, N_HEADS * V_DIM), BF16),
        scratch_shapes=[pltpu.VMEM((SEQ, 2 * HEAD_DIM), BF16), pltpu.VMEM((SEQ, V_DIM), BF16)],
        compiler_params=_params(("arbitrary", "arbitrary"), vmem),
        name="attn_prompt",
    )(q, k, v, lamv, g_subln)


PAGES_PER_STEP = 8
QROWS = 2 * GROUP * DEC_SEQ
PAIRS = N_KV_HEADS // 2
PROWS = 2 * QROWS


def _attn_sample_kernel(pt_ref, qbd_ref, knew_ref, vnew_ref, lam_ref, g_ref, *rest):
    del pt_ref
    kp_refs = rest[:PAGES_PER_STEP]
    vp_refs = rest[PAGES_PER_STEP:3 * PAGES_PER_STEP]
    o_ref, m_s, l_s, acc_s = rest[3 * PAGES_PER_STEP:]
    j = pl.program_id(1)

    @pl.when(j == 0)
    def _():
        m_s[...] = jnp.full(m_s.shape, NEG, F32)
        l_s[...] = jnp.zeros(l_s.shape, F32)
        acc_s[...] = jnp.zeros(acc_s.shape, F32)

    def scores(k_tiles):
        cols = []
        for kt in k_tiles:
            rows = [lax.dot_general(qbd_ref[0, hp], kt(hp), _NT, preferred_element_type=F32)
                    for hp in range(PAIRS)]
            cols.append(jnp.concatenate(rows, axis=0))
        return cols[0] if len(cols) == 1 else jnp.concatenate(cols, axis=1)

    def update(s, v_tiles, width):
        m_prev = m_s[...]
        m_new = jnp.maximum(m_prev, jnp.max(s, axis=-1, keepdims=True))
        alpha = jnp.exp(m_prev - m_new)
        p = jnp.exp(s - m_new)
        l_s[...] = alpha * l_s[...] + jnp.sum(p, axis=-1, keepdims=True)
        pb = p.astype(BF16)
        pv = []
        for hp in range(PAIRS):
            acc = None
            for i, vt in enumerate(v_tiles):
                d = jnp.dot(pb[hp * PROWS:(hp + 1) * PROWS, i * width:(i + 1) * width], vt(hp),
                            preferred_element_type=F32)
                acc = d if acc is None else acc + d
            pv.append(acc)
        acc_s[...] = alpha * acc_s[...] + jnp.concatenate(pv, axis=0)
        m_s[...] = m_new

    def k_page(ref):
        def get(hp):
            maps = [ref[pl.ds(2 * hp + c, 2 * PAGE_SIZE, stride=N_KV_HEADS), :] for c in range(2)]
            return jnp.concatenate(maps, axis=1).astype(BF16)
        return get

    def v_page(ref_lo, ref_hi):
        def get(hp):
            halves = [r[pl.ds(hp, 2 * PAGE_SIZE, stride=PAIRS), :] for r in (ref_lo, ref_hi)]
            return jnp.concatenate(halves, axis=1).astype(BF16)
        return get

    def which_head(shape):
        return (lax.broadcasted_iota(jnp.int32, shape, 0) >> int(math.log2(QROWS))) & 1

    s = scores([k_page(r) for r in kp_refs])
    col = lax.broadcasted_iota(jnp.int32, s.shape, 1)
    s = jnp.where((col & 1) == which_head(s.shape), s, NEG)
    update(s, [v_page(vp_refs[2 * p], vp_refs[2 * p + 1]) for p in range(PAGES_PER_STEP)], 2 * PAGE_SIZE)

    @pl.when(j == pl.num_programs(1) - 1)
    def _():
        pad = jnp.zeros((PAGE_SIZE - 2 * DEC_SEQ, 2 * HEAD_DIM), F32)

        def new_tile(ref):
            def get(hp):
                parts = [ref[:, h * V_DIM:(h + 1) * V_DIM] for h in (hp, hp + PAIRS)]
                return jnp.concatenate(parts + [pad], axis=0).astype(BF16)
            return get

        s = scores([new_tile(knew_ref)])
        t = lax.broadcasted_iota(jnp.int32, s.shape, 0) & (DEC_SEQ - 1)
        col = lax.broadcasted_iota(jnp.int32, s.shape, 1)
        ok = ((col >> int(math.log2(DEC_SEQ))) == which_head(s.shape)) & ((col & (DEC_SEQ - 1)) <= t)
        update(jnp.where(ok, s, NEG), [new_tile(vnew_ref)], PAGE_SIZE)

        lam = _lambda_value(lam_ref[...])
        o = acc_s[...] / l_s[...]
        half = GROUP * DEC_SEQ
        for hp in range(PAIRS):
            for hi in range(2):
                r0 = hp * PROWS + hi * QROWS
                d = o[r0:r0 + half] - lam * o[r0 + half:r0 + QROWS]
                y = _head_norm(d, g_ref[...])
                for g in range(GROUP):
                    c0 = ((hi * PAIRS + hp) * GROUP + g) * V_DIM
                    o_ref[:, c0:c0 + V_DIM] = y[g * DEC_SEQ:(g + 1) * DEC_SEQ]


def _attn_sample(q_s, k_s, v_s, cache_k, cache_v, page_table, lamv, g_subln):
    q6 = q_s.reshape(DEC_BATCH, DEC_SEQ, N_KV_HEADS, GROUP, 2, HEAD_DIM).transpose(0, 2, 4, 3, 1, 5)
    zero = jnp.zeros_like(q6[:, :, 0])
    top = jnp.concatenate([q6[:, :, 0], zero], axis=-1).reshape(DEC_BATCH, N_KV_HEADS, QROWS // 2, 2 * HEAD_DIM)
    bot = jnp.concatenate([zero, q6[:, :, 1]], axis=-1).reshape(DEC_BATCH, N_KV_HEADS, QROWS // 2, 2 * HEAD_DIM)
    qbd = jnp.concatenate([top, bot], axis=2)
    qbd = qbd.reshape(DEC_BATCH, 2, PAIRS, QROWS, 2 * HEAD_DIM).transpose(0, 2, 1, 3, 4).reshape(
        DEC_BATCH, PAIRS, PROWS, 2 * HEAD_DIM)
    n_pool = cache_k.shape[1]
    ck = cache_k.reshape(n_pool, PAGE_SIZE * N_KV_HEADS * 2, HEAD_DIM)
    cv = cache_v.reshape(n_pool, PAGE_SIZE * N_KV_HEADS, V_DIM)
    pps = PAGES_PER_STEP

    def k_spec(p):
        return pl.BlockSpec((None, ck.shape[1], HEAD_DIM), lambda b, j, pt: (pt[b, j * pps + p], 0, 0))

    def v_spec(p, half):
        return pl.BlockSpec((None, cv.shape[1], HEAD_DIM), lambda b, j, pt: (pt[b, j * pps + p], 0, half))

    in_specs = [
        pl.BlockSpec((1, PAIRS, PROWS, 2 * HEAD_DIM), lambda b, j, pt: (b, 0, 0, 0)),
        pl.BlockSpec((DEC_SEQ, K_W), lambda b, j, pt: (b, 0)),
        pl.BlockSpec((DEC_SEQ, V_W), lambda b, j, pt: (b, 0)),
        pl.BlockSpec((4, HEAD_DIM), lambda b, j, pt: (0, 0)),
        pl.BlockSpec((1, V_DIM), lambda b, j, pt: (0, 0)),
    ] + [k_spec(p) for p in range(pps)] + [v_spec(p, half) for p in range(pps) for half in range(2)]
    rows = N_KV_HEADS * QROWS
    vmem = (2 * 2 * pps * _nbytes((PAGE_SIZE, K_W), F32) + 2 * pps * _nbytes((PAGE_SIZE, K_W), BF16)
            + 6 * _nbytes((rows, pps * 2 * PAGE_SIZE), F32) + 4 * _nbytes((rows, V_DIM), F32)
            + 4 * _nbytes((DEC_SEQ, Q_W), F32))
    return pl.pallas_call(
        _attn_sample_kernel,
        grid_spec=pltpu.PrefetchScalarGridSpec(
            num_scalar_prefetch=1,
            grid=(DEC_BATCH, N_PAGES // pps),
            in_specs=in_specs,
            out_specs=pl.BlockSpec((DEC_SEQ, N_HEADS * V_DIM), lambda b, j, pt: (b, 0)),
            scratch_shapes=[pltpu.VMEM((rows, 1), F32), pltpu.VMEM((rows, 1), F32),
                            pltpu.VMEM((rows, V_DIM), F32)],
        ),
        out_shape=jax.ShapeDtypeStruct((ROWS_S, N_HEADS * V_DIM), F32),
        compiler_params=_params(("arbitrary", "arbitrary"), vmem),
        name="attn_sample",
    )(page_table, qbd, k_s, v_s, lamv, g_subln, *([ck] * pps), *([cv] * (2 * pps)))


def _sgu_kernel(u_ref, v_ref, w_ref, b_ref, g_ref, yb_ref, vn_ref, *, t, last_only):
    v = v_ref[...]
    ms = jnp.mean(v * v, axis=-1, keepdims=True)
    vn = v * lax.rsqrt(ms + EPS) * g_ref[...]
    if last_only:
        @pl.when(pl.program_id(1) == pl.num_programs(1) - 1)
        def _():
            vn_ref[...] = vn
    else:
        vn_ref[...] = vn
    r = v.shape[0]
    row = lax.broadcasted_iota(jnp.int32, (r, r), 0)
    col = lax.broadcasted_iota(jnp.int32, (r, r), 1)
    keep = (col <= row) & (col >= row - (row & (t - 1)))
    for g in range(N_SGU_GROUPS):
        cs = slice(g * SGU_GROUP_W, (g + 1) * SGU_GROUP_W)
        w = jnp.where(keep, w_ref[g], 0.0).astype(BF16)
        mixed = jnp.dot(w, vn[:, cs].astype(BF16), preferred_element_type=F32) + b_ref[g]
        yb_ref[:, cs] = (u_ref[:, cs] * mixed).astype(yb_ref.dtype)


def _sgu_prompt(uv, w_spatial, b_spatial, g_sgu):
    nch = SEQ // CHUNK
    vmem = 2 * (3 * _nbytes((CHUNK, SGU_W), F32) + _nbytes((CHUNK, SGU_W), BF16)) + 6 * _nbytes((CHUNK, SGU_W), F32)
    return pl.pallas_call(
        functools.partial(_sgu_kernel, t=CHUNK, last_only=True),
        grid=(BATCH, nch),
        in_specs=[
            pl.BlockSpec((CHUNK, SGU_W), lambda b, c: (b * nch + c, 0)),
            pl.BlockSpec((CHUNK, SGU_W), lambda b, c: (b * nch + c, 1)),
            pl.BlockSpec((N_SGU_GROUPS, CHUNK, CHUNK), lambda b, c: (0, 0, 0)),
            pl.BlockSpec((N_SGU_GROUPS, CHUNK, 1), lambda b, c: (0, 0, 0)),
            pl.BlockSpec((1, SGU_W), lambda b, c: (0, 0)),
        ],
        out_specs=[pl.BlockSpec((CHUNK, SGU_W), lambda b, c: (b * nch + c, 0)),
                   pl.BlockSpec((None, CHUNK, SGU_W), lambda b, c: (b, 0, 0))],
        out_shape=[jax.ShapeDtypeStruct((ROWS_P, SGU_W), BF16),
                   jax.ShapeDtypeStruct((BATCH, CHUNK, SGU_W), F32)],
        compiler_params=_params(("arbitrary", "arbitrary"), vmem),
        name="sgu_prompt",
    )(uv, uv, w_spatial, b_spatial[:, :CHUNK, None], g_sgu.reshape(1, SGU_W))


def _sgu_sample(uv, w_spatial, b_spatial, g_sgu):
    reps = ROWS_S // DEC_SEQ
    w = jnp.tile(w_spatial[:, :DEC_SEQ, :DEC_SEQ], (1, reps, reps))
    b = jnp.tile(b_spatial[:, :DEC_SEQ], (1, reps))[:, :, None]
    vmem = 2 * (4 * _nbytes((ROWS_S, SGU_W), F32)) + 6 * _nbytes((ROWS_S, SGU_W), F32)
    return pl.pallas_call(
        functools.partial(_sgu_kernel, t=DEC_SEQ, last_only=False),
        grid=(1, 1),
        in_specs=[
            pl.BlockSpec((ROWS_S, SGU_W), lambda b, c: (0, 0)),
            pl.BlockSpec((ROWS_S, SGU_W), lambda b, c: (0, 1)),
            pl.BlockSpec((N_SGU_GROUPS, ROWS_S, ROWS_S), lambda b, c: (0, 0, 0)),
            pl.BlockSpec((N_SGU_GROUPS, ROWS_S, 1), lambda b, c: (0, 0, 0)),
            pl.BlockSpec((1, SGU_W), lambda b, c: (0, 0)),
        ],
        out_specs=[pl.BlockSpec((ROWS_S, SGU_W), lambda b, c: (0, 0)),
                   pl.BlockSpec((ROWS_S, SGU_W), lambda b, c: (0, 0))],
        out_shape=[jax.ShapeDtypeStruct((ROWS_S, SGU_W), BF16),
                   jax.ShapeDtypeStruct((ROWS_S, SGU_W), F32)],
        compiler_params=_params(("arbitrary", "arbitrary"), vmem),
        name="sgu_sample",
    )(uv, uv, w, b, g_sgu.reshape(1, SGU_W))


TAIL = 8


def _ffn_kernel(xp_ref, xs_ref, wg_ref, wu_ref, cw_ref, cb_ref, p1_ref, p2_ref,
                midp_ref, mids_ref, tail_ref, as_ref, carry):
    m, n = pl.program_id(0), pl.program_id(1)
    tiles_per_seq = SEQ // BM

    def act(a, sh1, sh2, up):
        cw = cw_ref[...]
        c = cb_ref[...] + ((cw[0:1] * sh2 + cw[1:2] * sh1) + cw[2:3] * a)
        return jax.nn.silu(c) * up

    x = xp_ref[...]
    a = jnp.dot(x, wg_ref[...], preferred_element_type=F32)
    up = jnp.dot(x, wu_ref[...], preferred_element_type=F32)

    @pl.when(m % tiles_per_seq == 0)
    def _():
        carry[n] = jnp.zeros(carry.shape[1:], F32)

    prev = carry[n]
    row = lax.broadcasted_iota(jnp.int32, a.shape, 0)
    sh1 = jnp.where(row == 0, prev[TAIL - 1:TAIL], pltpu.roll(a, 1, 0))
    sh2 = jnp.where(row == 0, prev[TAIL - 2:TAIL - 1],
                    jnp.where(row == 1, prev[TAIL - 1:TAIL], pltpu.roll(a, 2, 0)))
    midp_ref[...] = act(a, sh1, sh2, up).astype(midp_ref.dtype)
    carry[n] = a[BM - TAIL:]

    @pl.when(m % tiles_per_seq == tiles_per_seq - 1)
    def _():
        tail_ref[...] = a[BM - TAIL:]

    @pl.when(m == N_MP - 1)
    def _():
        xs = xs_ref[...]
        a_s = jnp.dot(xs, wg_ref[...], preferred_element_type=F32)
        up_s = jnp.dot(xs, wu_ref[...], preferred_element_type=F32)
        t = lax.broadcasted_iota(jnp.int32, a_s.shape, 0) & (DEC_SEQ - 1)
        s1 = jnp.where(t == 0, p1_ref[...], pltpu.roll(a_s, 1, 0))
        s2 = jnp.where(t < 2, p2_ref[...], pltpu.roll(a_s, 2, 0))
        mids_ref[...] = act(a_s, s1, s2, up_s).astype(mids_ref.dtype)
        as_ref[...] = a_s


def _ffn_gate_up(hp, hs, wg, wu, conv_w, conv_b, state):
    bn = 256
    n_tiles = D_FF // bn
    prev1 = jnp.pad(state[:, 1:2], ((0, 0), (0, DEC_SEQ - 1), (0, 0))).reshape(ROWS_S, D_FF)
    prev2 = jnp.pad(state, ((0, 0), (0, DEC_SEQ - 2), (0, 0))).reshape(ROWS_S, D_FF)
    tiles_per_seq = SEQ // BM
    last = lambda m, n: jnp.where(m == N_MP - 1, n, 0)
    seq_end = lambda m, n: jnp.where(m % tiles_per_seq == tiles_per_seq - 1, n, 0)
    vmem = (2 * _nbytes((BM + ROWS_S, D_MODEL), BF16) + 2 * 2 * _nbytes((D_MODEL, bn), BF16)
            + 2 * 4 * _nbytes((BM + ROWS_S, bn), F32) + 12 * _nbytes((BM, bn), F32)
            + _nbytes((n_tiles, TAIL, bn), F32))
    return pl.pallas_call(
        _ffn_kernel,
        grid=(N_MP, n_tiles),
        in_specs=[
            pl.BlockSpec((BM, D_MODEL), lambda m, n: (m, 0)),
            pl.BlockSpec((ROWS_S, D_MODEL), lambda m, n: (0, 0)),
            pl.BlockSpec((D_MODEL, bn), lambda m, n: (0, n)),
            pl.BlockSpec((D_MODEL, bn), lambda m, n: (0, n)),
            pl.BlockSpec((CONV_W, bn), lambda m, n: (0, n)),
            pl.BlockSpec((1, bn), lambda m, n: (0, n)),
            pl.BlockSpec((ROWS_S, bn), lambda m, n: (0, last(m, n))),
            pl.BlockSpec((ROWS_S, bn), lambda m, n: (0, last(m, n))),
        ],
        out_specs=[
            pl.BlockSpec((BM, bn), lambda m, n: (m, n)),
            pl.BlockSpec((ROWS_S, bn), lambda m, n: (0, last(m, n))),
            pl.BlockSpec((None, TAIL, bn), lambda m, n: (m // tiles_per_seq, 0, seq_end(m, n))),
            pl.BlockSpec((ROWS_S, bn), lambda m, n: (0, last(m, n))),
        ],
        out_shape=[
            jax.ShapeDtypeStruct((ROWS_P, D_FF), BF16),
            jax.ShapeDtypeStruct((ROWS_S, D_FF), BF16),
            jax.ShapeDtypeStruct((BATCH, TAIL, D_FF), F32),
            jax.ShapeDtypeStruct((ROWS_S, D_FF), F32),
        ],
        scratch_shapes=[pltpu.VMEM((n_tiles, TAIL, bn), F32)],
        compiler_params=_params(("arbitrary", "arbitrary"), vmem),
        name="ffn_gate_up",
    )(hp, hs, wg, wu, conv_w, conv_b.reshape(1, D_FF), prev1, prev2)


def _cast_kernel(x_ref, o_ref):
    o_ref[...] = x_ref[...].astype(o_ref.dtype)


def _cast_bf16(w, name):
    k, n = w.shape
    rows = 256
    vmem = 2 * (_nbytes((rows, n), F32) + _nbytes((rows, n), BF16)) + _nbytes((rows, n), F32)
    return pl.pallas_call(
        _cast_kernel,
        grid=(k // rows,),
        in_specs=[pl.BlockSpec((rows, n), lambda i: (i, 0))],
        out_specs=pl.BlockSpec((rows, n), lambda i: (i, 0)),
        out_shape=jax.ShapeDtypeStruct((k, n), BF16),
        compiler_params=_params(("arbitrary",), vmem),
        name=name,
    )(w)


def _down_kernel(x_ref, w_ref, r_ref, o_ref):
    o_ref[...] = r_ref[...] + jnp.dot(x_ref[...], w_ref[...], preferred_element_type=F32)


def _ffn_down(mid, wb, res, bm, name):
    rows = mid.shape[0]
    bn = 512
    vmem = (2 * (_nbytes((bm, D_FF), BF16) + _nbytes((D_FF, bn), BF16)) + 4 * _nbytes((bm, bn), F32)
            + 2 * _nbytes((bm, bn), F32))
    return pl.pallas_call(
        _down_kernel,
        grid=(rows // bm, D_MODEL // bn),
        in_specs=[
            pl.BlockSpec((bm, D_FF), lambda m, n: (m, 0)),
            pl.BlockSpec((D_FF, bn), lambda m, n: (0, n)),
            pl.BlockSpec((bm, bn), lambda m, n: (m, n)),
        ],
        out_specs=pl.BlockSpec((bm, bn), lambda m, n: (m, n)),
        out_shape=jax.ShapeDtypeStruct((rows, D_MODEL), F32),
        compiler_params=_params(("arbitrary", "arbitrary"), vmem),
        name=name,
    )(mid, wb, res)


def kernel(x_prompt, x_sample, cache_k, cache_v, state_conv, page_table, g_attn, w_in, lam_q1, lam_k1, lam_q2, lam_k2, g_subln, g_sgu, w_spatial, b_spatial, w_o_a, w_o_b, w_out, g_ffn, w_gate, conv_w, conv_b, w_up, w_down, g_final):
    xp = x_prompt.reshape(ROWS_P, D_MODEL)
    xs = x_sample.reshape(ROWS_S, D_MODEL)
    w_in0 = w_in[0]
    lamv = jnp.stack([lam_q1[0], lam_k1[0], lam_q2[0], lam_k2[0]])
    gsub = g_subln[0].reshape(1, V_DIM)

    hp = _rmsnorm_rows(xp, g_attn[0], BF16, "rms_attn_p")
    hs = _rmsnorm_rows(xs, g_attn[0], BF16, "rms_attn_s")

    bn = 512
    one = lambda f: (lambda accs, ex: [f(accs[0])])
    (q_p,), (q_s,) = _fused_matmul(hp, hs, [w_in0], [0], Q_W, bn, one(lambda a: a * SCALE), [BF16], name="proj_q")
    (k_p,), (k_s,) = _fused_matmul(hp, hs, [w_in0], [Q_W // bn], K_W, bn, one(lambda a: a), [F32], name="proj_k")
    (v_p,), (v_s,) = _fused_matmul(hp, hs, [w_in0], [(Q_W + K_W) // bn], V_W, bn, one(lambda a: a), [F32],
                                   name="proj_v")
    (uv_p,), (uv_s,) = _fused_matmul(hp, hs, [w_in0], [(Q_W + K_W + V_W) // bn], 2 * SGU_W, bn,
                                     one(_gelu_exact), [F32], name="proj_uv")
    (gt_p,), (gt_s,) = _fused_matmul(hp, hs, [w_in0], [(Q_W + K_W + V_W + 2 * SGU_W) // bn], 2 * D_MODEL, bn,
                                     one(jax.nn.sigmoid), [F32], name="proj_gates")

    ya_p = _attn_prompt(q_p, k_p, v_p, lamv, gsub)
    ya_s = _attn_sample(q_s, k_s, v_s, cache_k, cache_v, page_table, lamv, gsub)

    yb_p, vn_p = _sgu_prompt(uv_p, w_spatial[0], b_spatial[0], g_sgu[0])
    yb_s, vn_s = _sgu_sample(uv_s, w_spatial[0], b_spatial[0], g_sgu[0])

    (ta_p,), (ta_s,) = _fused_matmul(ya_p, ya_s, [w_o_a[0]], [0], D_MODEL, bn,
                                     lambda accs, ex: [ex[0] * accs[0]], [F32],
                                     extras=[(gt_p, gt_s)], ex_off=[0], name="merge_a")
    wobb = _cast_bf16(w_o_b[0], "cast_w_o_b")
    woutb = _cast_bf16(w_out[0], "cast_w_out")
    (z_p,), (z_s,) = _stream_matmul(yb_p, yb_s, wobb, D_MODEL, bn,
                                    lambda accs, ex: [ex[1] + ex[0] * accs[0]], [BF16],
                                    extras=[(gt_p, gt_s), (ta_p, ta_s)], ex_off=[D_MODEL // bn, 0], name="merge_b")
    (x1_p,), (x1_s,) = _stream_matmul(z_p, z_s, woutb, D_MODEL, bn,
                                      lambda accs, ex: [ex[0] + accs[0]], [F32],
                                      extras=[(xp, xs)], ex_off=[0], name="out_proj")

    h2_p = _rmsnorm_rows(x1_p, g_ffn[0], BF16, "rms_ffn_p")
    h2_s = _rmsnorm_rows(x1_s, g_ffn[0], BF16, "rms_ffn_s")
    wgb = _cast_bf16(w_gate[0], "cast_w_gate")
    wub = _cast_bf16(w_up[0], "cast_w_up")
    wdb = _cast_bf16(w_down[0], "cast_w_down")
    mid_p, mid_s, tail_p, a_s = _ffn_gate_up(h2_p, h2_s, wgb, wub, conv_w[0], conv_b[0], state_conv[0])
    x2_p = _ffn_down(mid_p, wdb, x1_p, 512, "ffn_down_p")
    x2_s = _ffn_down(mid_s, wdb, x1_s, ROWS_S, "ffn_down_s")

    y_p = _rmsnorm_rows(x2_p, g_final, F32, "rms_final_p")
    y_s = _rmsnorm_rows(x2_s, g_final, F32, "rms_final_s")

    return (
        y_p.reshape(BATCH, SEQ, D_MODEL),
        y_s.reshape(DEC_BATCH, DEC_SEQ, D_MODEL),
        k_p.reshape(1, BATCH, SEQ, N_KV_HEADS, 2, HEAD_DIM),
        v_p.reshape(1, BATCH, SEQ, N_KV_HEADS, V_DIM),
        k_s.reshape(1, DEC_BATCH, DEC_SEQ, N_KV_HEADS, 2, HEAD_DIM),
        v_s.reshape(1, DEC_BATCH, DEC_SEQ, N_KV_HEADS, V_DIM),
        vn_p.reshape(1, BATCH, CHUNK, SGU_W),
        vn_s.reshape(1, DEC_BATCH, DEC_SEQ, SGU_W),
        tail_p[:, TAIL - (CONV_W - 1):].reshape(1, BATCH, CONV_W - 1, D_FF),
        a_s.reshape(DEC_BATCH, DEC_SEQ, D_FF)[:, DEC_SEQ - (CONV_W - 1):].reshape(1, DEC_BATCH, CONV_W - 1, D_FF),
    )
```

```python
import functools
import math

import jax
import jax.numpy as jnp
from jax import lax
from jax.experimental import pallas as pl
from jax.experimental.pallas import tpu as pltpu

D_MODEL = 4096
BATCH = 4
SEQ = 2048
DEC_BATCH = 32
DEC_SEQ = 8
PAST_LEN = 8192
PAGE_SIZE = 128
N_HEADS = 16
N_KV_HEADS = 8
GROUP = N_HEADS // N_KV_HEADS
HEAD_DIM = D_MODEL // (2 * N_HEADS)
V_DIM = 2 * HEAD_DIM
Q_W = N_HEADS * 2 * HEAD_DIM
K_W = N_KV_HEADS * 2 * HEAD_DIM
V_W = N_KV_HEADS * V_DIM
SGU_W = D_MODEL
N_SGU_GROUPS = 8
SGU_GROUP_W = SGU_W // N_SGU_GROUPS
CHUNK = 128
D_FF = 11008
CONV_W = 3
SCALE = 1.0 / math.sqrt(HEAD_DIM)
NEG = -1e30
EPS = 1e-6
LAMBDA_INIT = 0.8 - 0.6 * math.exp(-0.3 * 0)

ROWS_P = BATCH * SEQ
ROWS_S = DEC_BATCH * DEC_SEQ
N_PAGES = PAST_LEN // PAGE_SIZE

V7X_VMEM_BYTES = 64 * 1024 * 1024
VMEM_CAP_BYTES = 58 * 1024 * 1024

BM = 1024
N_MP = ROWS_P // BM

F32 = jnp.float32
BF16 = jnp.bfloat16
_NT = (((1,), (1,)), ((), ()))


def _nbytes(shape, dtype):
    return math.prod(shape) * jnp.dtype(dtype).itemsize


def _params(sem, vmem_bytes):
    return pltpu.CompilerParams(dimension_semantics=sem,
                                vmem_limit_bytes=int(min(vmem_bytes, VMEM_CAP_BYTES)))


def _rms_kernel(x_ref, g_ref, o_ref):
    x = x_ref[...].astype(F32)
    ms = jnp.mean(x * x, axis=-1, keepdims=True)
    o_ref[...] = (x * lax.rsqrt(ms + EPS) * g_ref[...]).astype(o_ref.dtype)


def _rmsnorm_rows(x, g, out_dtype, name):
    n, d = x.shape
    rows = min(n, 256)
    vmem = 2 * (_nbytes((rows, d), x.dtype) + _nbytes((rows, d), out_dtype)) + 4 * _nbytes((rows, d), F32)
    return pl.pallas_call(
        _rms_kernel,
        grid=(n // rows,),
        in_specs=[pl.BlockSpec((rows, d), lambda i: (i, 0)), pl.BlockSpec((1, d), lambda i: (0, 0))],
        out_specs=pl.BlockSpec((rows, d), lambda i: (i, 0)),
        out_shape=jax.ShapeDtypeStruct((n, d), out_dtype),
        compiler_params=_params(("arbitrary",), vmem),
        name=name,
    )(x, g.reshape(1, d))


def _mms_kernel(*refs, n_ex, n_out, n_cast, epilogue):
    it = iter(refs)
    xp_ref, xs_ref, w_ref = next(it), next(it), next(it)
    exp_refs = [next(it) for _ in range(n_ex)]
    exs_refs = [next(it) for _ in range(n_ex)]
    cin_refs = [next(it) for _ in range(n_cast)]
    outp_refs = [next(it) for _ in range(n_out)]
    outs_refs = [next(it) for _ in range(n_out)]
    cout_refs = [next(it) for _ in range(n_cast)]

    for ci, co in zip(cin_refs, cout_refs):
        co[...] = ci[...].astype(BF16)

    def run(x_ref, ex_refs, out_refs):
        acc = jnp.dot(x_ref[...].astype(BF16), w_ref[...], preferred_element_type=F32)
        vals = epilogue([acc], [r[...] for r in ex_refs])
        for o_ref, v in zip(out_refs, vals):
            o_ref[...] = v.astype(o_ref.dtype)

    run(xp_ref, exp_refs, outp_refs)

    @pl.when(pl.program_id(0) == N_MP - 1)
    def _():
        run(xs_ref, exs_refs, outs_refs)


def _stream_matmul(xp, xs, wb, w_off, n_cols, bn, epilogue, out_dtypes, extras=(), ex_off=(), cast=(), name="mms"):
    k_dim = xp.shape[1]
    n_tiles = n_cols // bn
    n_ex, n_out, n_cast = len(extras), len(out_dtypes), len(cast)
    last = lambda m, n: jnp.where(m == N_MP - 1, n, 0)
    in_specs = [pl.BlockSpec((BM, k_dim), lambda m, n: (m, 0)),
                pl.BlockSpec((ROWS_S, k_dim), lambda m, n: (0, 0)),
                pl.BlockSpec((k_dim, bn), lambda m, n: (0, n + w_off))]
    for off in ex_off:
        in_specs.append(pl.BlockSpec((BM, bn), lambda m, n, off=off: (m, n + off)))
    for off in ex_off:
        in_specs.append(pl.BlockSpec((ROWS_S, bn), lambda m, n, off=off: (0, last(m, n) + off)))
    out_specs = ([pl.BlockSpec((BM, bn), lambda m, n: (m, n))] * n_out
                 + [pl.BlockSpec((ROWS_S, bn), lambda m, n: (0, last(m, n)))] * n_out)
    out_shape = ([jax.ShapeDtypeStruct((ROWS_P, n_cols), dt) for dt in out_dtypes]
                 + [jax.ShapeDtypeStruct((ROWS_S, n_cols), dt) for dt in out_dtypes])
    vmem = (2 * (_nbytes((BM, k_dim), xp.dtype) + _nbytes((ROWS_S, k_dim), xs.dtype))
            + 2 * _nbytes((k_dim, bn), BF16)
            + 2 * (n_ex + n_out) * _nbytes((BM + ROWS_S, bn), F32)
            + 3 * _nbytes((BM, bn), F32))
    for arr, n_chunks in cast:
        assert n_chunks <= N_MP * n_tiles and arr.shape[0] % n_chunks == 0
        blk = (arr.shape[0] // n_chunks, arr.shape[1])
        chunk = lambda m, n, n_chunks=n_chunks: (jnp.minimum(m * n_tiles + n, n_chunks - 1), 0)
        in_specs.append(pl.BlockSpec(blk, chunk))
        out_specs.append(pl.BlockSpec(blk, chunk))
        out_shape.append(jax.ShapeDtypeStruct(arr.shape, BF16))
        vmem += 2 * (_nbytes(blk, F32) + _nbytes(blk, BF16)) + _nbytes(blk, F32)
    args = [xp, xs, wb] + [e[0] for e in extras] + [e[1] for e in extras] + [c[0] for c in cast]
    res = pl.pallas_call(
        functools.partial(_mms_kernel, n_ex=n_ex, n_out=n_out, n_cast=n_cast, epilogue=epilogue),
        grid=(N_MP, n_tiles),
        in_specs=in_specs,
        out_specs=out_specs,
        out_shape=out_shape,
        compiler_params=_params(("arbitrary", "arbitrary"), vmem),
        name=name,
    )(*args)
    return res[:n_out], res[n_out:2 * n_out], res[2 * n_out:]


def _gelu_exact(x):
    return 0.5 * x * (1.0 + lax.erf(x * math.sqrt(0.5)))


def _lambda_value(lv):
    a = jnp.sum(lv[0:1] * lv[1:2], axis=-1, keepdims=True)
    b = jnp.sum(lv[2:3] * lv[3:4], axis=-1, keepdims=True)
    return jnp.exp(a) - jnp.exp(b) + LAMBDA_INIT


def _head_norm(o, g):
    ms = jnp.mean(o * o, axis=-1, keepdims=True)
    return o * lax.rsqrt(ms + EPS) * g * (1.0 - LAMBDA_INIT)


def _attn_prompt_kernel(q_ref, k_ref, v_ref, lam_ref, g_ref, o_ref, kb, vb, *, bq):
    def cast_rows(i, carry):
        r = pl.multiple_of(i * 256, 256)
        kb[pl.ds(r, 256), :] = k_ref[pl.ds(r, 256), :].astype(BF16)
        vb[pl.ds(r, 256), :] = v_ref[pl.ds(r, 256), :].astype(BF16)
        return carry
    lax.fori_loop(0, SEQ // 256, cast_rows, 0)

    lam = _lambda_value(lam_ref[...])
    row = lax.broadcasted_iota(jnp.int32, (GROUP * bq, bq), 0) & (bq - 1)
    col = lax.broadcasted_iota(jnp.int32, (GROUP * bq, bq), 1)
    causal = col <= row
    for qi in range(SEQ // bq):
        keys = (qi + 1) * bq
        q = q_ref[qi * bq:(qi + 1) * bq, :]
        o_maps = []
        for c in range(2):
            qc = jnp.concatenate([q[:, (g * 2 + c) * HEAD_DIM:(g * 2 + c + 1) * HEAD_DIM] for g in range(GROUP)],
                                 axis=0)
            s = lax.dot_general(qc, kb[0:keys, c * HEAD_DIM:(c + 1) * HEAD_DIM], _NT, preferred_element_type=F32)
            diag = jnp.where(causal, s[:, keys - bq:], NEG)
            s = diag if qi == 0 else jnp.concatenate([s[:, :keys - bq], diag], axis=1)
            p = jnp.exp(s - jnp.max(s, axis=-1, keepdims=True))
            l = jnp.sum(p, axis=-1, keepdims=True)
            o_maps.append(jnp.dot(p.astype(BF16), vb[0:keys, :], preferred_element_type=F32) / l)
        o = o_maps[0] - lam * o_maps[1]
        for g in range(GROUP):
            o_ref[qi * bq:(qi + 1) * bq, g * V_DIM:(g + 1) * V_DIM] = _head_norm(
                o[g * bq:(g + 1) * bq], g_ref[...]).astype(o_ref.dtype)


def _attn_prompt(q, k, v, lamv, g_subln):
    bq = 256
    qw = GROUP * 2 * HEAD_DIM
    vmem = (2 * (2 * _nbytes((SEQ, qw), BF16) + 2 * _nbytes((SEQ, V_DIM), F32))
            + 2 * _nbytes((SEQ, V_DIM), BF16) + 10 * _nbytes((GROUP * bq, SEQ), F32))
    return pl.pallas_call(
        functools.partial(_attn_prompt_kernel, bq=bq),
        grid=(BATCH, N_KV_HEADS),
        in_specs=[
            pl.BlockSpec((SEQ, qw), lambda b, h: (b, h)),
            pl.BlockSpec((SEQ, 2 * HEAD_DIM), lambda b, h: (b, h)),
            pl.BlockSpec((SEQ, V_DIM), lambda b, h: (b, h)),
            pl.BlockSpec((4, HEAD_DIM), lambda b, h: (0, 0)),
            pl.BlockSpec((1, V_DIM), lambda b, h: (0, 0)),
        ],
        out_specs=pl.BlockSpec((SEQ, GROUP * V_DIM), lambda b, h: (b, h)),
        out_shape=jax.ShapeDtypeStruct((ROWS_P, N_HEADS * V_DIM), BF16),
        scratch_shapes=[pltpu.VMEM((SEQ, 2 * HEAD_DIM), BF16), pltpu.VMEM((SEQ, V_DIM), BF16)],
        compiler_params=_params(("arbitrary", "arbitrary"), vmem),
        name="attn_prompt",
    )(q, k, v, lamv, g_subln)


PAGES_PER_STEP = 8
QROWS = 2 * GROUP * DEC_SEQ
PAIRS = N_KV_HEADS // 2
PROWS = 2 * QROWS


def _attn_sample_kernel(pt_ref, qbd_ref, knew_ref, vnew_ref, lam_ref, g_ref, *rest):
    del pt_ref
    kp_refs = rest[:PAGES_PER_STEP]
    vp_refs = rest[PAGES_PER_STEP:3 * PAGES_PER_STEP]
    o_ref, m_s, l_s, acc_s = rest[3 * PAGES_PER_STEP:]
    j = pl.program_id(1)

    @pl.when(j == 0)
    def _():
        m_s[...] = jnp.full(m_s.shape, NEG, F32)
        l_s[...] = jnp.zeros(l_s.shape, F32)
        acc_s[...] = jnp.zeros(acc_s.shape, F32)

    def scores(k_tiles):
        cols = []
        for kt in k_tiles:
            rows = [lax.dot_general(qbd_ref[0, hp], kt(hp), _NT, preferred_element_type=F32)
                    for hp in range(PAIRS)]
            cols.append(jnp.concatenate(rows, axis=0))
        return cols[0] if len(cols) == 1 else jnp.concatenate(cols, axis=1)

    def update(s, v_tiles, width):
        m_prev = m_s[...]
        m_new = jnp.maximum(m_prev, jnp.max(s, axis=-1, keepdims=True))
        alpha = jnp.exp(m_prev - m_new)
        p = jnp.exp(s - m_new)
        l_s[...] = alpha * l_s[...] + jnp.sum(p, axis=-1, keepdims=True)
        pb = p.astype(BF16)
        pv = []
        for hp in range(PAIRS):
            acc = None
            for i, vt in enumerate(v_tiles):
                d = jnp.dot(pb[hp * PROWS:(hp + 1) * PROWS, i * width:(i + 1) * width], vt(hp),
                            preferred_element_type=F32)
                acc = d if acc is None else acc + d
            pv.append(acc)
        acc_s[...] = alpha * acc_s[...] + jnp.concatenate(pv, axis=0)
        m_s[...] = m_new

    def k_page(ref):
        def get(hp):
            maps = [ref[pl.ds(2 * hp + c, 2 * PAGE_SIZE, stride=N_KV_HEADS), :] for c in range(2)]
            return jnp.concatenate(maps, axis=1).astype(BF16)
        return get

    def v_page(ref_lo, ref_hi):
        def get(hp):
            halves = [r[pl.ds(hp, 2 * PAGE_SIZE, stride=PAIRS), :] for r in (ref_lo, ref_hi)]
            return jnp.concatenate(halves, axis=1).astype(BF16)
        return get

    def which_head(shape):
        return (lax.broadcasted_iota(jnp.int32, shape, 0) >> int(math.log2(QROWS))) & 1

    s = scores([k_page(r) for r in kp_refs])
    col = lax.broadcasted_iota(jnp.int32, s.shape, 1)
    s = jnp.where((col & 1) == which_head(s.shape), s, NEG)
    update(s, [v_page(vp_refs[2 * p], vp_refs[2 * p + 1]) for p in range(PAGES_PER_STEP)], 2 * PAGE_SIZE)

    @pl.when(j == pl.num_programs(1) - 1)
    def _():
        pad = jnp.zeros((PAGE_SIZE - 2 * DEC_SEQ, 2 * HEAD_DIM), F32)

        def new_tile(ref):
            def get(hp):
                parts = [ref[:, h * V_DIM:(h + 1) * V_DIM] for h in (hp, hp + PAIRS)]
                return jnp.concatenate(parts + [pad], axis=0).astype(BF16)
            return get

        s = scores([new_tile(knew_ref)])
        t = lax.broadcasted_iota(jnp.int32, s.shape, 0) & (DEC_SEQ - 1)
        col = lax.broadcasted_iota(jnp.int32, s.shape, 1)
        ok = ((col >> int(math.log2(DEC_SEQ))) == which_head(s.shape)) & ((col & (DEC_SEQ - 1)) <= t)
        update(jnp.where(ok, s, NEG), [new_tile(vnew_ref)], PAGE_SIZE)

        lam = _lambda_value(lam_ref[...])
        o = acc_s[...] / l_s[...]
        half = GROUP * DEC_SEQ
        for hp in range(PAIRS):
            for hi in range(2):
                r0 = hp * PROWS + hi * QROWS
                d = o[r0:r0 + half] - lam * o[r0 + half:r0 + QROWS]
                y = _head_norm(d, g_ref[...])
                for g in range(GROUP):
                    c0 = ((hi * PAIRS + hp) * GROUP + g) * V_DIM
                    o_ref[:, c0:c0 + V_DIM] = y[g * DEC_SEQ:(g + 1) * DEC_SEQ]


def _attn_sample(q_s, k_s, v_s, cache_k, cache_v, page_table, lamv, g_subln):
    q6 = q_s.reshape(DEC_BATCH, DEC_SEQ, N_KV_HEADS, GROUP, 2, HEAD_DIM).transpose(0, 2, 4, 3, 1, 5)
    zero = jnp.zeros_like(q6[:, :, 0])
    top = jnp.concatenate([q6[:, :, 0], zero], axis=-1).reshape(DEC_BATCH, N_KV_HEADS, QROWS // 2, 2 * HEAD_DIM)
    bot = jnp.concatenate([zero, q6[:, :, 1]], axis=-1).reshape(DEC_BATCH, N_KV_HEADS, QROWS // 2, 2 * HEAD_DIM)
    qbd = jnp.concatenate([top, bot], axis=2)
    qbd = qbd.reshape(DEC_BATCH, 2, PAIRS, QROWS, 2 * HEAD_DIM).transpose(0, 2, 1, 3, 4).reshape(
        DEC_BATCH, PAIRS, PROWS, 2 * HEAD_DIM)
    n_pool = cache_k.shape[1]
    ck = cache_k.reshape(n_pool, PAGE_SIZE * N_KV_HEADS * 2, HEAD_DIM)
    cv = cache_v.reshape(n_pool, PAGE_SIZE * N_KV_HEADS, V_DIM)
    pps = PAGES_PER_STEP

    def k_spec(p):
        return pl.BlockSpec((None, ck.shape[1], HEAD_DIM), lambda b, j, pt: (pt[b, j * pps + p], 0, 0))

    def v_spec(p, half):
        return pl.BlockSpec((None, cv.shape[1], HEAD_DIM), lambda b, j, pt: (pt[b, j * pps + p], 0, half))

    in_specs = [
        pl.BlockSpec((1, PAIRS, PROWS, 2 * HEAD_DIM), lambda b, j, pt: (b, 0, 0, 0)),
        pl.BlockSpec((DEC_SEQ, K_W), lambda b, j, pt: (b, 0)),
        pl.BlockSpec((DEC_SEQ, V_W), lambda b, j, pt: (b, 0)),
        pl.BlockSpec((4, HEAD_DIM), lambda b, j, pt: (0, 0)),
        pl.BlockSpec((1, V_DIM), lambda b, j, pt: (0, 0)),
    ] + [k_spec(p) for p in range(pps)] + [v_spec(p, half) for p in range(pps) for half in range(2)]
    rows = N_KV_HEADS * QROWS
    vmem = (2 * 2 * pps * _nbytes((PAGE_SIZE, K_W), F32) + 2 * pps * _nbytes((PAGE_SIZE, K_W), BF16)
            + 6 * _nbytes((rows, pps * 2 * PAGE_SIZE), F32) + 4 * _nbytes((rows, V_DIM), F32)
            + 4 * _nbytes((DEC_SEQ, Q_W), F32))
    return pl.pallas_call(
        _attn_sample_kernel,
        grid_spec=pltpu.PrefetchScalarGridSpec(
            num_scalar_prefetch=1,
            grid=(DEC_BATCH, N_PAGES // pps),
            in_specs=in_specs,
            out_specs=pl.BlockSpec((DEC_SEQ, N_HEADS * V_DIM), lambda b, j, pt: (b, 0)),
            scratch_shapes=[pltpu.VMEM((rows, 1), F32), pltpu.VMEM((rows, 1), F32),
                            pltpu.VMEM((rows, V_DIM), F32)],
        ),
        out_shape=jax.ShapeDtypeStruct((ROWS_S, N_HEADS * V_DIM), F32),
        compiler_params=_params(("arbitrary", "arbitrary"), vmem),
        name="attn_sample",
    )(page_table, qbd, k_s, v_s, lamv, g_subln, *([ck] * pps), *([cv] * (2 * pps)))


def _sgu_kernel(u_ref, v_ref, w_ref, b_ref, g_ref, yb_ref, vn_ref, *, t, last_only):
    v = v_ref[...]
    ms = jnp.mean(v * v, axis=-1, keepdims=True)
    vn = v * lax.rsqrt(ms + EPS) * g_ref[...]
    if last_only:
        @pl.when(pl.program_id(1) == pl.num_programs(1) - 1)
        def _():
            vn_ref[...] = vn
    else:
        vn_ref[...] = vn
    r = v.shape[0]
    row = lax.broadcasted_iota(jnp.int32, (r, r), 0)
    col = lax.broadcasted_iota(jnp.int32, (r, r), 1)
    keep = (col <= row) & (col >= row - (row & (t - 1)))
    for g in range(N_SGU_GROUPS):
        cs = slice(g * SGU_GROUP_W, (g + 1) * SGU_GROUP_W)
        w = jnp.where(keep, w_ref[g], 0.0).astype(BF16)
        mixed = jnp.dot(w, vn[:, cs].astype(BF16), preferred_element_type=F32) + b_ref[g]
        yb_ref[:, cs] = (u_ref[:, cs] * mixed).astype(yb_ref.dtype)


def _sgu_prompt(uv, w_spatial, b_spatial, g_sgu):
    nch = SEQ // CHUNK
    vmem = 2 * (3 * _nbytes((CHUNK, SGU_W), F32) + _nbytes((CHUNK, SGU_W), BF16)) + 6 * _nbytes((CHUNK, SGU_W), F32)
    return pl.pallas_call(
        functools.partial(_sgu_kernel, t=CHUNK, last_only=True),
        grid=(BATCH, nch),
        in_specs=[
            pl.BlockSpec((CHUNK, SGU_W), lambda b, c: (b * nch + c, 0)),
            pl.BlockSpec((CHUNK, SGU_W), lambda b, c: (b * nch + c, 1)),
            pl.BlockSpec((N_SGU_GROUPS, CHUNK, CHUNK), lambda b, c: (0, 0, 0)),
            pl.BlockSpec((N_SGU_GROUPS, CHUNK, 1), lambda b, c: (0, 0, 0)),
            pl.BlockSpec((1, SGU_W), lambda b, c: (0, 0)),
        ],
        out_specs=[pl.BlockSpec((CHUNK, SGU_W), lambda b, c: (b * nch + c, 0)),
                   pl.BlockSpec((None, CHUNK, SGU_W), lambda b, c: (b, 0, 0))],
        out_shape=[jax.ShapeDtypeStruct((ROWS_P, SGU_W), BF16),
                   jax.ShapeDtypeStruct((BATCH, CHUNK, SGU_W), F32)],
        compiler_params=_params(("arbitrary", "arbitrary"), vmem),
        name="sgu_prompt",
    )(uv, uv, w_spatial, b_spatial[:, :CHUNK, None], g_sgu.reshape(1, SGU_W))


def _sgu_sample(uv, w_spatial, b_spatial, g_sgu):
    reps = ROWS_S // DEC_SEQ
    w = jnp.tile(w_spatial[:, :DEC_SEQ, :DEC_SEQ], (1, reps, reps))
    b = jnp.tile(b_spatial[:, :DEC_SEQ], (1, reps))[:, :, None]
    vmem = 2 * (4 * _nbytes((ROWS_S, SGU_W), F32)) + 6 * _nbytes((ROWS_S, SGU_W), F32)
    return pl.pallas_call(
        functools.partial(_sgu_kernel, t=DEC_SEQ, last_only=False),
        grid=(1, 1),
        in_specs=[
            pl.BlockSpec((ROWS_S, SGU_W), lambda b, c: (0, 0)),
            pl.BlockSpec((ROWS_S, SGU_W), lambda b, c: (0, 1)),
            pl.BlockSpec((N_SGU_GROUPS, ROWS_S, ROWS_S), lambda b, c: (0, 0, 0)),
            pl.BlockSpec((N_SGU_GROUPS, ROWS_S, 1), lambda b, c: (0, 0, 0)),
            pl.BlockSpec((1, SGU_W), lambda b, c: (0, 0)),
        ],
        out_specs=[pl.BlockSpec((ROWS_S, SGU_W), lambda b, c: (0, 0)),
                   pl.BlockSpec((ROWS_S, SGU_W), lambda b, c: (0, 0))],
        out_shape=[jax.ShapeDtypeStruct((ROWS_S, SGU_W), BF16),
                   jax.ShapeDtypeStruct((ROWS_S, SGU_W), F32)],
        compiler_params=_params(("arbitrary", "arbitrary"), vmem),
        name="sgu_sample",
    )(uv, uv, w, b, g_sgu.reshape(1, SGU_W))


TAIL = 8


def _ffn_kernel(xp_ref, xs_ref, wg_ref, wu_ref, cw_ref, cb_ref, p1_ref, p2_ref,
                midp_ref, mids_ref, tail_ref, as_ref, carry):
    m, n = pl.program_id(0), pl.program_id(1)
    tiles_per_seq = SEQ // BM

    def act(a, sh1, sh2, up):
        cw = cw_ref[...]
        c = cb_ref[...] + ((cw[0:1] * sh2 + cw[1:2] * sh1) + cw[2:3] * a)
        return jax.nn.silu(c) * up

    x = xp_ref[...]
    a = jnp.dot(x, wg_ref[...], preferred_element_type=F32)
    up = jnp.dot(x, wu_ref[...], preferred_element_type=F32)

    @pl.when(m % tiles_per_seq == 0)
    def _():
        carry[n] = jnp.zeros(carry.shape[1:], F32)

    prev = carry[n]
    row = lax.broadcasted_iota(jnp.int32, a.shape, 0)
    sh1 = jnp.where(row == 0, prev[TAIL - 1:TAIL], pltpu.roll(a, 1, 0))
    sh2 = jnp.where(row == 0, prev[TAIL - 2:TAIL - 1],
                    jnp.where(row == 1, prev[TAIL - 1:TAIL], pltpu.roll(a, 2, 0)))
    midp_ref[...] = act(a, sh1, sh2, up).astype(midp_ref.dtype)
    carry[n] = a[BM - TAIL:]

    @pl.when(m % tiles_per_seq == tiles_per_seq - 1)
    def _():
        tail_ref[...] = a[BM - TAIL:]

    @pl.when(m == N_MP - 1)
    def _():
        xs = xs_ref[...]
        a_s = jnp.dot(xs, wg_ref[...], preferred_element_type=F32)
        up_s = jnp.dot(xs, wu_ref[...], preferred_element_type=F32)
        t = lax.broadcasted_iota(jnp.int32, a_s.shape, 0) & (DEC_SEQ - 1)
        s1 = jnp.where(t == 0, p1_ref[...], pltpu.roll(a_s, 1, 0))
        s2 = jnp.where(t < 2, p2_ref[...], pltpu.roll(a_s, 2, 0))
        mids_ref[...] = act(a_s, s1, s2, up_s).astype(mids_ref.dtype)
        as_ref[...] = a_s


def _ffn_gate_up(hp, hs, wg, wu, conv_w, conv_b, state):
    bn = 256
    n_tiles = D_FF // bn
    prev1 = jnp.pad(state[:, 1:2], ((0, 0), (0, DEC_SEQ - 1), (0, 0))).reshape(ROWS_S, D_FF)
    prev2 = jnp.pad(state, ((0, 0), (0, DEC_SEQ - 2), (0, 0))).reshape(ROWS_S, D_FF)
    tiles_per_seq = SEQ // BM
    last = lambda m, n: jnp.where(m == N_MP - 1, n, 0)
    seq_end = lambda m, n: jnp.where(m % tiles_per_seq == tiles_per_seq - 1, n, 0)
    vmem = (2 * _nbytes((BM + ROWS_S, D_MODEL), BF16) + 2 * 2 * _nbytes((D_MODEL, bn), BF16)
            + 2 * 4 * _nbytes((BM + ROWS_S, bn), F32) + 12 * _nbytes((BM, bn), F32)
            + _nbytes((n_tiles, TAIL, bn), F32))
    return pl.pallas_call(
        _ffn_kernel,
        grid=(N_MP, n_tiles),
        in_specs=[
            pl.BlockSpec((BM, D_MODEL), lambda m, n: (m, 0)),
            pl.BlockSpec((ROWS_S, D_MODEL), lambda m, n: (0, 0)),
            pl.BlockSpec((D_MODEL, bn), lambda m, n: (0, n)),
            pl.BlockSpec((D_MODEL, bn), lambda m, n: (0, n)),
            pl.BlockSpec((CONV_W, bn), lambda m, n: (0, n)),
            pl.BlockSpec((1, bn), lambda m, n: (0, n)),
            pl.BlockSpec((ROWS_S, bn), lambda m, n: (0, last(m, n))),
            pl.BlockSpec((ROWS_S, bn), lambda m, n: (0, last(m, n))),
        ],
        out_specs=[
            pl.BlockSpec((BM, bn), lambda m, n: (m, n)),
            pl.BlockSpec((ROWS_S, bn), lambda m, n: (0, last(m, n))),
            pl.BlockSpec((None, TAIL, bn), lambda m, n: (m // tiles_per_seq, 0, seq_end(m, n))),
            pl.BlockSpec((ROWS_S, bn), lambda m, n: (0, last(m, n))),
        ],
        out_shape=[
            jax.ShapeDtypeStruct((ROWS_P, D_FF), BF16),
            jax.ShapeDtypeStruct((ROWS_S, D_FF), BF16),
            jax.ShapeDtypeStruct((BATCH, TAIL, D_FF), F32),
            jax.ShapeDtypeStruct((ROWS_S, D_FF), F32),
        ],
        scratch_shapes=[pltpu.VMEM((n_tiles, TAIL, bn), F32)],
        compiler_params=_params(("arbitrary", "arbitrary"), vmem),
        name="ffn_gate_up",
    )(hp, hs, wg, wu, conv_w, conv_b.reshape(1, D_FF), prev1, prev2)


def _cast_kernel(x_ref, o_ref):
    o_ref[...] = x_ref[...].astype(o_ref.dtype)


def _cast_bf16(w, name):
    k, n = w.shape
    rows = 64
    assert k % rows == 0
    vmem = 2 * (_nbytes((rows, n), F32) + _nbytes((rows, n), BF16)) + _nbytes((rows, n), F32)
    return pl.pallas_call(
        _cast_kernel,
        grid=(k // rows,),
        in_specs=[pl.BlockSpec((rows, n), lambda i: (i, 0))],
        out_specs=pl.BlockSpec((rows, n), lambda i: (i, 0)),
        out_shape=jax.ShapeDtypeStruct((k, n), BF16),
        compiler_params=_params(("arbitrary",), vmem),
        name=name,
    )(w)


def _down_kernel(x_ref, w_ref, r_ref, o_ref):
    o_ref[...] = r_ref[...] + jnp.dot(x_ref[...], w_ref[...], preferred_element_type=F32)


def _ffn_down(mid, wb, res, bm, name):
    rows = mid.shape[0]
    bn = 512
    vmem = (2 * (_nbytes((bm, D_FF), BF16) + _nbytes((D_FF, bn), BF16)) + 4 * _nbytes((bm, bn), F32)
            + 2 * _nbytes((bm, bn), F32))
    return pl.pallas_call(
        _down_kernel,
        grid=(rows // bm, D_MODEL // bn),
        in_specs=[
            pl.BlockSpec((bm, D_FF), lambda m, n: (m, 0)),
            pl.BlockSpec((D_FF, bn), lambda m, n: (0, n)),
            pl.BlockSpec((bm, bn), lambda m, n: (m, n)),
        ],
        out_specs=pl.BlockSpec((bm, bn), lambda m, n: (m, n)),
        out_shape=jax.ShapeDtypeStruct((rows, D_MODEL), F32),
        compiler_params=_params(("arbitrary", "arbitrary"), vmem),
        name=name,
    )(mid, wb, res)


def kernel(x_prompt, x_sample, cache_k, cache_v, state_conv, page_table, g_attn, w_in, lam_q1, lam_k1, lam_q2, lam_k2, g_subln, g_sgu, w_spatial, b_spatial, w_o_a, w_o_b, w_out, g_ffn, w_gate, conv_w, conv_b, w_up, w_down, g_final):
    xp = x_prompt.reshape(ROWS_P, D_MODEL)
    xs = x_sample.reshape(ROWS_S, D_MODEL)
    lamv = jnp.stack([lam_q1[0], lam_k1[0], lam_q2[0], lam_k2[0]])
    gsub = g_subln[0].reshape(1, V_DIM)

    hp = _rmsnorm_rows(xp, g_attn[0], BF16, "rms_attn_p")
    hs = _rmsnorm_rows(xs, g_attn[0], BF16, "rms_attn_s")

    bn = 512
    one = lambda f: (lambda accs, ex: [f(accs[0])])
    col = lambda c: c // bn
    w_inb = _cast_bf16(w_in[0], "cast_w_in")
    (q_p,), (q_s,), (woab,) = _stream_matmul(hp, hs, w_inb, 0, Q_W, bn, one(lambda a: a * SCALE), [BF16],
                                             cast=[(w_o_a[0], 64)], name="proj_q")
    (k_p,), (k_s,), (wobb,) = _stream_matmul(hp, hs, w_inb, col(Q_W), K_W, bn, one(lambda a: a), [F32],
                                             cast=[(w_o_b[0], 32)], name="proj_k")
    (v_p,), (v_s,), (woutb,) = _stream_matmul(hp, hs, w_inb, col(Q_W + K_W), V_W, bn, one(lambda a: a), [F32],
                                              cast=[(w_out[0], 32)], name="proj_v")
    (uv_p,), (uv_s,), (wgb, wdb) = _stream_matmul(hp, hs, w_inb, col(Q_W + K_W + V_W), 2 * SGU_W, bn,
                                                  one(_gelu_exact), [F32],
                                                  cast=[(w_gate[0], 128), (w_down[0], 86)], name="proj_uv")
    (gt_p,), (gt_s,), (wub,) = _stream_matmul(hp, hs, w_inb, col(Q_W + K_W + V_W + 2 * SGU_W), 2 * D_MODEL, bn,
                                              one(jax.nn.sigmoid), [F32], cast=[(w_up[0], 128)], name="proj_gates")

    ya_p = _attn_prompt(q_p, k_p, v_p, lamv, gsub)
    ya_s = _attn_sample(q_s, k_s, v_s, cache_k, cache_v, page_table, lamv, gsub)

    yb_p, vn_p = _sgu_prompt(uv_p, w_spatial[0], b_spatial[0], g_sgu[0])
    yb_s, vn_s = _sgu_sample(uv_s, w_spatial[0], b_spatial[0], g_sgu[0])

    (ta_p,), (ta_s,), _ = _stream_matmul(ya_p, ya_s, woab, 0, D_MODEL, bn,
                                         lambda accs, ex: [ex[0] * accs[0]], [F32],
                                         extras=[(gt_p, gt_s)], ex_off=[0], name="merge_a")
    (z_p,), (z_s,), _ = _stream_matmul(yb_p, yb_s, wobb, 0, D_MODEL, bn,
                                       lambda accs, ex: [ex[1] + ex[0] * accs[0]], [BF16],
                                       extras=[(gt_p, gt_s), (ta_p, ta_s)], ex_off=[col(D_MODEL), 0], name="merge_b")
    (x1_p,), (x1_s,), _ = _stream_matmul(z_p, z_s, woutb, 0, D_MODEL, bn,
                                         lambda accs, ex: [ex[0] + accs[0]], [F32],
                                         extras=[(xp, xs)], ex_off=[0], name="out_proj")

    h2_p = _rmsnorm_rows(x1_p, g_ffn[0], BF16, "rms_ffn_p")
    h2_s = _rmsnorm_rows(x1_s, g_ffn[0], BF16, "rms_ffn_s")
    mid_p, mid_s, tail_p, a_s = _ffn_gate_up(h2_p, h2_s, wgb, wub, conv_w[0], conv_b[0], state_conv[0])
    x2_p = _ffn_down(mid_p, wdb, x1_p, 512, "ffn_down_p")
    x2_s = _ffn_down(mid_s, wdb, x1_s, ROWS_S, "ffn_down_s")

    y_p = _rmsnorm_rows(x2_p, g_final, F32, "rms_final_p")
    y_s = _rmsnorm_rows(x2_s, g_final, F32, "rms_final_s")

    return (
        y_p.reshape(BATCH, SEQ, D_MODEL),
        y_s.reshape(DEC_BATCH, DEC_SEQ, D_MODEL),
        k_p.reshape(1, BATCH, SEQ, N_KV_HEADS, 2, HEAD_DIM),
        v_p.reshape(1, BATCH, SEQ, N_KV_HEADS, V_DIM),
        k_s.reshape(1, DEC_BATCH, DEC_SEQ, N_KV_HEADS, 2, HEAD_DIM),
        v_s.reshape(1, DEC_BATCH, DEC_SEQ, N_KV_HEADS, V_DIM),
        vn_p.reshape(1, BATCH, CHUNK, SGU_W),
        vn_s.reshape(1, DEC_BATCH, DEC_SEQ, SGU_W),
        tail_p[:, TAIL - (CONV_W - 1):].reshape(1, BATCH, CONV_W - 1, D_FF),
        a_s.reshape(DEC_BATCH, DEC_SEQ, D_FF)[:, DEC_SEQ - (CONV_W - 1):].reshape(1, DEC_BATCH, CONV_W - 1, D_FF),
    )
```

```python
import functools
import math

import jax
import jax.numpy as jnp
from jax import lax
from jax.experimental import pallas as pl
from jax.experimental.pallas import tpu as pltpu

D_MODEL = 4096
BATCH = 4
SEQ = 2048
DEC_BATCH = 32
DEC_SEQ = 8
PAST_LEN = 8192
PAGE_SIZE = 128
N_HEADS = 16
N_KV_HEADS = 8
GROUP = N_HEADS // N_KV_HEADS
HEAD_DIM = D_MODEL // (2 * N_HEADS)
V_DIM = 2 * HEAD_DIM
Q_W = N_HEADS * 2 * HEAD_DIM
K_W = N_KV_HEADS * 2 * HEAD_DIM
V_W = N_KV_HEADS * V_DIM
SGU_W = D_MODEL
N_SGU_GROUPS = 8
SGU_GROUP_W = SGU_W // N_SGU_GROUPS
CHUNK = 128
D_FF = 11008
CONV_W = 3
SCALE = 1.0 / math.sqrt(HEAD_DIM)
LOG2E = math.log2(math.e)
NEG = -1e30
EPS = 1e-6
LAMBDA_INIT = 0.8 - 0.6 * math.exp(-0.3 * 0)

ROWS_P = BATCH * SEQ
ROWS_S = DEC_BATCH * DEC_SEQ
N_PAGES = PAST_LEN // PAGE_SIZE

V7X_VMEM_BYTES = 64 * 1024 * 1024
VMEM_CAP_BYTES = 58 * 1024 * 1024

BM = 1024
N_MP = ROWS_P // BM

F32 = jnp.float32
BF16 = jnp.bfloat16
_NT = (((1,), (1,)), ((), ()))


def _nbytes(shape, dtype):
    return math.prod(shape) * jnp.dtype(dtype).itemsize


def _params(sem, vmem_bytes):
    return pltpu.CompilerParams(dimension_semantics=sem,
                                vmem_limit_bytes=int(min(vmem_bytes, VMEM_CAP_BYTES)))


def _rms_kernel(x_ref, g_ref, o_ref):
    x = x_ref[...].astype(F32)
    ms = jnp.mean(x * x, axis=-1, keepdims=True)
    o_ref[...] = (x * lax.rsqrt(ms + EPS) * g_ref[...]).astype(o_ref.dtype)


def _rmsnorm_rows(x, g, out_dtype, name):
    n, d = x.shape
    rows = min(n, 256)
    vmem = 2 * (_nbytes((rows, d), x.dtype) + _nbytes((rows, d), out_dtype)) + 4 * _nbytes((rows, d), F32)
    return pl.pallas_call(
        _rms_kernel,
        grid=(n // rows,),
        in_specs=[pl.BlockSpec((rows, d), lambda i: (i, 0)), pl.BlockSpec((1, d), lambda i: (0, 0))],
        out_specs=pl.BlockSpec((rows, d), lambda i: (i, 0)),
        out_shape=jax.ShapeDtypeStruct((n, d), out_dtype),
        compiler_params=_params(("arbitrary",), vmem),
        name=name,
    )(x, g.reshape(1, d))


def _mms_kernel(*refs, n_ex, n_out, n_cast, epilogue):
    it = iter(refs)
    xp_ref, xs_ref, w_ref = next(it), next(it), next(it)
    exp_refs = [next(it) for _ in range(n_ex)]
    exs_refs = [next(it) for _ in range(n_ex)]
    cin_refs = [next(it) for _ in range(n_cast)]
    outp_refs = [next(it) for _ in range(n_out)]
    outs_refs = [next(it) for _ in range(n_out)]
    cout_refs = [next(it) for _ in range(n_cast)]

    for ci, co in zip(cin_refs, cout_refs):
        co[...] = ci[...].astype(BF16)

    def run(x_ref, ex_refs, out_refs):
        acc = jnp.dot(x_ref[...].astype(BF16), w_ref[...], preferred_element_type=F32)
        vals = epilogue([acc], [r[...] for r in ex_refs])
        for o_ref, v in zip(out_refs, vals):
            o_ref[...] = v.astype(o_ref.dtype)

    run(xp_ref, exp_refs, outp_refs)

    @pl.when(pl.program_id(0) == N_MP - 1)
    def _():
        run(xs_ref, exs_refs, outs_refs)


def _stream_matmul(xp, xs, wb, w_off, n_cols, bn, epilogue, out_dtypes, extras=(), ex_off=(), cast=(), name="mms"):
    k_dim = xp.shape[1]
    n_tiles = n_cols // bn
    n_ex, n_out, n_cast = len(extras), len(out_dtypes), len(cast)
    last = lambda m, n: jnp.where(m == N_MP - 1, n, 0)
    in_specs = [pl.BlockSpec((BM, k_dim), lambda m, n: (m, 0)),
                pl.BlockSpec((ROWS_S, k_dim), lambda m, n: (0, 0)),
                pl.BlockSpec((k_dim, bn), lambda m, n: (0, n + w_off))]
    for off in ex_off:
        in_specs.append(pl.BlockSpec((BM, bn), lambda m, n, off=off: (m, n + off)))
    for off in ex_off:
        in_specs.append(pl.BlockSpec((ROWS_S, bn), lambda m, n, off=off: (0, last(m, n) + off)))
    out_specs = ([pl.BlockSpec((BM, bn), lambda m, n: (m, n))] * n_out
                 + [pl.BlockSpec((ROWS_S, bn), lambda m, n: (0, last(m, n)))] * n_out)
    out_shape = ([jax.ShapeDtypeStruct((ROWS_P, n_cols), dt) for dt in out_dtypes]
                 + [jax.ShapeDtypeStruct((ROWS_S, n_cols), dt) for dt in out_dtypes])
    vmem = (2 * (_nbytes((BM, k_dim), xp.dtype) + _nbytes((ROWS_S, k_dim), xs.dtype))
            + 2 * _nbytes((k_dim, bn), BF16)
            + 2 * (n_ex + n_out) * _nbytes((BM + ROWS_S, bn), F32)
            + 3 * _nbytes((BM, bn), F32))
    for arr, n_chunks, width, col_blk in cast:
        assert n_chunks <= N_MP * n_tiles and arr.shape[0] % n_chunks == 0
        blk = (arr.shape[0] // n_chunks, width)
        step = lambda m, n, n_chunks=n_chunks: jnp.minimum(m * n_tiles + n, n_chunks - 1)
        in_specs.append(pl.BlockSpec(blk, lambda m, n, step=step, col_blk=col_blk: (step(m, n), col_blk)))
        out_specs.append(pl.BlockSpec(blk, lambda m, n, step=step: (step(m, n), 0)))
        out_shape.append(jax.ShapeDtypeStruct((arr.shape[0], width), BF16))
        vmem += 2 * (_nbytes(blk, F32) + _nbytes(blk, BF16)) + _nbytes(blk, F32)
    args = [xp, xs, wb] + [e[0] for e in extras] + [e[1] for e in extras] + [c[0] for c in cast]
    res = pl.pallas_call(
        functools.partial(_mms_kernel, n_ex=n_ex, n_out=n_out, n_cast=n_cast, epilogue=epilogue),
        grid=(N_MP, n_tiles),
        in_specs=in_specs,
        out_specs=out_specs,
        out_shape=out_shape,
        compiler_params=_params(("arbitrary", "arbitrary"), vmem),
        name=name,
    )(*args)
    return res[:n_out], res[n_out:2 * n_out], res[2 * n_out:]


def _gelu_exact(x):
    return 0.5 * x * (1.0 + lax.erf(x * math.sqrt(0.5)))


def _lambda_value(lv):
    a = jnp.sum(lv[0:1] * lv[1:2], axis=-1, keepdims=True)
    b = jnp.sum(lv[2:3] * lv[3:4], axis=-1, keepdims=True)
    return jnp.exp(a) - jnp.exp(b) + LAMBDA_INIT


def _head_norm(o, g):
    ms = jnp.mean(o * o, axis=-1, keepdims=True)
    return o * lax.rsqrt(ms + EPS) * g * (1.0 - LAMBDA_INIT)


def _attn_prompt_kernel(q_ref, k_ref, v_ref, lam_ref, g_ref, o_ref, kb, vb, *, bq):
    def cast_rows(i, carry):
        r = pl.multiple_of(i * 256, 256)
        kb[pl.ds(r, 256), :] = k_ref[pl.ds(r, 256), :].astype(BF16)
        vb[pl.ds(r, 256), :] = v_ref[pl.ds(r, 256), :].astype(BF16)
        return carry
    lax.fori_loop(0, SEQ // 256, cast_rows, 0)

    lam = _lambda_value(lam_ref[...])
    row = lax.broadcasted_iota(jnp.int32, (GROUP * bq, bq), 0) & (bq - 1)
    col = lax.broadcasted_iota(jnp.int32, (GROUP * bq, bq), 1)
    causal = col <= row
    for qi in range(SEQ // bq):
        keys = (qi + 1) * bq
        q = q_ref[qi * bq:(qi + 1) * bq, :]
        o_maps = []
        for c in range(2):
            qc = jnp.concatenate([q[:, (g * 2 + c) * HEAD_DIM:(g * 2 + c + 1) * HEAD_DIM] for g in range(GROUP)],
                                 axis=0)
            s = lax.dot_general(qc, kb[0:keys, c * HEAD_DIM:(c + 1) * HEAD_DIM], _NT, preferred_element_type=F32)
            diag = jnp.where(causal, s[:, keys - bq:], NEG)
            s = diag if qi == 0 else jnp.concatenate([s[:, :keys - bq], diag], axis=1)
            p = jnp.exp2(s - jnp.max(s, axis=-1, keepdims=True))
            l = jnp.sum(p, axis=-1, keepdims=True)
            o_maps.append(jnp.dot(p.astype(BF16), vb[0:keys, :], preferred_element_type=F32) / l)
        o = o_maps[0] - lam * o_maps[1]
        for g in range(GROUP):
            o_ref[qi * bq:(qi + 1) * bq, g * V_DIM:(g + 1) * V_DIM] = _head_norm(
                o[g * bq:(g + 1) * bq], g_ref[...]).astype(o_ref.dtype)


def _attn_prompt(q, k, v, lamv, g_subln):
    bq = 256
    qw = GROUP * 2 * HEAD_DIM
    vmem = (2 * (2 * _nbytes((SEQ, qw), BF16) + 2 * _nbytes((SEQ, V_DIM), F32))
            + 2 * _nbytes((SEQ, V_DIM), BF16) + 10 * _nbytes((GROUP * bq, SEQ), F32))
    return pl.pallas_call(
        functools.partial(_attn_prompt_kernel, bq=bq),
        grid=(BATCH, N_KV_HEADS),
        in_specs=[
            pl.BlockSpec((SEQ, qw), lambda b, h: (b, h)),
            pl.BlockSpec((SEQ, 2 * HEAD_DIM), lambda b, h: (b, h)),
            pl.BlockSpec((SEQ, V_DIM), lambda b, h: (b, h)),
            pl.BlockSpec((4, HEAD_DIM), lambda b, h: (0, 0)),
            pl.BlockSpec((1, V_DIM), lambda b, h: (0, 0)),
        ],
        out_specs=pl.BlockSpec((SEQ, GROUP * V_DIM), lambda b, h: (b, h)),
        out_shape=jax.ShapeDtypeStruct((ROWS_P, N_HEADS * V_DIM), BF16),
        scratch_shapes=[pltpu.VMEM((SEQ, 2 * HEAD_DIM), BF16), pltpu.VMEM((SEQ, V_DIM), BF16)],
        compiler_params=_params(("arbitrary", "arbitrary"), vmem),
        name="attn_prompt",
    )(q, k, v, lamv, g_subln)


PAGES_PER_STEP = 8
QROWS = 2 * GROUP * DEC_SEQ
PAIRS = N_KV_HEADS // 2
PROWS = 2 * QROWS


def _attn_sample_kernel(pt_ref, qbd_ref, knew_ref, vnew_ref, lam_ref, g_ref, *rest):
    del pt_ref
    kp_refs = rest[:PAGES_PER_STEP]
    vp_refs = rest[PAGES_PER_STEP:3 * PAGES_PER_STEP]
    o_ref, m_s, l_s, acc_s = rest[3 * PAGES_PER_STEP:]
    j = pl.program_id(1)

    @pl.when(j == 0)
    def _():
        m_s[...] = jnp.full(m_s.shape, NEG, F32)
        l_s[...] = jnp.zeros(l_s.shape, F32)
        acc_s[...] = jnp.zeros(acc_s.shape, F32)

    def scores(k_tiles):
        cols = []
        for kt in k_tiles:
            rows = [lax.dot_general(qbd_ref[0, hp], kt(hp), _NT, preferred_element_type=F32)
                    for hp in range(PAIRS)]
            cols.append(jnp.concatenate(rows, axis=0))
        return cols[0] if len(cols) == 1 else jnp.concatenate(cols, axis=1)

    def update(s, v_tiles, width):
        m_prev = m_s[...]
        m_new = jnp.maximum(m_prev, jnp.max(s, axis=-1, keepdims=True))
        alpha = jnp.exp2(m_prev - m_new)
        p = jnp.exp2(s - m_new)
        l_s[...] = alpha * l_s[...] + jnp.sum(p, axis=-1, keepdims=True)
        pb = p.astype(BF16)
        pv = []
        for hp in range(PAIRS):
            acc = None
            for i, vt in enumerate(v_tiles):
                d = jnp.dot(pb[hp * PROWS:(hp + 1) * PROWS, i * width:(i + 1) * width], vt(hp),
                            preferred_element_type=F32)
                acc = d if acc is None else acc + d
            pv.append(acc)
        acc_s[...] = alpha * acc_s[...] + jnp.concatenate(pv, axis=0)
        m_s[...] = m_new

    def k_page(ref):
        def get(hp):
            maps = [ref[pl.ds(2 * hp + c, 2 * PAGE_SIZE, stride=N_KV_HEADS), :] for c in range(2)]
            return jnp.concatenate(maps, axis=1).astype(BF16)
        return get

    def v_page(ref_lo, ref_hi):
        def get(hp):
            halves = [r[pl.ds(hp, 2 * PAGE_SIZE, stride=PAIRS), :] for r in (ref_lo, ref_hi)]
            return jnp.concatenate(halves, axis=1).astype(BF16)
        return get

    def which_head(shape):
        return (lax.broadcasted_iota(jnp.int32, shape, 0) >> int(math.log2(QROWS))) & 1

    s = scores([k_page(r) for r in kp_refs])
    col = lax.broadcasted_iota(jnp.int32, s.shape, 1)
    s = jnp.where((col & 1) == which_head(s.shape), s, NEG)
    update(s, [v_page(vp_refs[2 * p], vp_refs[2 * p + 1]) for p in range(PAGES_PER_STEP)], 2 * PAGE_SIZE)

    @pl.when(j == pl.num_programs(1) - 1)
    def _():
        pad = jnp.zeros((PAGE_SIZE - 2 * DEC_SEQ, 2 * HEAD_DIM), F32)

        def new_tile(ref):
            def get(hp):
                parts = [ref[:, h * V_DIM:(h + 1) * V_DIM] for h in (hp, hp + PAIRS)]
                return jnp.concatenate(parts + [pad], axis=0).astype(BF16)
            return get

        s = scores([new_tile(knew_ref)])
        t = lax.broadcasted_iota(jnp.int32, s.shape, 0) & (DEC_SEQ - 1)
        col = lax.broadcasted_iota(jnp.int32, s.shape, 1)
        ok = ((col >> int(math.log2(DEC_SEQ))) == which_head(s.shape)) & ((col & (DEC_SEQ - 1)) <= t)
        update(jnp.where(ok, s, NEG), [new_tile(vnew_ref)], PAGE_SIZE)

        lam = _lambda_value(lam_ref[...])
        o = acc_s[...] / l_s[...]
        half = GROUP * DEC_SEQ
        for hp in range(PAIRS):
            for hi in range(2):
                r0 = hp * PROWS + hi * QROWS
                d = o[r0:r0 + half] - lam * o[r0 + half:r0 + QROWS]
                y = _head_norm(d, g_ref[...])
                for g in range(GROUP):
                    c0 = ((hi * PAIRS + hp) * GROUP + g) * V_DIM
                    o_ref[:, c0:c0 + V_DIM] = y[g * DEC_SEQ:(g + 1) * DEC_SEQ]


def _attn_sample(q_s, k_s, v_s, cache_k, cache_v, page_table, lamv, g_subln):
    q6 = q_s.reshape(DEC_BATCH, DEC_SEQ, N_KV_HEADS, GROUP, 2, HEAD_DIM).transpose(0, 2, 4, 3, 1, 5)
    zero = jnp.zeros_like(q6[:, :, 0])
    top = jnp.concatenate([q6[:, :, 0], zero], axis=-1).reshape(DEC_BATCH, N_KV_HEADS, QROWS // 2, 2 * HEAD_DIM)
    bot = jnp.concatenate([zero, q6[:, :, 1]], axis=-1).reshape(DEC_BATCH, N_KV_HEADS, QROWS // 2, 2 * HEAD_DIM)
    qbd = jnp.concatenate([top, bot], axis=2)
    qbd = qbd.reshape(DEC_BATCH, 2, PAIRS, QROWS, 2 * HEAD_DIM).transpose(0, 2, 1, 3, 4).reshape(
        DEC_BATCH, PAIRS, PROWS, 2 * HEAD_DIM)
    n_pool = cache_k.shape[1]
    ck = cache_k.reshape(n_pool, PAGE_SIZE * N_KV_HEADS * 2, HEAD_DIM)
    cv = cache_v.reshape(n_pool, PAGE_SIZE * N_KV_HEADS, V_DIM)
    pps = PAGES_PER_STEP

    def k_spec(p):
        return pl.BlockSpec((None, ck.shape[1], HEAD_DIM), lambda b, j, pt: (pt[b, j * pps + p], 0, 0))

    def v_spec(p, half):
        return pl.BlockSpec((None, cv.shape[1], HEAD_DIM), lambda b, j, pt: (pt[b, j * pps + p], 0, half))

    in_specs = [
        pl.BlockSpec((1, PAIRS, PROWS, 2 * HEAD_DIM), lambda b, j, pt: (b, 0, 0, 0)),
        pl.BlockSpec((DEC_SEQ, K_W), lambda b, j, pt: (b, 0)),
        pl.BlockSpec((DEC_SEQ, V_W), lambda b, j, pt: (b, 0)),
        pl.BlockSpec((4, HEAD_DIM), lambda b, j, pt: (0, 0)),
        pl.BlockSpec((1, V_DIM), lambda b, j, pt: (0, 0)),
    ] + [k_spec(p) for p in range(pps)] + [v_spec(p, half) for p in range(pps) for half in range(2)]
    rows = N_KV_HEADS * QROWS
    vmem = (2 * 2 * pps * _nbytes((PAGE_SIZE, K_W), F32) + 2 * pps * _nbytes((PAGE_SIZE, K_W), BF16)
            + 6 * _nbytes((rows, pps * 2 * PAGE_SIZE), F32) + 4 * _nbytes((rows, V_DIM), F32)
            + 4 * _nbytes((DEC_SEQ, Q_W), F32))
    return pl.pallas_call(
        _attn_sample_kernel,
        grid_spec=pltpu.PrefetchScalarGridSpec(
            num_scalar_prefetch=1,
            grid=(DEC_BATCH, N_PAGES // pps),
            in_specs=in_specs,
            out_specs=pl.BlockSpec((DEC_SEQ, N_HEADS * V_DIM), lambda b, j, pt: (b, 0)),
            scratch_shapes=[pltpu.VMEM((rows, 1), F32), pltpu.VMEM((rows, 1), F32),
                            pltpu.VMEM((rows, V_DIM), F32)],
        ),
        out_shape=jax.ShapeDtypeStruct((ROWS_S, N_HEADS * V_DIM), F32),
        compiler_params=_params(("arbitrary", "arbitrary"), vmem),
        name="attn_sample",
    )(page_table, qbd, k_s, v_s, lamv, g_subln, *([ck] * pps), *([cv] * (2 * pps)))


def _sgu_kernel(u_ref, v_ref, w_ref, b_ref, g_ref, yb_ref, vn_ref, *, t, last_only):
    v = v_ref[...]
    ms = jnp.mean(v * v, axis=-1, keepdims=True)
    vn = v * lax.rsqrt(ms + EPS) * g_ref[...]
    if last_only:
        @pl.when(pl.program_id(1) == pl.num_programs(1) - 1)
        def _():
            vn_ref[...] = vn
    else:
        vn_ref[...] = vn
    r = v.shape[0]
    row = lax.broadcasted_iota(jnp.int32, (r, r), 0)
    col = lax.broadcasted_iota(jnp.int32, (r, r), 1)
    keep = (col <= row) & (col >= row - (row & (t - 1)))
    for g in range(N_SGU_GROUPS):
        cs = slice(g * SGU_GROUP_W, (g + 1) * SGU_GROUP_W)
        w = jnp.where(keep, w_ref[g], 0.0).astype(BF16)
        mixed = jnp.dot(w, vn[:, cs].astype(BF16), preferred_element_type=F32) + b_ref[g]
        yb_ref[:, cs] = (u_ref[:, cs] * mixed).astype(yb_ref.dtype)


def _sgu_prompt(uv, w_spatial, b_spatial, g_sgu):
    nch = SEQ // CHUNK
    vmem = 2 * (3 * _nbytes((CHUNK, SGU_W), F32) + _nbytes((CHUNK, SGU_W), BF16)) + 6 * _nbytes((CHUNK, SGU_W), F32)
    return pl.pallas_call(
        functools.partial(_sgu_kernel, t=CHUNK, last_only=True),
        grid=(BATCH, nch),
        in_specs=[
            pl.BlockSpec((CHUNK, SGU_W), lambda b, c: (b * nch + c, 0)),
            pl.BlockSpec((CHUNK, SGU_W), lambda b, c: (b * nch + c, 1)),
            pl.BlockSpec((N_SGU_GROUPS, CHUNK, CHUNK), lambda b, c: (0, 0, 0)),
            pl.BlockSpec((N_SGU_GROUPS, CHUNK, 1), lambda b, c: (0, 0, 0)),
            pl.BlockSpec((1, SGU_W), lambda b, c: (0, 0)),
        ],
        out_specs=[pl.BlockSpec((CHUNK, SGU_W), lambda b, c: (b * nch + c, 0)),
                   pl.BlockSpec((None, CHUNK, SGU_W), lambda b, c: (b, 0, 0))],
        out_shape=[jax.ShapeDtypeStruct((ROWS_P, SGU_W), BF16),
                   jax.ShapeDtypeStruct((BATCH, CHUNK, SGU_W), F32)],
        compiler_params=_params(("arbitrary", "arbitrary"), vmem),
        name="sgu_prompt",
    )(uv, uv, w_spatial, b_spatial[:, :CHUNK, None], g_sgu.reshape(1, SGU_W))


def _sgu_sample(uv, w_spatial, b_spatial, g_sgu):
    reps = ROWS_S // DEC_SEQ
    w = jnp.concatenate([w_spatial[:, :DEC_SEQ, :DEC_SEQ]] * reps, axis=2)
    w = jnp.broadcast_to(w[:, None], (N_SGU_GROUPS, reps, DEC_SEQ, ROWS_S)).reshape(N_SGU_GROUPS, ROWS_S, ROWS_S)
    b = jnp.tile(b_spatial[:, :DEC_SEQ], (1, reps))[:, :, None]
    vmem = 2 * (4 * _nbytes((ROWS_S, SGU_W), F32)) + 6 * _nbytes((ROWS_S, SGU_W), F32)
    return pl.pallas_call(
        functools.partial(_sgu_kernel, t=DEC_SEQ, last_only=False),
        grid=(1, 1),
        in_specs=[
            pl.BlockSpec((ROWS_S, SGU_W), lambda b, c: (0, 0)),
            pl.BlockSpec((ROWS_S, SGU_W), lambda b, c: (0, 1)),
            pl.BlockSpec((N_SGU_GROUPS, ROWS_S, ROWS_S), lambda b, c: (0, 0, 0)),
            pl.BlockSpec((N_SGU_GROUPS, ROWS_S, 1), lambda b, c: (0, 0, 0)),
            pl.BlockSpec((1, SGU_W), lambda b, c: (0, 0)),
        ],
        out_specs=[pl.BlockSpec((ROWS_S, SGU_W), lambda b, c: (0, 0)),
                   pl.BlockSpec((ROWS_S, SGU_W), lambda b, c: (0, 0))],
        out_shape=[jax.ShapeDtypeStruct((ROWS_S, SGU_W), BF16),
                   jax.ShapeDtypeStruct((ROWS_S, SGU_W), F32)],
        compiler_params=_params(("arbitrary", "arbitrary"), vmem),
        name="sgu_sample",
    )(uv, uv, w, b, g_sgu.reshape(1, SGU_W))


TAIL = 8


def _ffn_kernel(xp_ref, xs_ref, wg_ref, wu_ref, cw_ref, cb_ref, p1_ref, p2_ref,
                midp_ref, mids_ref, tail_ref, as_ref, carry):
    m, n = pl.program_id(0), pl.program_id(1)
    tiles_per_seq = SEQ // BM

    def act(a, sh1, sh2, up):
        cw = cw_ref[...]
        c = cb_ref[...] + ((cw[0:1] * sh2 + cw[1:2] * sh1) + cw[2:3] * a)
        return jax.nn.silu(c) * up

    x = xp_ref[...]
    a = jnp.dot(x, wg_ref[...], preferred_element_type=F32)
    up = jnp.dot(x, wu_ref[...], preferred_element_type=F32)

    @pl.when(m % tiles_per_seq == 0)
    def _():
        carry[n] = jnp.zeros(carry.shape[1:], F32)

    prev = carry[n]
    row = lax.broadcasted_iota(jnp.int32, a.shape, 0)
    sh1 = jnp.where(row == 0, prev[TAIL - 1:TAIL], pltpu.roll(a, 1, 0))
    sh2 = jnp.where(row == 0, prev[TAIL - 2:TAIL - 1],
                    jnp.where(row == 1, prev[TAIL - 1:TAIL], pltpu.roll(a, 2, 0)))
    midp_ref[...] = act(a, sh1, sh2, up).astype(midp_ref.dtype)
    carry[n] = a[BM - TAIL:]

    @pl.when(m % tiles_per_seq == tiles_per_seq - 1)
    def _():
        tail_ref[...] = a[BM - TAIL:]

    @pl.when(m == N_MP - 1)
    def _():
        xs = xs_ref[...]
        a_s = jnp.dot(xs, wg_ref[...], preferred_element_type=F32)
        up_s = jnp.dot(xs, wu_ref[...], preferred_element_type=F32)
        t = lax.broadcasted_iota(jnp.int32, a_s.shape, 0) & (DEC_SEQ - 1)
        s1 = jnp.where(t == 0, p1_ref[...], pltpu.roll(a_s, 1, 0))
        s2 = jnp.where(t < 2, p2_ref[...], pltpu.roll(a_s, 2, 0))
        mids_ref[...] = act(a_s, s1, s2, up_s).astype(mids_ref.dtype)
        as_ref[...] = a_s


def _ffn_gate_up(hp, hs, wg, wu, conv_w, conv_b, state):
    bn = 256
    n_tiles = D_FF // bn
    prev1 = jnp.pad(state[:, 1:2], ((0, 0), (0, DEC_SEQ - 1), (0, 0))).reshape(ROWS_S, D_FF)
    prev2 = jnp.pad(state, ((0, 0), (0, DEC_SEQ - 2), (0, 0))).reshape(ROWS_S, D_FF)
    tiles_per_seq = SEQ // BM
    last = lambda m, n: jnp.where(m == N_MP - 1, n, 0)
    seq_end = lambda m, n: jnp.where(m % tiles_per_seq == tiles_per_seq - 1, n, 0)
    vmem = (2 * _nbytes((BM + ROWS_S, D_MODEL), BF16) + 2 * 2 * _nbytes((D_MODEL, bn), BF16)
            + 2 * 4 * _nbytes((BM + ROWS_S, bn), F32) + 12 * _nbytes((BM, bn), F32)
            + _nbytes((n_tiles, TAIL, bn), F32))
    return pl.pallas_call(
        _ffn_kernel,
        grid=(N_MP, n_tiles),
        in_specs=[
            pl.BlockSpec((BM, D_MODEL), lambda m, n: (m, 0)),
            pl.BlockSpec((ROWS_S, D_MODEL), lambda m, n: (0, 0)),
            pl.BlockSpec((D_MODEL, bn), lambda m, n: (0, n)),
            pl.BlockSpec((D_MODEL, bn), lambda m, n: (0, n)),
            pl.BlockSpec((CONV_W, bn), lambda m, n: (0, n)),
            pl.BlockSpec((1, bn), lambda m, n: (0, n)),
            pl.BlockSpec((ROWS_S, bn), lambda m, n: (0, last(m, n))),
            pl.BlockSpec((ROWS_S, bn), lambda m, n: (0, last(m, n))),
        ],
        out_specs=[
            pl.BlockSpec((BM, bn), lambda m, n: (m, n)),
            pl.BlockSpec((ROWS_S, bn), lambda m, n: (0, last(m, n))),
            pl.BlockSpec((None, TAIL, bn), lambda m, n: (m // tiles_per_seq, 0, seq_end(m, n))),
            pl.BlockSpec((ROWS_S, bn), lambda m, n: (0, last(m, n))),
        ],
        out_shape=[
            jax.ShapeDtypeStruct((ROWS_P, D_FF), BF16),
            jax.ShapeDtypeStruct((ROWS_S, D_FF), BF16),
            jax.ShapeDtypeStruct((BATCH, TAIL, D_FF), F32),
            jax.ShapeDtypeStruct((ROWS_S, D_FF), F32),
        ],
        scratch_shapes=[pltpu.VMEM((n_tiles, TAIL, bn), F32)],
        compiler_params=_params(("arbitrary", "arbitrary"), vmem),
        name="ffn_gate_up",
    )(hp, hs, wg, wu, conv_w, conv_b.reshape(1, D_FF), prev1, prev2)


def _cast_kernel(x_ref, o_ref):
    o_ref[...] = x_ref[...].astype(o_ref.dtype)


def _cast_bf16(w, n, name):
    k = w.shape[0]
    rows = 256
    assert k % rows == 0
    vmem = 2 * (_nbytes((rows, n), F32) + _nbytes((rows, n), BF16)) + _nbytes((rows, n), F32)
    return pl.pallas_call(
        _cast_kernel,
        grid=(k // rows,),
        in_specs=[pl.BlockSpec((rows, n), lambda i: (i, 0))],
        out_specs=pl.BlockSpec((rows, n), lambda i: (i, 0)),
        out_shape=jax.ShapeDtypeStruct((k, n), BF16),
        compiler_params=_params(("arbitrary",), vmem),
        name=name,
    )(w)


def _down_kernel(x_ref, w_ref, r_ref, o_ref):
    o_ref[...] = r_ref[...] + jnp.dot(x_ref[...], w_ref[...], preferred_element_type=F32)


def _ffn_down(mid, wb, res, bm, name):
    rows = mid.shape[0]
    bn = 512
    vmem = (2 * (_nbytes((bm, D_FF), BF16) + _nbytes((D_FF, bn), BF16)) + 4 * _nbytes((bm, bn), F32)
            + 2 * _nbytes((bm, bn), F32))
    return pl.pallas_call(
        _down_kernel,
        grid=(rows // bm, D_MODEL // bn),
        in_specs=[
            pl.BlockSpec((bm, D_FF), lambda m, n: (m, 0)),
            pl.BlockSpec((D_FF, bn), lambda m, n: (0, n)),
            pl.BlockSpec((bm, bn), lambda m, n: (m, n)),
        ],
        out_specs=pl.BlockSpec((bm, bn), lambda m, n: (m, n)),
        out_shape=jax.ShapeDtypeStruct((rows, D_MODEL), F32),
        compiler_params=_params(("arbitrary", "arbitrary"), vmem),
        name=name,
    )(mid, wb, res)


def kernel(x_prompt, x_sample, cache_k, cache_v, state_conv, page_table, g_attn, w_in, lam_q1, lam_k1, lam_q2, lam_k2, g_subln, g_sgu, w_spatial, b_spatial, w_o_a, w_o_b, w_out, g_ffn, w_gate, conv_w, conv_b, w_up, w_down, g_final):
    xp = x_prompt.reshape(ROWS_P, D_MODEL)
    xs = x_sample.reshape(ROWS_S, D_MODEL)
    lamv = jnp.stack([lam_q1[0], lam_k1[0], lam_q2[0], lam_k2[0]])
    gsub = g_subln[0].reshape(1, V_DIM)

    hp = _rmsnorm_rows(xp, g_attn[0], BF16, "rms_attn_p")
    hs = _rmsnorm_rows(xs, g_attn[0], BF16, "rms_attn_s")

    bn = 512
    one = lambda f: (lambda accs, ex: [f(accs[0])])
    col = lambda c: c // bn
    whole = lambda w, n_chunks: (w, n_chunks, w.shape[1], 0)
    w_in0 = w_in[0]
    w_qb = _cast_bf16(w_in0, Q_W, "cast_w_q")
    (q_p,), (q_s,), (w_kvb, w_uvb, w_gtb) = _stream_matmul(
        hp, hs, w_qb, 0, Q_W, 256, one(lambda a: a * (SCALE * LOG2E)), [BF16],
        cast=[(w_in0, 128, K_W + V_W, 1), (w_in0, 128, 2 * SGU_W, 1), (w_in0, 128, 2 * D_MODEL, 2)], name="proj_q")
    (k_p,), (k_s,), (wobb,) = _stream_matmul(hp, hs, w_kvb, 0, K_W, bn, one(lambda a: a), [F32],
                                             cast=[whole(w_o_b[0], 32)], name="proj_k")
    (v_p,), (v_s,), (woutb,) = _stream_matmul(hp, hs, w_kvb, col(K_W), V_W, bn, one(lambda a: a), [F32],
                                              cast=[whole(w_out[0], 32)], name="proj_v")
    (uv_p,), (uv_s,), (wgb, wdb, woab) = _stream_matmul(
        hp, hs, w_uvb, 0, 2 * SGU_W, bn, one(_gelu_exact), [F32],
        cast=[whole(w_gate[0], 128), whole(w_down[0], 86), whole(w_o_a[0], 128)], name="proj_uv")
    (gt_p,), (gt_s,), (wub,) = _stream_matmul(hp, hs, w_gtb, 0, 2 * D_MODEL, bn, one(jax.nn.sigmoid), [F32],
                                              cast=[whole(w_up[0], 128)], name="proj_gates")

    ya_p = _attn_prompt(q_p, k_p, v_p, lamv, gsub)
    ya_s = _attn_sample(q_s, k_s, v_s, cache_k, cache_v, page_table, lamv, gsub)

    yb_p, vn_p = _sgu_prompt(uv_p, w_spatial[0], b_spatial[0], g_sgu[0])
    yb_s, vn_s = _sgu_sample(uv_s, w_spatial[0], b_spatial[0], g_sgu[0])

    (ta_p,), (ta_s,), _ = _stream_matmul(ya_p, ya_s, woab, 0, D_MODEL, bn,
                                         lambda accs, ex: [ex[0] * accs[0]], [F32],
                                         extras=[(gt_p, gt_s)], ex_off=[0], name="merge_a")
    (z_p,), (z_s,), _ = _stream_matmul(yb_p, yb_s, wobb, 0, D_MODEL, bn,
                                       lambda accs, ex: [ex[1] + ex[0] * accs[0]], [BF16],
                                       extras=[(gt_p, gt_s), (ta_p, ta_s)], ex_off=[col(D_MODEL), 0], name="merge_b")
    (x1_p,), (x1_s,), _ = _stream_matmul(z_p, z_s, woutb, 0, D_MODEL, bn,
                                         lambda accs, ex: [ex[0] + accs[0]], [F32],
                                         extras=[(xp, xs)], ex_off=[0], name="out_proj")

    h2_p = _rmsnorm_rows(x1_p, g_ffn[0], BF16, "rms_ffn_p")
    h2_s = _rmsnorm_rows(x1_s, g_ffn[0], BF16, "rms_ffn_s")
    mid_p, mid_s, tail_p, a_s = _ffn_gate_up(h2_p, h2_s, wgb, wub, conv_w[0], conv_b[0], state_conv[0])
    x2_p = _ffn_down(mid_p, wdb, x1_p, 512, "ffn_down_p")
    x2_s = _ffn_down(mid_s, wdb, x1_s, ROWS_S, "ffn_down_s")

    y_p = _rmsnorm_rows(x2_p, g_final, F32, "rms_final_p")
    y_s = _rmsnorm_rows(x2_s, g_final, F32, "rms_final_s")

    return (
        y_p.reshape(BATCH, SEQ, D_MODEL),
        y_s.reshape(DEC_BATCH, DEC_SEQ, D_MODEL),
        k_p.reshape(1, BATCH, SEQ, N_KV_HEADS, 2, HEAD_DIM),
        v_p.reshape(1, BATCH, SEQ, N_KV_HEADS, V_DIM),
        k_s.reshape(1, DEC_BATCH, DEC_SEQ, N_KV_HEADS, 2, HEAD_DIM),
        v_s.reshape(1, DEC_BATCH, DEC_SEQ, N_KV_HEADS, V_DIM),
        vn_p.reshape(1, BATCH, CHUNK, SGU_W),
        vn_s.reshape(1, DEC_BATCH, DEC_SEQ, SGU_W),
        tail_p[:, TAIL - (CONV_W - 1):].reshape(1, BATCH, CONV_W - 1, D_FF),
        a_s.reshape(DEC_BATCH, DEC_SEQ, D_FF)[:, DEC_SEQ - (CONV_W - 1):].reshape(1, DEC_BATCH, CONV_W - 1, D_FF),
    )
```

```python
import functools
import math

import jax
import jax.numpy as jnp
from jax import lax
from jax.experimental import pallas as pl
from jax.experimental.pallas import tpu as pltpu

D_MODEL = 4096
BATCH = 4
SEQ = 2048
DEC_BATCH = 32
DEC_SEQ = 8
PAST_LEN = 8192
PAGE_SIZE = 128
N_HEADS = 16
N_KV_HEADS = 8
GROUP = N_HEADS // N_KV_HEADS
HEAD_DIM = D_MODEL // (2 * N_HEADS)
V_DIM = 2 * HEAD_DIM
Q_W = N_HEADS * 2 * HEAD_DIM
K_W = N_KV_HEADS * 2 * HEAD_DIM
V_W = N_KV_HEADS * V_DIM
SGU_W = D_MODEL
N_SGU_GROUPS = 8
SGU_GROUP_W = SGU_W // N_SGU_GROUPS
CHUNK = 128
D_FF = 11008
CONV_W = 3
SCALE = 1.0 / math.sqrt(HEAD_DIM)
LOG2E = math.log2(math.e)
NEG = -1e30
EPS = 1e-6
LAMBDA_INIT = 0.8 - 0.6 * math.exp(-0.3 * 0)

ROWS_P = BATCH * SEQ
ROWS_S = DEC_BATCH * DEC_SEQ
N_PAGES = PAST_LEN // PAGE_SIZE

V7X_VMEM_BYTES = 64 * 1024 * 1024
VMEM_CAP_BYTES = 58 * 1024 * 1024

BM = 1024
N_MP = ROWS_P // BM

F32 = jnp.float32
BF16 = jnp.bfloat16
_NT = (((1,), (1,)), ((), ()))


def _nbytes(shape, dtype):
    return math.prod(shape) * jnp.dtype(dtype).itemsize


def _params(sem, vmem_bytes):
    return pltpu.CompilerParams(dimension_semantics=sem,
                                vmem_limit_bytes=int(min(vmem_bytes, VMEM_CAP_BYTES)))


def _rms_kernel(x_ref, g_ref, o_ref):
    x = x_ref[...].astype(F32)
    ms = jnp.mean(x * x, axis=-1, keepdims=True)
    o_ref[...] = (x * lax.rsqrt(ms + EPS) * g_ref[...]).astype(o_ref.dtype)


def _rmsnorm_rows(x, g, out_dtype, name):
    n, d = x.shape
    rows = min(n, 256)
    vmem = 2 * (_nbytes((rows, d), x.dtype) + _nbytes((rows, d), out_dtype)) + 4 * _nbytes((rows, d), F32)
    return pl.pallas_call(
        _rms_kernel,
        grid=(n // rows,),
        in_specs=[pl.BlockSpec((rows, d), lambda i: (i, 0)), pl.BlockSpec((1, d), lambda i: (0, 0))],
        out_specs=pl.BlockSpec((rows, d), lambda i: (i, 0)),
        out_shape=jax.ShapeDtypeStruct((n, d), out_dtype),
        compiler_params=_params(("arbitrary",), vmem),
        name=name,
    )(x, g.reshape(1, d))


def _mms_kernel(*refs, n_ex, n_out, n_cast, epilogue):
    it = iter(refs)
    xp_ref, xs_ref, w_ref = next(it), next(it), next(it)
    exp_refs = [next(it) for _ in range(n_ex)]
    exs_refs = [next(it) for _ in range(n_ex)]
    cin_refs = [next(it) for _ in range(n_cast)]
    outp_refs = [next(it) for _ in range(n_out)]
    outs_refs = [next(it) for _ in range(n_out)]
    cout_refs = [next(it) for _ in range(n_cast)]

    for ci, co in zip(cin_refs, cout_refs):
        co[...] = ci[...].astype(BF16)

    def run(x_ref, ex_refs, out_refs):
        acc = jnp.dot(x_ref[...].astype(BF16), w_ref[...], preferred_element_type=F32)
        vals = epilogue([acc], [r[...] for r in ex_refs])
        for o_ref, v in zip(out_refs, vals):
            o_ref[...] = v.astype(o_ref.dtype)

    run(xp_ref, exp_refs, outp_refs)

    @pl.when(pl.program_id(0) == N_MP - 1)
    def _():
        run(xs_ref, exs_refs, outs_refs)


def _stream_matmul(xp, xs, wb, w_off, n_cols, bn, epilogue, out_dtypes, extras=(), ex_off=(), cast=(), name="mms"):
    k_dim = xp.shape[1]
    n_tiles = n_cols // bn
    n_ex, n_out, n_cast = len(extras), len(out_dtypes), len(cast)
    last = lambda m, n: jnp.where(m == N_MP - 1, n, 0)
    in_specs = [pl.BlockSpec((BM, k_dim), lambda m, n: (m, 0)),
                pl.BlockSpec((ROWS_S, k_dim), lambda m, n: (0, 0)),
                pl.BlockSpec((k_dim, bn), lambda m, n: (0, n + w_off))]
    for off in ex_off:
        in_specs.append(pl.BlockSpec((BM, bn), lambda m, n, off=off: (m, n + off)))
    for off in ex_off:
        in_specs.append(pl.BlockSpec((ROWS_S, bn), lambda m, n, off=off: (0, last(m, n) + off)))
    out_specs = ([pl.BlockSpec((BM, bn), lambda m, n: (m, n))] * n_out
                 + [pl.BlockSpec((ROWS_S, bn), lambda m, n: (0, last(m, n)))] * n_out)
    out_shape = ([jax.ShapeDtypeStruct((ROWS_P, n_cols), dt) for dt in out_dtypes]
                 + [jax.ShapeDtypeStruct((ROWS_S, n_cols), dt) for dt in out_dtypes])
    vmem = (2 * (_nbytes((BM, k_dim), xp.dtype) + _nbytes((ROWS_S, k_dim), xs.dtype))
            + 2 * _nbytes((k_dim, bn), BF16)
            + 2 * (n_ex + n_out) * _nbytes((BM + ROWS_S, bn), F32)
            + 3 * _nbytes((BM, bn), F32))
    for arr, n_chunks, width, col_blk in cast:
        assert n_chunks <= N_MP * n_tiles and arr.shape[0] % n_chunks == 0
        blk = (arr.shape[0] // n_chunks, width)
        step = lambda m, n, n_chunks=n_chunks: jnp.minimum(m * n_tiles + n, n_chunks - 1)
        in_specs.append(pl.BlockSpec(blk, lambda m, n, step=step, col_blk=col_blk: (step(m, n), col_blk)))
        out_specs.append(pl.BlockSpec(blk, lambda m, n, step=step: (step(m, n), 0)))
        out_shape.append(jax.ShapeDtypeStruct((arr.shape[0], width), BF16))
        vmem += 2 * (_nbytes(blk, F32) + _nbytes(blk, BF16)) + _nbytes(blk, F32)
    args = [xp, xs, wb] + [e[0] for e in extras] + [e[1] for e in extras] + [c[0] for c in cast]
    res = pl.pallas_call(
        functools.partial(_mms_kernel, n_ex=n_ex, n_out=n_out, n_cast=n_cast, epilogue=epilogue),
        grid=(N_MP, n_tiles),
        in_specs=in_specs,
        out_specs=out_specs,
        out_shape=out_shape,
        compiler_params=_params(("arbitrary", "arbitrary"), vmem),
        name=name,
    )(*args)
    return res[:n_out], res[n_out:2 * n_out], res[2 * n_out:]


def _gelu_exact(x):
    return 0.5 * x * (1.0 + lax.erf(x * math.sqrt(0.5)))


def _lambda_value(lv):
    a = jnp.sum(lv[0:1] * lv[1:2], axis=-1, keepdims=True)
    b = jnp.sum(lv[2:3] * lv[3:4], axis=-1, keepdims=True)
    return jnp.exp(a) - jnp.exp(b) + LAMBDA_INIT


def _head_norm(o, g):
    ms = jnp.mean(o * o, axis=-1, keepdims=True)
    return o * lax.rsqrt(ms + EPS) * g * (1.0 - LAMBDA_INIT)


def _attn_prompt_kernel(q_ref, k_ref, v_ref, lam_ref, g_ref, o_ref, kb, vb, *, bq):
    def cast_rows(i, carry):
        r = pl.multiple_of(i * 256, 256)
        kb[pl.ds(r, 256), :] = k_ref[pl.ds(r, 256), :].astype(BF16)
        vb[pl.ds(r, 256), :] = v_ref[pl.ds(r, 256), :].astype(BF16)
        return carry
    lax.fori_loop(0, SEQ // 256, cast_rows, 0)

    lam = _lambda_value(lam_ref[...])
    row = lax.broadcasted_iota(jnp.int32, (GROUP * bq, bq), 0) & (bq - 1)
    col = lax.broadcasted_iota(jnp.int32, (GROUP * bq, bq), 1)
    causal = col <= row
    for qi in range(SEQ // bq):
        keys = (qi + 1) * bq
        q = q_ref[qi * bq:(qi + 1) * bq, :]
        o_maps = []
        for c in range(2):
            qc = jnp.concatenate([q[:, (g * 2 + c) * HEAD_DIM:(g * 2 + c + 1) * HEAD_DIM] for g in range(GROUP)],
                                 axis=0)
            s = lax.dot_general(qc, kb[0:keys, c * HEAD_DIM:(c + 1) * HEAD_DIM], _NT, preferred_element_type=F32)
            diag = jnp.where(causal, s[:, keys - bq:], NEG)
            s = diag if qi == 0 else jnp.concatenate([s[:, :keys - bq], diag], axis=1)
            p = jnp.exp2(s - jnp.max(s, axis=-1, keepdims=True))
            l = jnp.sum(p, axis=-1, keepdims=True)
            o_maps.append(jnp.dot(p.astype(BF16), vb[0:keys, :], preferred_element_type=F32) / l)
        o = o_maps[0] - lam * o_maps[1]
        for g in range(GROUP):
            o_ref[qi * bq:(qi + 1) * bq, g * V_DIM:(g + 1) * V_DIM] = _head_norm(
                o[g * bq:(g + 1) * bq], g_ref[...]).astype(o_ref.dtype)


def _attn_prompt(q, k, v, lamv, g_subln):
    bq = 256
    qw = GROUP * 2 * HEAD_DIM
    vmem = (2 * (2 * _nbytes((SEQ, qw), BF16) + 2 * _nbytes((SEQ, V_DIM), F32))
            + 2 * _nbytes((SEQ, V_DIM), BF16) + 10 * _nbytes((GROUP * bq, SEQ), F32))
    return pl.pallas_call(
        functools.partial(_attn_prompt_kernel, bq=bq),
        grid=(BATCH, N_KV_HEADS),
        in_specs=[
            pl.BlockSpec((SEQ, qw), lambda b, h: (b, h)),
            pl.BlockSpec((SEQ, 2 * HEAD_DIM), lambda b, h: (b, h)),
            pl.BlockSpec((SEQ, V_DIM), lambda b, h: (b, h)),
            pl.BlockSpec((4, HEAD_DIM), lambda b, h: (0, 0)),
            pl.BlockSpec((1, V_DIM), lambda b, h: (0, 0)),
        ],
        out_specs=pl.BlockSpec((SEQ, GROUP * V_DIM), lambda b, h: (b, h)),
        out_shape=jax.ShapeDtypeStruct((ROWS_P, N_HEADS * V_DIM), BF16),
        scratch_shapes=[pltpu.VMEM((SEQ, 2 * HEAD_DIM), BF16), pltpu.VMEM((SEQ, V_DIM), BF16)],
        compiler_params=_params(("arbitrary", "arbitrary"), vmem),
        name="attn_prompt",
    )(q, k, v, lamv, g_subln)


PAGES_PER_STEP = 8
QROWS = 2 * GROUP * DEC_SEQ
PAIRS = N_KV_HEADS // 2
PROWS = 2 * QROWS


def _attn_sample_kernel(pt_ref, qbd_ref, knew_ref, vnew_ref, lam_ref, g_ref, *rest):
    del pt_ref
    kp_refs = rest[:PAGES_PER_STEP]
    vp_refs = rest[PAGES_PER_STEP:3 * PAGES_PER_STEP]
    o_ref, m_s, l_s, acc_s = rest[3 * PAGES_PER_STEP:]
    j = pl.program_id(1)

    @pl.when(j == 0)
    def _():
        m_s[...] = jnp.full(m_s.shape, NEG, F32)
        l_s[...] = jnp.zeros(l_s.shape, F32)
        acc_s[...] = jnp.zeros(acc_s.shape, F32)

    def scores(k_tiles):
        cols = []
        for kt in k_tiles:
            rows = [lax.dot_general(qbd_ref[0, hp], kt(hp), _NT, preferred_element_type=F32)
                    for hp in range(PAIRS)]
            cols.append(jnp.concatenate(rows, axis=0))
        return cols[0] if len(cols) == 1 else jnp.concatenate(cols, axis=1)

    def update(s, v_tiles, width):
        m_prev = m_s[...]
        m_new = jnp.maximum(m_prev, jnp.max(s, axis=-1, keepdims=True))
        alpha = jnp.exp2(m_prev - m_new)
        p = jnp.exp2(s - m_new)
        l_s[...] = alpha * l_s[...] + jnp.sum(p, axis=-1, keepdims=True)
        pb = p.astype(BF16)
        pv = []
        for hp in range(PAIRS):
            acc = None
            for i, vt in enumerate(v_tiles):
                d = jnp.dot(pb[hp * PROWS:(hp + 1) * PROWS, i * width:(i + 1) * width], vt(hp),
                            preferred_element_type=F32)
                acc = d if acc is None else acc + d
            pv.append(acc)
        acc_s[...] = alpha * acc_s[...] + jnp.concatenate(pv, axis=0)
        m_s[...] = m_new

    def k_page(ref):
        def get(hp):
            maps = [ref[pl.ds(2 * hp + c, 2 * PAGE_SIZE, stride=N_KV_HEADS), :] for c in range(2)]
            return jnp.concatenate(maps, axis=1).astype(BF16)
        return get

    def v_page(ref_lo, ref_hi):
        def get(hp):
            halves = [r[pl.ds(hp, 2 * PAGE_SIZE, stride=PAIRS), :] for r in (ref_lo, ref_hi)]
            return jnp.concatenate(halves, axis=1).astype(BF16)
        return get

    def which_head(shape):
        return (lax.broadcasted_iota(jnp.int32, shape, 0) >> int(math.log2(QROWS))) & 1

    s = scores([k_page(r) for r in kp_refs])
    col = lax.broadcasted_iota(jnp.int32, s.shape, 1)
    s = jnp.where((col & 1) == which_head(s.shape), s, NEG)
    update(s, [v_page(vp_refs[2 * p], vp_refs[2 * p + 1]) for p in range(PAGES_PER_STEP)], 2 * PAGE_SIZE)

    @pl.when(j == pl.num_programs(1) - 1)
    def _():
        pad = jnp.zeros((PAGE_SIZE - 2 * DEC_SEQ, 2 * HEAD_DIM), F32)

        def new_tile(ref):
            def get(hp):
                parts = [ref[:, h * V_DIM:(h + 1) * V_DIM] for h in (hp, hp + PAIRS)]
                return jnp.concatenate(parts + [pad], axis=0).astype(BF16)
            return get

        s = scores([new_tile(knew_ref)])
        t = lax.broadcasted_iota(jnp.int32, s.shape, 0) & (DEC_SEQ - 1)
        col = lax.broadcasted_iota(jnp.int32, s.shape, 1)
        ok = ((col >> int(math.log2(DEC_SEQ))) == which_head(s.shape)) & ((col & (DEC_SEQ - 1)) <= t)
        update(jnp.where(ok, s, NEG), [new_tile(vnew_ref)], PAGE_SIZE)

        lam = _lambda_value(lam_ref[...])
        o = acc_s[...] / l_s[...]
        half = GROUP * DEC_SEQ
        for hp in range(PAIRS):
            for hi in range(2):
                r0 = hp * PROWS + hi * QROWS
                d = o[r0:r0 + half] - lam * o[r0 + half:r0 + QROWS]
                y = _head_norm(d, g_ref[...])
                for g in range(GROUP):
                    c0 = ((hi * PAIRS + hp) * GROUP + g) * V_DIM
                    o_ref[:, c0:c0 + V_DIM] = y[g * DEC_SEQ:(g + 1) * DEC_SEQ]


def _attn_sample(q_s, k_s, v_s, cache_k, cache_v, page_table, lamv, g_subln):
    q6 = q_s.reshape(DEC_BATCH, DEC_SEQ, N_KV_HEADS, GROUP, 2, HEAD_DIM).transpose(0, 2, 4, 3, 1, 5)
    zero = jnp.zeros_like(q6[:, :, 0])
    top = jnp.concatenate([q6[:, :, 0], zero], axis=-1).reshape(DEC_BATCH, N_KV_HEADS, QROWS // 2, 2 * HEAD_DIM)
    bot = jnp.concatenate([zero, q6[:, :, 1]], axis=-1).reshape(DEC_BATCH, N_KV_HEADS, QROWS // 2, 2 * HEAD_DIM)
    qbd = jnp.concatenate([top, bot], axis=2)
    qbd = qbd.reshape(DEC_BATCH, 2, PAIRS, QROWS, 2 * HEAD_DIM).transpose(0, 2, 1, 3, 4).reshape(
        DEC_BATCH, PAIRS, PROWS, 2 * HEAD_DIM)
    n_pool = cache_k.shape[1]
    ck = cache_k.reshape(n_pool, PAGE_SIZE * N_KV_HEADS * 2, HEAD_DIM)
    cv = cache_v.reshape(n_pool, PAGE_SIZE * N_KV_HEADS, V_DIM)
    pps = PAGES_PER_STEP

    def k_spec(p):
        return pl.BlockSpec((None, ck.shape[1], HEAD_DIM), lambda b, j, pt: (pt[b, j * pps + p], 0, 0))

    def v_spec(p, half):
        return pl.BlockSpec((None, cv.shape[1], HEAD_DIM), lambda b, j, pt: (pt[b, j * pps + p], 0, half))

    in_specs = [
        pl.BlockSpec((1, PAIRS, PROWS, 2 * HEAD_DIM), lambda b, j, pt: (b, 0, 0, 0)),
        pl.BlockSpec((DEC_SEQ, K_W), lambda b, j, pt: (b, 0)),
        pl.BlockSpec((DEC_SEQ, V_W), lambda b, j, pt: (b, 0)),
        pl.BlockSpec((4, HEAD_DIM), lambda b, j, pt: (0, 0)),
        pl.BlockSpec((1, V_DIM), lambda b, j, pt: (0, 0)),
    ] + [k_spec(p) for p in range(pps)] + [v_spec(p, half) for p in range(pps) for half in range(2)]
    rows = N_KV_HEADS * QROWS
    vmem = (2 * 2 * pps * _nbytes((PAGE_SIZE, K_W), F32) + 2 * pps * _nbytes((PAGE_SIZE, K_W), BF16)
            + 6 * _nbytes((rows, pps * 2 * PAGE_SIZE), F32) + 4 * _nbytes((rows, V_DIM), F32)
            + 4 * _nbytes((DEC_SEQ, Q_W), F32))
    return pl.pallas_call(
        _attn_sample_kernel,
        grid_spec=pltpu.PrefetchScalarGridSpec(
            num_scalar_prefetch=1,
            grid=(DEC_BATCH, N_PAGES // pps),
            in_specs=in_specs,
            out_specs=pl.BlockSpec((DEC_SEQ, N_HEADS * V_DIM), lambda b, j, pt: (b, 0)),
            scratch_shapes=[pltpu.VMEM((rows, 1), F32), pltpu.VMEM((rows, 1), F32),
                            pltpu.VMEM((rows, V_DIM), F32)],
        ),
        out_shape=jax.ShapeDtypeStruct((ROWS_S, N_HEADS * V_DIM), F32),
        compiler_params=_params(("arbitrary", "arbitrary"), vmem),
        name="attn_sample",
    )(page_table, qbd, k_s, v_s, lamv, g_subln, *([ck] * pps), *([cv] * (2 * pps)))


def _sgu_kernel(u_ref, v_ref, w_ref, b_ref, g_ref, yb_ref, vn_ref, *, t, last_only):
    v = v_ref[...]
    ms = jnp.mean(v * v, axis=-1, keepdims=True)
    vn = v * lax.rsqrt(ms + EPS) * g_ref[...]
    if last_only:
        @pl.when(pl.program_id(1) == pl.num_programs(1) - 1)
        def _():
            vn_ref[...] = vn
    else:
        vn_ref[...] = vn
    r = v.shape[0]
    row = lax.broadcasted_iota(jnp.int32, (r, r), 0)
    col = lax.broadcasted_iota(jnp.int32, (r, r), 1)
    keep = (col <= row) & (col >= row - (row & (t - 1)))
    for g in range(N_SGU_GROUPS):
        cs = slice(g * SGU_GROUP_W, (g + 1) * SGU_GROUP_W)
        w = jnp.where(keep, w_ref[g], 0.0).astype(BF16)
        mixed = jnp.dot(w, vn[:, cs].astype(BF16), preferred_element_type=F32) + b_ref[g]
        yb_ref[:, cs] = (u_ref[:, cs] * mixed).astype(yb_ref.dtype)


def _sgu_prompt(uv, w_spatial, b_spatial, g_sgu):
    nch = SEQ // CHUNK
    vmem = 2 * (3 * _nbytes((CHUNK, SGU_W), F32) + _nbytes((CHUNK, SGU_W), BF16)) + 6 * _nbytes((CHUNK, SGU_W), F32)
    return pl.pallas_call(
        functools.partial(_sgu_kernel, t=CHUNK, last_only=True),
        grid=(BATCH, nch),
        in_specs=[
            pl.BlockSpec((CHUNK, SGU_W), lambda b, c: (b * nch + c, 0)),
            pl.BlockSpec((CHUNK, SGU_W), lambda b, c: (b * nch + c, 1)),
            pl.BlockSpec((N_SGU_GROUPS, CHUNK, CHUNK), lambda b, c: (0, 0, 0)),
            pl.BlockSpec((N_SGU_GROUPS, CHUNK, 1), lambda b, c: (0, 0, 0)),
            pl.BlockSpec((1, SGU_W), lambda b, c: (0, 0)),
        ],
        out_specs=[pl.BlockSpec((CHUNK, SGU_W), lambda b, c: (b * nch + c, 0)),
                   pl.BlockSpec((None, CHUNK, SGU_W), lambda b, c: (b, 0, 0))],
        out_shape=[jax.ShapeDtypeStruct((ROWS_P, SGU_W), BF16),
                   jax.ShapeDtypeStruct((BATCH, CHUNK, SGU_W), F32)],
        compiler_params=_params(("arbitrary", "arbitrary"), vmem),
        name="sgu_prompt",
    )(uv, uv, w_spatial, b_spatial[:, :CHUNK, None], g_sgu.reshape(1, SGU_W))


def _sgu_sample(uv, w_spatial, b_spatial, g_sgu):
    reps = ROWS_S // DEC_SEQ
    w = jnp.concatenate([w_spatial[:, :DEC_SEQ, :DEC_SEQ]] * reps, axis=2)
    w = jnp.broadcast_to(w[:, None], (N_SGU_GROUPS, reps, DEC_SEQ, ROWS_S)).reshape(N_SGU_GROUPS, ROWS_S, ROWS_S)
    b = jnp.tile(b_spatial[:, :DEC_SEQ], (1, reps))[:, :, None]
    vmem = 2 * (4 * _nbytes((ROWS_S, SGU_W), F32)) + 6 * _nbytes((ROWS_S, SGU_W), F32)
    return pl.pallas_call(
        functools.partial(_sgu_kernel, t=DEC_SEQ, last_only=False),
        grid=(1, 1),
        in_specs=[
            pl.BlockSpec((ROWS_S, SGU_W), lambda b, c: (0, 0)),
            pl.BlockSpec((ROWS_S, SGU_W), lambda b, c: (0, 1)),
            pl.BlockSpec((N_SGU_GROUPS, ROWS_S, ROWS_S), lambda b, c: (0, 0, 0)),
            pl.BlockSpec((N_SGU_GROUPS, ROWS_S, 1), lambda b, c: (0, 0, 0)),
            pl.BlockSpec((1, SGU_W), lambda b, c: (0, 0)),
        ],
        out_specs=[pl.BlockSpec((ROWS_S, SGU_W), lambda b, c: (0, 0)),
                   pl.BlockSpec((ROWS_S, SGU_W), lambda b, c: (0, 0))],
        out_shape=[jax.ShapeDtypeStruct((ROWS_S, SGU_W), BF16),
                   jax.ShapeDtypeStruct((ROWS_S, SGU_W), F32)],
        compiler_params=_params(("arbitrary", "arbitrary"), vmem),
        name="sgu_sample",
    )(uv, uv, w, b, g_sgu.reshape(1, SGU_W))


TAIL = 8


def _ffn_kernel(xp_ref, xs_ref, wg_ref, wu_ref, cw_ref, cb_ref, p1_ref, p2_ref,
                midp_ref, mids_ref, tail_ref, as_ref, carry):
    m, n = pl.program_id(0), pl.program_id(1)
    tiles_per_seq = SEQ // BM

    def conv_silu(a, sh1, sh2):
        cw = cw_ref[...]
        c = cb_ref[...] + ((cw[0:1] * sh2 + cw[1:2] * sh1) + cw[2:3] * a)
        return jax.nn.silu(c)

    @pl.when(m % tiles_per_seq == 0)
    def _():
        carry[n] = jnp.zeros(carry.shape[1:], F32)

    x = xp_ref[...]
    a = jnp.dot(x, wg_ref[...], preferred_element_type=F32)
    prev = carry[n]
    row = lax.broadcasted_iota(jnp.int32, a.shape, 0)
    sh1 = jnp.where(row == 0, prev[TAIL - 1:TAIL], pltpu.roll(a, 1, 0))
    sh2 = jnp.where(row == 0, prev[TAIL - 2:TAIL - 1],
                    jnp.where(row == 1, prev[TAIL - 1:TAIL], pltpu.roll(a, 2, 0)))
    gate = conv_silu(a, sh1, sh2)
    up = jnp.dot(x, wu_ref[...], preferred_element_type=F32)
    midp_ref[...] = (gate * up).astype(midp_ref.dtype)
    carry[n] = a[BM - TAIL:]

    @pl.when(m % tiles_per_seq == tiles_per_seq - 1)
    def _():
        tail_ref[...] = a[BM - TAIL:]

    @pl.when(m == N_MP - 1)
    def _():
        xs = xs_ref[...]
        a_s = jnp.dot(xs, wg_ref[...], preferred_element_type=F32)
        up_s = jnp.dot(xs, wu_ref[...], preferred_element_type=F32)
        t = lax.broadcasted_iota(jnp.int32, a_s.shape, 0) & (DEC_SEQ - 1)
        s1 = jnp.where(t == 0, p1_ref[...], pltpu.roll(a_s, 1, 0))
        s2 = jnp.where(t < 2, p2_ref[...], pltpu.roll(a_s, 2, 0))
        mids_ref[...] = (conv_silu(a_s, s1, s2) * up_s).astype(mids_ref.dtype)
        as_ref[...] = a_s


def _ffn_gate_up(hp, hs, wg, wu, conv_w, conv_b, state):
    bn = 512
    n_tiles = pl.cdiv(D_FF, bn)
    prev1 = jnp.pad(state[:, 1:2], ((0, 0), (0, DEC_SEQ - 1), (0, 0))).reshape(ROWS_S, D_FF)
    prev2 = jnp.pad(state, ((0, 0), (0, DEC_SEQ - 2), (0, 0))).reshape(ROWS_S, D_FF)
    tiles_per_seq = SEQ // BM
    last = lambda m, n: jnp.where(m == N_MP - 1, n, 0)
    seq_end = lambda m, n: jnp.where(m % tiles_per_seq == tiles_per_seq - 1, n, 0)
    vmem = (2 * _nbytes((BM + ROWS_S, D_MODEL), BF16) + 2 * 2 * _nbytes((D_MODEL, bn), BF16)
            + 2 * 4 * _nbytes((BM + ROWS_S, bn), F32) + 12 * _nbytes((BM, bn), F32)
            + _nbytes((n_tiles, TAIL, bn), F32))
    return pl.pallas_call(
        _ffn_kernel,
        grid=(N_MP, n_tiles),
        in_specs=[
            pl.BlockSpec((BM, D_MODEL), lambda m, n: (m, 0)),
            pl.BlockSpec((ROWS_S, D_MODEL), lambda m, n: (0, 0)),
            pl.BlockSpec((D_MODEL, bn), lambda m, n: (0, n)),
            pl.BlockSpec((D_MODEL, bn), lambda m, n: (0, n)),
            pl.BlockSpec((CONV_W, bn), lambda m, n: (0, n)),
            pl.BlockSpec((1, bn), lambda m, n: (0, n)),
            pl.BlockSpec((ROWS_S, bn), lambda m, n: (0, last(m, n))),
            pl.BlockSpec((ROWS_S, bn), lambda m, n: (0, last(m, n))),
        ],
        out_specs=[
            pl.BlockSpec((BM, bn), lambda m, n: (m, n)),
            pl.BlockSpec((ROWS_S, bn), lambda m, n: (0, last(m, n))),
            pl.BlockSpec((None, TAIL, bn), lambda m, n: (m // tiles_per_seq, 0, seq_end(m, n))),
            pl.BlockSpec((ROWS_S, bn), lambda m, n: (0, last(m, n))),
        ],
        out_shape=[
            jax.ShapeDtypeStruct((ROWS_P, D_FF), BF16),
            jax.ShapeDtypeStruct((ROWS_S, D_FF), BF16),
            jax.ShapeDtypeStruct((BATCH, TAIL, D_FF), F32),
            jax.ShapeDtypeStruct((ROWS_S, D_FF), F32),
        ],
        scratch_shapes=[pltpu.VMEM((n_tiles, TAIL, bn), F32)],
        compiler_params=_params(("arbitrary", "arbitrary"), vmem),
        name="ffn_gate_up",
    )(hp, hs, wg, wu, conv_w, conv_b.reshape(1, D_FF), prev1, prev2)


def _cast_kernel(x_ref, o_ref):
    o_ref[...] = x_ref[...].astype(o_ref.dtype)


def _cast_bf16(w, n, name):
    k = w.shape[0]
    rows = 256
    assert k % rows == 0
    vmem = 2 * (_nbytes((rows, n), F32) + _nbytes((rows, n), BF16)) + _nbytes((rows, n), F32)
    return pl.pallas_call(
        _cast_kernel,
        grid=(k // rows,),
        in_specs=[pl.BlockSpec((rows, n), lambda i: (i, 0))],
        out_specs=pl.BlockSpec((rows, n), lambda i: (i, 0)),
        out_shape=jax.ShapeDtypeStruct((k, n), BF16),
        compiler_params=_params(("arbitrary",), vmem),
        name=name,
    )(w)


def _down_kernel(x_ref, w_ref, r_ref, o_ref):
    o_ref[...] = r_ref[...] + jnp.dot(x_ref[...], w_ref[...], preferred_element_type=F32)


def _ffn_down(mid, wb, res, bm, name):
    rows = mid.shape[0]
    bn = 512
    vmem = (2 * (_nbytes((bm, D_FF), BF16) + _nbytes((D_FF, bn), BF16)) + 4 * _nbytes((bm, bn), F32)
            + 2 * _nbytes((bm, bn), F32))
    return pl.pallas_call(
        _down_kernel,
        grid=(rows // bm, D_MODEL // bn),
        in_specs=[
            pl.BlockSpec((bm, D_FF), lambda m, n: (m, 0)),
            pl.BlockSpec((D_FF, bn), lambda m, n: (0, n)),
            pl.BlockSpec((bm, bn), lambda m, n: (m, n)),
        ],
        out_specs=pl.BlockSpec((bm, bn), lambda m, n: (m, n)),
        out_shape=jax.ShapeDtypeStruct((rows, D_MODEL), F32),
        compiler_params=_params(("arbitrary", "arbitrary"), vmem),
        name=name,
    )(mid, wb, res)


def kernel(x_prompt, x_sample, cache_k, cache_v, state_conv, page_table, g_attn, w_in, lam_q1, lam_k1, lam_q2, lam_k2, g_subln, g_sgu, w_spatial, b_spatial, w_o_a, w_o_b, w_out, g_ffn, w_gate, conv_w, conv_b, w_up, w_down, g_final):
    xp = x_prompt.reshape(ROWS_P, D_MODEL)
    xs = x_sample.reshape(ROWS_S, D_MODEL)
    lamv = jnp.stack([lam_q1[0], lam_k1[0], lam_q2[0], lam_k2[0]])
    gsub = g_subln[0].reshape(1, V_DIM)

    hp = _rmsnorm_rows(xp, g_attn[0], BF16, "rms_attn_p")
    hs = _rmsnorm_rows(xs, g_attn[0], BF16, "rms_attn_s")

    bn = 512
    one = lambda f: (lambda accs, ex: [f(accs[0])])
    col = lambda c: c // bn
    whole = lambda w, n_chunks: (w, n_chunks, w.shape[1], 0)
    w_in0 = w_in[0]
    w_qb = _cast_bf16(w_in0, Q_W, "cast_w_q")
    (q_p,), (q_s,), (w_kvb, w_uvb) = _stream_matmul(
        hp, hs, w_qb, 0, Q_W, 256, one(lambda a: a * (SCALE * LOG2E)), [BF16],
        cast=[(w_in0, 128, K_W + V_W, 1), (w_in0, 128, 2 * SGU_W, 1)], name="proj_q")
    (k_p,), (k_s,), (wobb,) = _stream_matmul(hp, hs, w_kvb, 0, K_W, bn, one(lambda a: a), [F32],
                                             cast=[whole(w_o_b[0], 32)], name="proj_k")
    (v_p,), (v_s,), (woutb,) = _stream_matmul(hp, hs, w_kvb, col(K_W), V_W, bn, one(lambda a: a), [F32],
                                              cast=[whole(w_out[0], 32)], name="proj_v")
    (uv_p,), (uv_s,), (w_gtb, wgb) = _stream_matmul(
        hp, hs, w_uvb, 0, 2 * SGU_W, bn, one(_gelu_exact), [F32],
        cast=[(w_in0, 128, 2 * D_MODEL, 2), whole(w_gate[0], 128)], name="proj_uv")
    (gt_p,), (gt_s,), (wub, wdb, woab) = _stream_matmul(
        hp, hs, w_gtb, 0, 2 * D_MODEL, bn, one(jax.nn.sigmoid), [F32],
        cast=[whole(w_up[0], 128), whole(w_down[0], 86), whole(w_o_a[0], 128)], name="proj_gates")

    ya_p = _attn_prompt(q_p, k_p, v_p, lamv, gsub)
    ya_s = _attn_sample(q_s, k_s, v_s, cache_k, cache_v, page_table, lamv, gsub)

    yb_p, vn_p = _sgu_prompt(uv_p, w_spatial[0], b_spatial[0], g_sgu[0])
    yb_s, vn_s = _sgu_sample(uv_s, w_spatial[0], b_spatial[0], g_sgu[0])

    (ta_p,), (ta_s,), _ = _stream_matmul(ya_p, ya_s, woab, 0, D_MODEL, bn,
                                         lambda accs, ex: [ex[0] * accs[0]], [F32],
                                         extras=[(gt_p, gt_s)], ex_off=[0], name="merge_a")
    (z_p,), (z_s,), _ = _stream_matmul(yb_p, yb_s, wobb, 0, D_MODEL, bn,
                                       lambda accs, ex: [ex[1] + ex[0] * accs[0]], [BF16],
                                       extras=[(gt_p, gt_s), (ta_p, ta_s)], ex_off=[col(D_MODEL), 0], name="merge_b")
    (x1_p,), (x1_s,), _ = _stream_matmul(z_p, z_s, woutb, 0, D_MODEL, bn,
                                         lambda accs, ex: [ex[0] + accs[0]], [F32],
                                         extras=[(xp, xs)], ex_off=[0], name="out_proj")

    h2_p = _rmsnorm_rows(x1_p, g_ffn[0], BF16, "rms_ffn_p")
    h2_s = _rmsnorm_rows(x1_s, g_ffn[0], BF16, "rms_ffn_s")
    mid_p, mid_s, tail_p, a_s = _ffn_gate_up(h2_p, h2_s, wgb, wub, conv_w[0], conv_b[0], state_conv[0])
    x2_p = _ffn_down(mid_p, wdb, x1_p, 512, "ffn_down_p")
    x2_s = _ffn_down(mid_s, wdb, x1_s, ROWS_S, "ffn_down_s")

    y_p = _rmsnorm_rows(x2_p, g_final, F32, "rms_final_p")
    y_s = _rmsnorm_rows(x2_s, g_final, F32, "rms_final_s")

    return (
        y_p.reshape(BATCH, SEQ, D_MODEL),
        y_s.reshape(DEC_BATCH, DEC_SEQ, D_MODEL),
        k_p.reshape(1, BATCH, SEQ, N_KV_HEADS, 2, HEAD_DIM),
        v_p.reshape(1, BATCH, SEQ, N_KV_HEADS, V_DIM),
        k_s.reshape(1, DEC_BATCH, DEC_SEQ, N_KV_HEADS, 2, HEAD_DIM),
        v_s.reshape(1, DEC_BATCH, DEC_SEQ, N_KV_HEADS, V_DIM),
        vn_p.reshape(1, BATCH, CHUNK, SGU_W),
        vn_s.reshape(1, DEC_BATCH, DEC_SEQ, SGU_W),
        tail_p[:, TAIL - (CONV_W - 1):].reshape(1, BATCH, CONV_W - 1, D_FF),
        a_s.reshape(DEC_BATCH, DEC_SEQ, D_FF)[:, DEC_SEQ - (CONV_W - 1):].reshape(1, DEC_BATCH, CONV_W - 1, D_FF),
    )
```

```python
import functools
import math

import jax
import jax.numpy as jnp
from jax import lax
from jax.experimental import pallas as pl
from jax.experimental.pallas import tpu as pltpu

D_MODEL = 4096
BATCH = 4
SEQ = 2048
DEC_BATCH = 32
DEC_SEQ = 8
PAST_LEN = 8192
PAGE_SIZE = 128
N_HEADS = 16
N_KV_HEADS = 8
GROUP = N_HEADS // N_KV_HEADS
HEAD_DIM = D_MODEL // (2 * N_HEADS)
V_DIM = 2 * HEAD_DIM
Q_W = N_HEADS * 2 * HEAD_DIM
K_W = N_KV_HEADS * 2 * HEAD_DIM
V_W = N_KV_HEADS * V_DIM
SGU_W = D_MODEL
N_SGU_GROUPS = 8
SGU_GROUP_W = SGU_W // N_SGU_GROUPS
CHUNK = 128
D_FF = 11008
CONV_W = 3
SCALE = 1.0 / math.sqrt(HEAD_DIM)
LOG2E = math.log2(math.e)
NEG = -1e30
EPS = 1e-6
LAMBDA_INIT = 0.8 - 0.6 * math.exp(-0.3 * 0)

ROWS_P = BATCH * SEQ
ROWS_S = DEC_BATCH * DEC_SEQ
N_PAGES = PAST_LEN // PAGE_SIZE

V7X_VMEM_BYTES = 64 * 1024 * 1024
VMEM_CAP_BYTES = 58 * 1024 * 1024

BM = 1024
N_MP = ROWS_P // BM

F32 = jnp.float32
BF16 = jnp.bfloat16
_NT = (((1,), (1,)), ((), ()))


def _nbytes(shape, dtype):
    return math.prod(shape) * jnp.dtype(dtype).itemsize


def _params(sem, vmem_bytes):
    return pltpu.CompilerParams(dimension_semantics=sem,
                                vmem_limit_bytes=int(min(vmem_bytes, VMEM_CAP_BYTES)))


def _rms_kernel(x_ref, g_ref, o_ref):
    x = x_ref[...].astype(F32)
    ms = jnp.mean(x * x, axis=-1, keepdims=True)
    o_ref[...] = (x * lax.rsqrt(ms + EPS) * g_ref[...]).astype(o_ref.dtype)


def _rmsnorm_rows(x, g, out_dtype, name):
    n, d = x.shape
    rows = min(n, 256)
    vmem = 2 * (_nbytes((rows, d), x.dtype) + _nbytes((rows, d), out_dtype)) + 4 * _nbytes((rows, d), F32)
    return pl.pallas_call(
        _rms_kernel,
        grid=(n // rows,),
        in_specs=[pl.BlockSpec((rows, d), lambda i: (i, 0)), pl.BlockSpec((1, d), lambda i: (0, 0))],
        out_specs=pl.BlockSpec((rows, d), lambda i: (i, 0)),
        out_shape=jax.ShapeDtypeStruct((n, d), out_dtype),
        compiler_params=_params(("arbitrary",), vmem),
        name=name,
    )(x, g.reshape(1, d))


def _mms_kernel(*refs, n_ex, n_out, n_cast, epilogue):
    it = iter(refs)
    xp_ref, xs_ref, w_ref = next(it), next(it), next(it)
    exp_refs = [next(it) for _ in range(n_ex)]
    exs_refs = [next(it) for _ in range(n_ex)]
    cin_refs = [next(it) for _ in range(n_cast)]
    outp_refs = [next(it) for _ in range(n_out)]
    outs_refs = [next(it) for _ in range(n_out)]
    cout_refs = [next(it) for _ in range(n_cast)]

    for ci, co in zip(cin_refs, cout_refs):
        co[...] = ci[...].astype(BF16)

    def run(x_ref, ex_refs, out_refs):
        acc = jnp.dot(x_ref[...].astype(BF16), w_ref[...], preferred_element_type=F32)
        vals = epilogue([acc], [r[...] for r in ex_refs])
        for o_ref, v in zip(out_refs, vals):
            o_ref[...] = v.astype(o_ref.dtype)

    run(xp_ref, exp_refs, outp_refs)

    @pl.when(pl.program_id(0) == N_MP - 1)
    def _():
        run(xs_ref, exs_refs, outs_refs)


def _stream_matmul(xp, xs, wb, w_off, n_cols, bn, epilogue, out_dtypes, extras=(), ex_off=(), cast=(), name="mms"):
    k_dim = xp.shape[1]
    n_tiles = n_cols // bn
    n_ex, n_out, n_cast = len(extras), len(out_dtypes), len(cast)
    last = lambda m, n: jnp.where(m == N_MP - 1, n, 0)
    in_specs = [pl.BlockSpec((BM, k_dim), lambda m, n: (m, 0)),
                pl.BlockSpec((ROWS_S, k_dim), lambda m, n: (0, 0)),
                pl.BlockSpec((k_dim, bn), lambda m, n: (0, n + w_off))]
    for off in ex_off:
        in_specs.append(pl.BlockSpec((BM, bn), lambda m, n, off=off: (m, n + off)))
    for off in ex_off:
        in_specs.append(pl.BlockSpec((ROWS_S, bn), lambda m, n, off=off: (0, last(m, n) + off)))
    out_specs = ([pl.BlockSpec((BM, bn), lambda m, n: (m, n))] * n_out
                 + [pl.BlockSpec((ROWS_S, bn), lambda m, n: (0, last(m, n)))] * n_out)
    out_shape = ([jax.ShapeDtypeStruct((ROWS_P, n_cols), dt) for dt in out_dtypes]
                 + [jax.ShapeDtypeStruct((ROWS_S, n_cols), dt) for dt in out_dtypes])
    vmem = (2 * (_nbytes((BM, k_dim), xp.dtype) + _nbytes((ROWS_S, k_dim), xs.dtype))
            + 2 * _nbytes((k_dim, bn), BF16)
            + 2 * (n_ex + n_out) * _nbytes((BM + ROWS_S, bn), F32)
            + 3 * _nbytes((BM, bn), F32))
    for arr, n_chunks, width, col_blk in cast:
        assert n_chunks <= N_MP * n_tiles and arr.shape[0] % n_chunks == 0
        blk = (arr.shape[0] // n_chunks, width)
        step = lambda m, n, n_chunks=n_chunks: jnp.minimum(m * n_tiles + n, n_chunks - 1)
        in_specs.append(pl.BlockSpec(blk, lambda m, n, step=step, col_blk=col_blk: (step(m, n), col_blk)))
        out_specs.append(pl.BlockSpec(blk, lambda m, n, step=step: (step(m, n), 0)))
        out_shape.append(jax.ShapeDtypeStruct((arr.shape[0], width), BF16))
        vmem += 2 * (_nbytes(blk, F32) + _nbytes(blk, BF16)) + _nbytes(blk, F32)
    args = [xp, xs, wb] + [e[0] for e in extras] + [e[1] for e in extras] + [c[0] for c in cast]
    res = pl.pallas_call(
        functools.partial(_mms_kernel, n_ex=n_ex, n_out=n_out, n_cast=n_cast, epilogue=epilogue),
        grid=(N_MP, n_tiles),
        in_specs=in_specs,
        out_specs=out_specs,
        out_shape=out_shape,
        compiler_params=_params(("arbitrary", "arbitrary"), vmem),
        name=name,
    )(*args)
    return res[:n_out], res[n_out:2 * n_out], res[2 * n_out:]


def _gelu_exact(x):
    return 0.5 * x * (1.0 + lax.erf(x * math.sqrt(0.5)))


def _lambda_value(lv):
    a = jnp.sum(lv[0:1] * lv[1:2], axis=-1, keepdims=True)
    b = jnp.sum(lv[2:3] * lv[3:4], axis=-1, keepdims=True)
    return jnp.exp(a) - jnp.exp(b) + LAMBDA_INIT


def _head_norm(o, g):
    ms = jnp.mean(o * o, axis=-1, keepdims=True)
    return o * lax.rsqrt(ms + EPS) * g * (1.0 - LAMBDA_INIT)


def _attn_prompt_kernel(q_ref, k_ref, v_ref, lam_ref, g_ref, o_ref, kb, vb, *, bq):
    def cast_rows(i, carry):
        r = pl.multiple_of(i * 256, 256)
        kb[pl.ds(r, 256), :] = k_ref[pl.ds(r, 256), :].astype(BF16)
        vb[pl.ds(r, 256), :] = v_ref[pl.ds(r, 256), :].astype(BF16)
        return carry
    lax.fori_loop(0, SEQ // 256, cast_rows, 0)

    lam = _lambda_value(lam_ref[...])
    row = lax.broadcasted_iota(jnp.int32, (GROUP * bq, bq), 0) & (bq - 1)
    col = lax.broadcasted_iota(jnp.int32, (GROUP * bq, bq), 1)
    causal = col <= row
    for qi in range(SEQ // bq):
        keys = (qi + 1) * bq
        q = q_ref[qi * bq:(qi + 1) * bq, :]
        o_maps = []
        for c in range(2):
            qc = jnp.concatenate([q[:, (g * 2 + c) * HEAD_DIM:(g * 2 + c + 1) * HEAD_DIM] for g in range(GROUP)],
                                 axis=0)
            s = lax.dot_general(qc, kb[0:keys, c * HEAD_DIM:(c + 1) * HEAD_DIM], _NT, preferred_element_type=F32)
            diag = jnp.where(causal, s[:, keys - bq:], NEG)
            s = diag if qi == 0 else jnp.concatenate([s[:, :keys - bq], diag], axis=1)
            p = jnp.exp2(s - jnp.max(s, axis=-1, keepdims=True))
            l = jnp.sum(p, axis=-1, keepdims=True)
            o_maps.append(jnp.dot(p.astype(BF16), vb[0:keys, :], preferred_element_type=F32) / l)
        o = o_maps[0] - lam * o_maps[1]
        for g in range(GROUP):
            o_ref[qi * bq:(qi + 1) * bq, g * V_DIM:(g + 1) * V_DIM] = _head_norm(
                o[g * bq:(g + 1) * bq], g_ref[...]).astype(o_ref.dtype)


def _attn_prompt(q, k, v, lamv, g_subln):
    bq = 256
    qw = GROUP * 2 * HEAD_DIM
    vmem = (2 * (2 * _nbytes((SEQ, qw), BF16) + 2 * _nbytes((SEQ, V_DIM), F32))
            + 2 * _nbytes((SEQ, V_DIM), BF16) + 10 * _nbytes((GROUP * bq, SEQ), F32))
    return pl.pallas_call(
        functools.partial(_attn_prompt_kernel, bq=bq),
        grid=(BATCH, N_KV_HEADS),
        in_specs=[
            pl.BlockSpec((SEQ, qw), lambda b, h: (b, h)),
            pl.BlockSpec((SEQ, 2 * HEAD_DIM), lambda b, h: (b, h)),
            pl.BlockSpec((SEQ, V_DIM), lambda b, h: (b, h)),
            pl.BlockSpec((4, HEAD_DIM), lambda b, h: (0, 0)),
            pl.BlockSpec((1, V_DIM), lambda b, h: (0, 0)),
        ],
        out_specs=pl.BlockSpec((SEQ, GROUP * V_DIM), lambda b, h: (b, h)),
        out_shape=jax.ShapeDtypeStruct((ROWS_P, N_HEADS * V_DIM), BF16),
        scratch_shapes=[pltpu.VMEM((SEQ, 2 * HEAD_DIM), BF16), pltpu.VMEM((SEQ, V_DIM), BF16)],
        compiler_params=_params(("arbitrary", "arbitrary"), vmem),
        name="attn_prompt",
    )(q, k, v, lamv, g_subln)


PAGES_PER_STEP = 8
QROWS = 2 * GROUP * DEC_SEQ
PAIRS = N_KV_HEADS // 2
PROWS = 2 * QROWS


def _attn_sample_kernel(pt_ref, qbd_ref, knew_ref, vnew_ref, lam_ref, g_ref, *rest):
    del pt_ref
    kp_refs = rest[:PAGES_PER_STEP]
    vp_refs = rest[PAGES_PER_STEP:3 * PAGES_PER_STEP]
    o_ref, m_s, l_s, acc_s = rest[3 * PAGES_PER_STEP:]
    j = pl.program_id(1)

    @pl.when(j == 0)
    def _():
        m_s[...] = jnp.full(m_s.shape, NEG, F32)
        l_s[...] = jnp.zeros(l_s.shape, F32)
        acc_s[...] = jnp.zeros(acc_s.shape, F32)

    def scores(k_tiles):
        cols = []
        for kt in k_tiles:
            rows = [lax.dot_general(qbd_ref[0, hp], kt(hp), _NT, preferred_element_type=F32)
                    for hp in range(PAIRS)]
            cols.append(jnp.concatenate(rows, axis=0))
        return cols[0] if len(cols) == 1 else jnp.concatenate(cols, axis=1)

    def update(s, v_tiles, width):
        m_prev = m_s[...]
        m_new = jnp.maximum(m_prev, jnp.max(s, axis=-1, keepdims=True))
        alpha = jnp.exp2(m_prev - m_new)
        p = jnp.exp2(s - m_new)
        l_s[...] = alpha * l_s[...] + jnp.sum(p, axis=-1, keepdims=True)
        pb = p.astype(BF16)
        pv = []
        for hp in range(PAIRS):
            acc = None
            for i, vt in enumerate(v_tiles):
                d = jnp.dot(pb[hp * PROWS:(hp + 1) * PROWS, i * width:(i + 1) * width], vt(hp),
                            preferred_element_type=F32)
                acc = d if acc is None else acc + d
            pv.append(acc)
        acc_s[...] = alpha * acc_s[...] + jnp.concatenate(pv, axis=0)
        m_s[...] = m_new

    def k_page(ref):
        def get(hp):
            maps = [ref[pl.ds(2 * hp + c, 2 * PAGE_SIZE, stride=N_KV_HEADS), :] for c in range(2)]
            return jnp.concatenate(maps, axis=1).astype(BF16)
        return get

    def v_page(ref_lo, ref_hi):
        def get(hp):
            halves = [r[pl.ds(hp, 2 * PAGE_SIZE, stride=PAIRS), :] for r in (ref_lo, ref_hi)]
            return jnp.concatenate(halves, axis=1).astype(BF16)
        return get

    def which_head(shape):
        return (lax.broadcasted_iota(jnp.int32, shape, 0) >> int(math.log2(QROWS))) & 1

    s = scores([k_page(r) for r in kp_refs])
    col = lax.broadcasted_iota(jnp.int32, s.shape, 1)
    s = jnp.where((col & 1) == which_head(s.shape), s, NEG)
    update(s, [v_page(vp_refs[2 * p], vp_refs[2 * p + 1]) for p in range(PAGES_PER_STEP)], 2 * PAGE_SIZE)

    @pl.when(j == pl.num_programs(1) - 1)
    def _():
        pad = jnp.zeros((PAGE_SIZE - 2 * DEC_SEQ, 2 * HEAD_DIM), F32)

        def new_tile(ref):
            def get(hp):
                parts = [ref[:, h * V_DIM:(h + 1) * V_DIM] for h in (hp, hp + PAIRS)]
                return jnp.concatenate(parts + [pad], axis=0).astype(BF16)
            return get

        s = scores([new_tile(knew_ref)])
        t = lax.broadcasted_iota(jnp.int32, s.shape, 0) & (DEC_SEQ - 1)
        col = lax.broadcasted_iota(jnp.int32, s.shape, 1)
        ok = ((col >> int(math.log2(DEC_SEQ))) == which_head(s.shape)) & ((col & (DEC_SEQ - 1)) <= t)
        update(jnp.where(ok, s, NEG), [new_tile(vnew_ref)], PAGE_SIZE)

        lam = _lambda_value(lam_ref[...])
        o = acc_s[...] / l_s[...]
        half = GROUP * DEC_SEQ
        for hp in range(PAIRS):
            for hi in range(2):
                r0 = hp * PROWS + hi * QROWS
                d = o[r0:r0 + half] - lam * o[r0 + half:r0 + QROWS]
                y = _head_norm(d, g_ref[...])
                for g in range(GROUP):
                    c0 = ((hi * PAIRS + hp) * GROUP + g) * V_DIM
                    o_ref[:, c0:c0 + V_DIM] = y[g * DEC_SEQ:(g + 1) * DEC_SEQ]


def _attn_sample(q_s, k_s, v_s, cache_k, cache_v, page_table, lamv, g_subln):
    q6 = q_s.reshape(DEC_BATCH, DEC_SEQ, N_KV_HEADS, GROUP, 2, HEAD_DIM).transpose(0, 2, 4, 3, 1, 5)
    zero = jnp.zeros_like(q6[:, :, 0])
    top = jnp.concatenate([q6[:, :, 0], zero], axis=-1).reshape(DEC_BATCH, N_KV_HEADS, QROWS // 2, 2 * HEAD_DIM)
    bot = jnp.concatenate([zero, q6[:, :, 1]], axis=-1).reshape(DEC_BATCH, N_KV_HEADS, QROWS // 2, 2 * HEAD_DIM)
    qbd = jnp.concatenate([top, bot], axis=2)
    qbd = qbd.reshape(DEC_BATCH, 2, PAIRS, QROWS, 2 * HEAD_DIM).transpose(0, 2, 1, 3, 4).reshape(
        DEC_BATCH, PAIRS, PROWS, 2 * HEAD_DIM)
    n_pool = cache_k.shape[1]
    ck = cache_k.reshape(n_pool, PAGE_SIZE * N_KV_HEADS * 2, HEAD_DIM)
    cv = cache_v.reshape(n_pool, PAGE_SIZE * N_KV_HEADS, V_DIM)
    pps = PAGES_PER_STEP

    def k_spec(p):
        return pl.BlockSpec((None, ck.shape[1], HEAD_DIM), lambda b, j, pt: (pt[b, j * pps + p], 0, 0))

    def v_spec(p, half):
        return pl.BlockSpec((None, cv.shape[1], HEAD_DIM), lambda b, j, pt: (pt[b, j * pps + p], 0, half))

    in_specs = [
        pl.BlockSpec((1, PAIRS, PROWS, 2 * HEAD_DIM), lambda b, j, pt: (b, 0, 0, 0)),
        pl.BlockSpec((DEC_SEQ, K_W), lambda b, j, pt: (b, 0)),
        pl.BlockSpec((DEC_SEQ, V_W), lambda b, j, pt: (b, 0)),
        pl.BlockSpec((4, HEAD_DIM), lambda b, j, pt: (0, 0)),
        pl.BlockSpec((1, V_DIM), lambda b, j, pt: (0, 0)),
    ] + [k_spec(p) for p in range(pps)] + [v_spec(p, half) for p in range(pps) for half in range(2)]
    rows = N_KV_HEADS * QROWS
    vmem = (2 * 2 * pps * _nbytes((PAGE_SIZE, K_W), F32) + 2 * pps * _nbytes((PAGE_SIZE, K_W), BF16)
            + 6 * _nbytes((rows, pps * 2 * PAGE_SIZE), F32) + 4 * _nbytes((rows, V_DIM), F32)
            + 4 * _nbytes((DEC_SEQ, Q_W), F32))
    return pl.pallas_call(
        _attn_sample_kernel,
        grid_spec=pltpu.PrefetchScalarGridSpec(
            num_scalar_prefetch=1,
            grid=(DEC_BATCH, N_PAGES // pps),
            in_specs=in_specs,
            out_specs=pl.BlockSpec((DEC_SEQ, N_HEADS * V_DIM), lambda b, j, pt: (b, 0)),
            scratch_shapes=[pltpu.VMEM((rows, 1), F32), pltpu.VMEM((rows, 1), F32),
                            pltpu.VMEM((rows, V_DIM), F32)],
        ),
        out_shape=jax.ShapeDtypeStruct((ROWS_S, N_HEADS * V_DIM), F32),
        compiler_params=_params(("arbitrary", "arbitrary"), vmem),
        name="attn_sample",
    )(page_table, qbd, k_s, v_s, lamv, g_subln, *([ck] * pps), *([cv] * (2 * pps)))


def _sgu_kernel(u_ref, v_ref, w_ref, b_ref, g_ref, yb_ref, vn_ref, *, t, last_only):
    v = v_ref[...]
    ms = jnp.mean(v * v, axis=-1, keepdims=True)
    vn = v * lax.rsqrt(ms + EPS) * g_ref[...]
    if last_only:
        @pl.when(pl.program_id(1) == pl.num_programs(1) - 1)
        def _():
            vn_ref[...] = vn
    else:
        vn_ref[...] = vn
    r = v.shape[0]
    row = lax.broadcasted_iota(jnp.int32, (r, r), 0)
    col = lax.broadcasted_iota(jnp.int32, (r, r), 1)
    keep = (col <= row) & (col >= row - (row & (t - 1)))
    for g in range(N_SGU_GROUPS):
        cs = slice(g * SGU_GROUP_W, (g + 1) * SGU_GROUP_W)
        w = jnp.where(keep, w_ref[g], 0.0).astype(BF16)
        mixed = jnp.dot(w, vn[:, cs].astype(BF16), preferred_element_type=F32) + b_ref[g]
        yb_ref[:, cs] = (u_ref[:, cs] * mixed).astype(yb_ref.dtype)


def _sgu_prompt(uv, w_spatial, b_spatial, g_sgu):
    nch = SEQ // CHUNK
    vmem = 2 * (3 * _nbytes((CHUNK, SGU_W), F32) + _nbytes((CHUNK, SGU_W), BF16)) + 6 * _nbytes((CHUNK, SGU_W), F32)
    return pl.pallas_call(
        functools.partial(_sgu_kernel, t=CHUNK, last_only=True),
        grid=(BATCH, nch),
        in_specs=[
            pl.BlockSpec((CHUNK, SGU_W), lambda b, c: (b * nch + c, 0)),
            pl.BlockSpec((CHUNK, SGU_W), lambda b, c: (b * nch + c, 1)),
            pl.BlockSpec((N_SGU_GROUPS, CHUNK, CHUNK), lambda b, c: (0, 0, 0)),
            pl.BlockSpec((N_SGU_GROUPS, CHUNK, 1), lambda b, c: (0, 0, 0)),
            pl.BlockSpec((1, SGU_W), lambda b, c: (0, 0)),
        ],
        out_specs=[pl.BlockSpec((CHUNK, SGU_W), lambda b, c: (b * nch + c, 0)),
                   pl.BlockSpec((None, CHUNK, SGU_W), lambda b, c: (b, 0, 0))],
        out_shape=[jax.ShapeDtypeStruct((ROWS_P, SGU_W), BF16),
                   jax.ShapeDtypeStruct((BATCH, CHUNK, SGU_W), F32)],
        compiler_params=_params(("arbitrary", "arbitrary"), vmem),
        name="sgu_prompt",
    )(uv, uv, w_spatial, b_spatial[:, :CHUNK, None], g_sgu.reshape(1, SGU_W))


def _sgu_sample(uv, w_spatial, b_spatial, g_sgu):
    reps = ROWS_S // DEC_SEQ
    w = jnp.concatenate([w_spatial[:, :DEC_SEQ, :DEC_SEQ]] * reps, axis=2)
    w = jnp.broadcast_to(w[:, None], (N_SGU_GROUPS, reps, DEC_SEQ, ROWS_S)).reshape(N_SGU_GROUPS, ROWS_S, ROWS_S)
    b = jnp.tile(b_spatial[:, :DEC_SEQ], (1, reps))[:, :, None]
    vmem = 2 * (4 * _nbytes((ROWS_S, SGU_W), F32)) + 6 * _nbytes((ROWS_S, SGU_W), F32)
    return pl.pallas_call(
        functools.partial(_sgu_kernel, t=DEC_SEQ, last_only=False),
        grid=(1, 1),
        in_specs=[
            pl.BlockSpec((ROWS_S, SGU_W), lambda b, c: (0, 0)),
            pl.BlockSpec((ROWS_S, SGU_W), lambda b, c: (0, 1)),
            pl.BlockSpec((N_SGU_GROUPS, ROWS_S, ROWS_S), lambda b, c: (0, 0, 0)),
            pl.BlockSpec((N_SGU_GROUPS, ROWS_S, 1), lambda b, c: (0, 0, 0)),
            pl.BlockSpec((1, SGU_W), lambda b, c: (0, 0)),
        ],
        out_specs=[pl.BlockSpec((ROWS_S, SGU_W), lambda b, c: (0, 0)),
                   pl.BlockSpec((ROWS_S, SGU_W), lambda b, c: (0, 0))],
        out_shape=[jax.ShapeDtypeStruct((ROWS_S, SGU_W), BF16),
                   jax.ShapeDtypeStruct((ROWS_S, SGU_W), F32)],
        compiler_params=_params(("arbitrary", "arbitrary"), vmem),
        name="sgu_sample",
    )(uv, uv, w, b, g_sgu.reshape(1, SGU_W))


TAIL = 8


def _ffn_kernel(xp_ref, xs_ref, wg_ref, wu_ref, cw_ref, cb_ref, p1_ref, p2_ref,
                midp_ref, mids_ref, tail_ref, as_ref, carry):
    m, n = pl.program_id(0), pl.program_id(1)
    tiles_per_seq = SEQ // BM

    def conv_silu(a, sh1, sh2):
        cw = cw_ref[...]
        c = cb_ref[...] + ((cw[0:1] * sh2 + cw[1:2] * sh1) + cw[2:3] * a)
        return jax.nn.silu(c)

    @pl.when(m % tiles_per_seq == 0)
    def _():
        carry[n] = jnp.zeros(carry.shape[1:], F32)

    x = xp_ref[...]
    a = jnp.dot(x, wg_ref[...], preferred_element_type=F32)
    prev = carry[n]
    row = lax.broadcasted_iota(jnp.int32, a.shape, 0)
    sh1 = jnp.where(row == 0, prev[TAIL - 1:TAIL], pltpu.roll(a, 1, 0))
    sh2 = jnp.where(row == 0, prev[TAIL - 2:TAIL - 1],
                    jnp.where(row == 1, prev[TAIL - 1:TAIL], pltpu.roll(a, 2, 0)))
    gate = conv_silu(a, sh1, sh2)
    up = jnp.dot(x, wu_ref[...], preferred_element_type=F32)
    midp_ref[...] = (gate * up).astype(midp_ref.dtype)
    carry[n] = a[BM - TAIL:]

    @pl.when(m % tiles_per_seq == tiles_per_seq - 1)
    def _():
        tail_ref[...] = a[BM - TAIL:]

    @pl.when(m == N_MP - 1)
    def _():
        xs = xs_ref[...]
        a_s = jnp.dot(xs, wg_ref[...], preferred_element_type=F32)
        up_s = jnp.dot(xs, wu_ref[...], preferred_element_type=F32)
        t = lax.broadcasted_iota(jnp.int32, a_s.shape, 0) & (DEC_SEQ - 1)
        s1 = jnp.where(t == 0, p1_ref[...], pltpu.roll(a_s, 1, 0))
        s2 = jnp.where(t < 2, p2_ref[...], pltpu.roll(a_s, 2, 0))
        mids_ref[...] = (conv_silu(a_s, s1, s2) * up_s).astype(mids_ref.dtype)
        as_ref[...] = a_s


def _ffn_gate_up(hp, hs, wg, wu, conv_w, conv_b, state):
    bn = 512
    n_tiles = pl.cdiv(D_FF, bn)
    prev1 = jnp.pad(state[:, 1:2], ((0, 0), (0, DEC_SEQ - 1), (0, 0))).reshape(ROWS_S, D_FF)
    prev2 = jnp.pad(state, ((0, 0), (0, DEC_SEQ - 2), (0, 0))).reshape(ROWS_S, D_FF)
    tiles_per_seq = SEQ // BM
    last = lambda m, n: jnp.where(m == N_MP - 1, n, 0)
    seq_end = lambda m, n: jnp.where(m % tiles_per_seq == tiles_per_seq - 1, n, 0)
    vmem = (2 * _nbytes((BM + ROWS_S, D_MODEL), BF16) + 2 * 2 * _nbytes((D_MODEL, bn), BF16)
            + 2 * 4 * _nbytes((BM + ROWS_S, bn), F32) + 12 * _nbytes((BM, bn), F32)
            + _nbytes((n_tiles, TAIL, bn), F32))
    return pl.pallas_call(
        _ffn_kernel,
        grid=(N_MP, n_tiles),
        in_specs=[
            pl.BlockSpec((BM, D_MODEL), lambda m, n: (m, 0)),
            pl.BlockSpec((ROWS_S, D_MODEL), lambda m, n: (0, 0)),
            pl.BlockSpec((D_MODEL, bn), lambda m, n: (0, n)),
            pl.BlockSpec((D_MODEL, bn), lambda m, n: (0, n)),
            pl.BlockSpec((CONV_W, bn), lambda m, n: (0, n)),
            pl.BlockSpec((1, bn), lambda m, n: (0, n)),
            pl.BlockSpec((ROWS_S, bn), lambda m, n: (0, last(m, n))),
            pl.BlockSpec((ROWS_S, bn), lambda m, n: (0, last(m, n))),
        ],
        out_specs=[
            pl.BlockSpec((BM, bn), lambda m, n: (m, n)),
            pl.BlockSpec((ROWS_S, bn), lambda m, n: (0, last(m, n))),
            pl.BlockSpec((None, TAIL, bn), lambda m, n: (m // tiles_per_seq, 0, seq_end(m, n))),
            pl.BlockSpec((ROWS_S, bn), lambda m, n: (0, last(m, n))),
        ],
        out_shape=[
            jax.ShapeDtypeStruct((ROWS_P, D_FF), BF16),
            jax.ShapeDtypeStruct((ROWS_S, D_FF), BF16),
            jax.ShapeDtypeStruct((BATCH, TAIL, D_FF), F32),
            jax.ShapeDtypeStruct((ROWS_S, D_FF), F32),
        ],
        scratch_shapes=[pltpu.VMEM((n_tiles, TAIL, bn), F32)],
        compiler_params=_params(("arbitrary", "arbitrary"), vmem),
        name="ffn_gate_up",
    )(hp, hs, wg, wu, conv_w, conv_b.reshape(1, D_FF), prev1, prev2)


def _cast_kernel(x_ref, o_ref):
    o_ref[...] = x_ref[...].astype(o_ref.dtype)


def _cast_bf16(w, n, name):
    k = w.shape[0]
    rows = 256
    assert k % rows == 0
    vmem = 2 * (_nbytes((rows, n), F32) + _nbytes((rows, n), BF16)) + _nbytes((rows, n), F32)
    return pl.pallas_call(
        _cast_kernel,
        grid=(k // rows,),
        in_specs=[pl.BlockSpec((rows, n), lambda i: (i, 0))],
        out_specs=pl.BlockSpec((rows, n), lambda i: (i, 0)),
        out_shape=jax.ShapeDtypeStruct((k, n), BF16),
        compiler_params=_params(("arbitrary",), vmem),
        name=name,
    )(w)


def _down_kernel(x_ref, w_ref, r_ref, o_ref):
    o_ref[...] = r_ref[...] + jnp.dot(x_ref[...], w_ref[...], preferred_element_type=F32)


def _ffn_down(mid, wb, res, bm, name):
    rows = mid.shape[0]
    bn = 512
    vmem = (2 * (_nbytes((bm, D_FF), BF16) + _nbytes((D_FF, bn), BF16)) + 4 * _nbytes((bm, bn), F32)
            + 2 * _nbytes((bm, bn), F32))
    return pl.pallas_call(
        _down_kernel,
        grid=(rows // bm, D_MODEL // bn),
        in_specs=[
            pl.BlockSpec((bm, D_FF), lambda m, n: (m, 0)),
            pl.BlockSpec((D_FF, bn), lambda m, n: (0, n)),
            pl.BlockSpec((bm, bn), lambda m, n: (m, n)),
        ],
        out_specs=pl.BlockSpec((bm, bn), lambda m, n: (m, n)),
        out_shape=jax.ShapeDtypeStruct((rows, D_MODEL), F32),
        compiler_params=_params(("arbitrary", "arbitrary"), vmem),
        name=name,
    )(mid, wb, res)


def kernel(x_prompt, x_sample, cache_k, cache_v, state_conv, page_table, g_attn, w_in, lam_q1, lam_k1, lam_q2, lam_k2, g_subln, g_sgu, w_spatial, b_spatial, w_o_a, w_o_b, w_out, g_ffn, w_gate, conv_w, conv_b, w_up, w_down, g_final):
    xp = x_prompt.reshape(ROWS_P, D_MODEL)
    xs = x_sample.reshape(ROWS_S, D_MODEL)
    lamv = jnp.stack([lam_q1[0], lam_k1[0], lam_q2[0], lam_k2[0]])
    gsub = g_subln[0].reshape(1, V_DIM)

    hp = _rmsnorm_rows(xp, g_attn[0], BF16, "rms_attn_p")
    hs = _rmsnorm_rows(xs, g_attn[0], BF16, "rms_attn_s")

    bn = 512
    one = lambda f: (lambda accs, ex: [f(accs[0])])
    col = lambda c: c // bn
    whole = lambda w, n_chunks: (w, n_chunks, w.shape[1], 0)
    w_in0 = w_in[0]
    w_qb = _cast_bf16(w_in0, Q_W, "cast_w_q")
    (q_p,), (q_s,), (w_kvb, w_uvb) = _stream_matmul(
        hp, hs, w_qb, 0, Q_W, bn, one(lambda a: a * (SCALE * LOG2E)), [BF16],
        cast=[(w_in0, 64, K_W + V_W, 1), (w_in0, 64, 2 * SGU_W, 1)], name="proj_q")
    (k_p,), (k_s,), (wobb,) = _stream_matmul(hp, hs, w_kvb, 0, K_W, bn, one(lambda a: a), [F32],
                                             cast=[whole(w_o_b[0], 32)], name="proj_k")
    (v_p,), (v_s,), (woutb,) = _stream_matmul(hp, hs, w_kvb, col(K_W), V_W, bn, one(lambda a: a), [F32],
                                              cast=[whole(w_out[0], 32)], name="proj_v")
    (uv_p,), (uv_s,), (w_gtb, wgb) = _stream_matmul(
        hp, hs, w_uvb, 0, 2 * SGU_W, bn, one(_gelu_exact), [F32],
        cast=[(w_in0, 128, 2 * D_MODEL, 2), whole(w_gate[0], 128)], name="proj_uv")
    (gt_p,), (gt_s,), (wub, wdb, woab) = _stream_matmul(
        hp, hs, w_gtb, 0, 2 * D_MODEL, bn, one(jax.nn.sigmoid), [F32],
        cast=[whole(w_up[0], 128), whole(w_down[0], 86), whole(w_o_a[0], 128)], name="proj_gates")

    ya_p = _attn_prompt(q_p, k_p, v_p, lamv, gsub)
    ya_s = _attn_sample(q_s, k_s, v_s, cache_k, cache_v, page_table, lamv, gsub)

    yb_p, vn_p = _sgu_prompt(uv_p, w_spatial[0], b_spatial[0], g_sgu[0])
    yb_s, vn_s = _sgu_sample(uv_s, w_spatial[0], b_spatial[0], g_sgu[0])

    (ta_p,), (ta_s,), _ = _stream_matmul(ya_p, ya_s, woab, 0, D_MODEL, bn,
                                         lambda accs, ex: [ex[0] * accs[0]], [F32],
                                         extras=[(gt_p, gt_s)], ex_off=[0], name="merge_a")
    (z_p,), (z_s,), _ = _stream_matmul(yb_p, yb_s, wobb, 0, D_MODEL, bn,
                                       lambda accs, ex: [ex[1] + ex[0] * accs[0]], [BF16],
                                       extras=[(gt_p, gt_s), (ta_p, ta_s)], ex_off=[col(D_MODEL), 0], name="merge_b")
    (x1_p,), (x1_s,), _ = _stream_matmul(z_p, z_s, woutb, 0, D_MODEL, bn,
                                         lambda accs, ex: [ex[0] + accs[0]], [F32],
                                         extras=[(xp, xs)], ex_off=[0], name="out_proj")

    h2_p = _rmsnorm_rows(x1_p, g_ffn[0], BF16, "rms_ffn_p")
    h2_s = _rmsnorm_rows(x1_s, g_ffn[0], BF16, "rms_ffn_s")
    mid_p, mid_s, tail_p, a_s = _ffn_gate_up(h2_p, h2_s, wgb, wub, conv_w[0], conv_b[0], state_conv[0])
    x2_p = _ffn_down(mid_p, wdb, x1_p, 512, "ffn_down_p")
    x2_s = _ffn_down(mid_s, wdb, x1_s, ROWS_S, "ffn_down_s")

    y_p = _rmsnorm_rows(x2_p, g_final, F32, "rms_final_p")
    y_s = _rmsnorm_rows(x2_s, g_final, F32, "rms_final_s")

    return (
        y_p.reshape(BATCH, SEQ, D_MODEL),
        y_s.reshape(DEC_BATCH, DEC_SEQ, D_MODEL),
        k_p.reshape(1, BATCH, SEQ, N_KV_HEADS, 2, HEAD_DIM),
        v_p.reshape(1, BATCH, SEQ, N_KV_HEADS, V_DIM),
        k_s.reshape(1, DEC_BATCH, DEC_SEQ, N_KV_HEADS, 2, HEAD_DIM),
        v_s.reshape(1, DEC_BATCH, DEC_SEQ, N_KV_HEADS, V_DIM),
        vn_p.reshape(1, BATCH, CHUNK, SGU_W),
        vn_s.reshape(1, DEC_BATCH, DEC_SEQ, SGU_W),
        tail_p[:, TAIL - (CONV_W - 1):].reshape(1, BATCH, CONV_W - 1, D_FF),
        a_s.reshape(DEC_BATCH, DEC_SEQ, D_FF)[:, DEC_SEQ - (CONV_W - 1):].reshape(1, DEC_BATCH, CONV_W - 1, D_FF),
    )
```

```python
import functools
import math

import jax
import jax.numpy as jnp
from jax import lax
from jax.experimental import pallas as pl
from jax.experimental.pallas import tpu as pltpu

D_MODEL = 4096
BATCH = 4
SEQ = 2048
DEC_BATCH = 32
DEC_SEQ = 8
PAST_LEN = 8192
PAGE_SIZE = 128
N_HEADS = 16
N_KV_HEADS = 8
GROUP = N_HEADS // N_KV_HEADS
HEAD_DIM = D_MODEL // (2 * N_HEADS)
V_DIM = 2 * HEAD_DIM
Q_W = N_HEADS * 2 * HEAD_DIM
K_W = N_KV_HEADS * 2 * HEAD_DIM
V_W = N_KV_HEADS * V_DIM
SGU_W = D_MODEL
N_SGU_GROUPS = 8
SGU_GROUP_W = SGU_W // N_SGU_GROUPS
CHUNK = 128
D_FF = 11008
CONV_W = 3
SCALE = 1.0 / math.sqrt(HEAD_DIM)
LOG2E = math.log2(math.e)
NEG = -1e30
EPS = 1e-6
LAMBDA_INIT = 0.8 - 0.6 * math.exp(-0.3 * 0)

ROWS_P = BATCH * SEQ
ROWS_S = DEC_BATCH * DEC_SEQ
N_PAGES = PAST_LEN // PAGE_SIZE

V7X_VMEM_BYTES = 64 * 1024 * 1024
VMEM_CAP_BYTES = V7X_VMEM_BYTES - 6 * 1024 * 1024

MXU_COLS = 256
BM = 1024
N_MP = ROWS_P // BM

F32 = jnp.float32
BF16 = jnp.bfloat16
_NT = (((1,), (1,)), ((), ()))


def _nbytes(shape, dtype):
    return math.prod(shape) * jnp.dtype(dtype).itemsize


def _params(sem, vmem_bytes):
    return pltpu.CompilerParams(dimension_semantics=sem,
                                vmem_limit_bytes=int(min(vmem_bytes, VMEM_CAP_BYTES)))


def _rms_kernel(x_ref, g_ref, o_ref):
    x = x_ref[...].astype(F32)
    ms = jnp.mean(x * x, axis=-1, keepdims=True)
    o_ref[...] = (x * lax.rsqrt(ms + EPS) * g_ref[...]).astype(o_ref.dtype)


def _rmsnorm_rows(x, g, out_dtype, name):
    n, d = x.shape
    rows = min(n, 512)
    vmem = 2 * (_nbytes((rows, d), x.dtype) + _nbytes((rows, d), out_dtype)) + 4 * _nbytes((rows, d), F32)
    return pl.pallas_call(
        _rms_kernel,
        grid=(n // rows,),
        in_specs=[pl.BlockSpec((rows, d), lambda i: (i, 0)), pl.BlockSpec((1, d), lambda i: (0, 0))],
        out_specs=pl.BlockSpec((rows, d), lambda i: (i, 0)),
        out_shape=jax.ShapeDtypeStruct((n, d), out_dtype),
        compiler_params=_params(("arbitrary",), vmem),
        name=name,
    )(x, g.reshape(1, d))


def _mms_kernel(*refs, n_ex, n_out, n_cast, epilogue):
    it = iter(refs)
    xp_ref, xs_ref, w_ref = next(it), next(it), next(it)
    exp_refs = [next(it) for _ in range(n_ex)]
    exs_refs = [next(it) for _ in range(n_ex)]
    cin_refs = [next(it) for _ in range(n_cast)]
    outp_refs = [next(it) for _ in range(n_out)]
    outs_refs = [next(it) for _ in range(n_out)]
    cout_refs = [next(it) for _ in range(n_cast)]

    for ci, co in zip(cin_refs, cout_refs):
        co[...] = ci[...].astype(BF16)

    def run(x_ref, ex_refs, out_refs):
        x = x_ref[...].astype(BF16)
        for c0 in range(0, w_ref.shape[1], MXU_COLS):
            cs = slice(c0, c0 + MXU_COLS)
            acc = jnp.dot(x, w_ref[:, cs], preferred_element_type=F32)
            vals = epilogue([acc], [r[:, cs] for r in ex_refs])
            for o_ref, v in zip(out_refs, vals):
                o_ref[:, cs] = v.astype(o_ref.dtype)

    run(xp_ref, exp_refs, outp_refs)

    @pl.when(pl.program_id(0) == N_MP - 1)
    def _():
        run(xs_ref, exs_refs, outs_refs)


def _stream_matmul(xp, xs, wb, w_off, n_cols, bn, epilogue, out_dtypes, extras=(), ex_off=(), cast=(), name="mms"):
    k_dim = xp.shape[1]
    n_tiles = n_cols // bn
    n_ex, n_out, n_cast = len(extras), len(out_dtypes), len(cast)
    last = lambda m, n: jnp.where(m == N_MP - 1, n, 0)
    in_specs = [pl.BlockSpec((BM, k_dim), lambda m, n: (m, 0)),
                pl.BlockSpec((ROWS_S, k_dim), lambda m, n: (0, 0)),
                pl.BlockSpec((k_dim, bn), lambda m, n: (0, n + w_off))]
    for off in ex_off:
        in_specs.append(pl.BlockSpec((BM, bn), lambda m, n, off=off: (m, n + off)))
    for off in ex_off:
        in_specs.append(pl.BlockSpec((ROWS_S, bn), lambda m, n, off=off: (0, last(m, n) + off)))
    out_specs = ([pl.BlockSpec((BM, bn), lambda m, n: (m, n))] * n_out
                 + [pl.BlockSpec((ROWS_S, bn), lambda m, n: (0, last(m, n)))] * n_out)
    out_shape = ([jax.ShapeDtypeStruct((ROWS_P, n_cols), dt) for dt in out_dtypes]
                 + [jax.ShapeDtypeStruct((ROWS_S, n_cols), dt) for dt in out_dtypes])
    vmem = (2 * (_nbytes((BM, k_dim), xp.dtype) + _nbytes((ROWS_S, k_dim), xs.dtype))
            + 2 * _nbytes((k_dim, bn), BF16)
            + 2 * (n_ex + n_out) * _nbytes((BM + ROWS_S, bn), F32)
            + 3 * _nbytes((BM, bn), F32))
    for arr, n_chunks, width, col_blk in cast:
        assert n_chunks <= N_MP * n_tiles and arr.shape[0] % n_chunks == 0
        blk = (arr.shape[0] // n_chunks, width)
        step = lambda m, n, n_chunks=n_chunks: jnp.minimum(m * n_tiles + n, n_chunks - 1)
        in_specs.append(pl.BlockSpec(blk, lambda m, n, step=step, col_blk=col_blk: (step(m, n), col_blk)))
        out_specs.append(pl.BlockSpec(blk, lambda m, n, step=step: (step(m, n), 0)))
        out_shape.append(jax.ShapeDtypeStruct((arr.shape[0], width), BF16))
        vmem += 2 * (_nbytes(blk, F32) + _nbytes(blk, BF16)) + _nbytes(blk, F32)
    args = [xp, xs, wb] + [e[0] for e in extras] + [e[1] for e in extras] + [c[0] for c in cast]
    res = pl.pallas_call(
        functools.partial(_mms_kernel, n_ex=n_ex, n_out=n_out, n_cast=n_cast, epilogue=epilogue),
        grid=(N_MP, n_tiles),
        in_specs=in_specs,
        out_specs=out_specs,
        out_shape=out_shape,
        compiler_params=_params(("arbitrary", "arbitrary"), vmem),
        name=name,
    )(*args)
    return res[:n_out], res[n_out:2 * n_out], res[2 * n_out:]


def _gelu_exact(x):
    return 0.5 * x * (1.0 + lax.erf(x * math.sqrt(0.5)))


def _lambda_value(lv):
    a = jnp.sum(lv[0:1] * lv[1:2], axis=-1, keepdims=True)
    b = jnp.sum(lv[2:3] * lv[3:4], axis=-1, keepdims=True)
    return jnp.exp(a) - jnp.exp(b) + LAMBDA_INIT


def _head_norm(o, g):
    ms = jnp.mean(o * o, axis=-1, keepdims=True)
    return o * lax.rsqrt(ms + EPS) * g * (1.0 - LAMBDA_INIT)


def _attn_prompt_kernel(q_ref, k_ref, v_ref, lam_ref, g_ref, o_ref, kb, vb, *, bq):
    def cast_rows(i, carry):
        r = pl.multiple_of(i * 256, 256)
        kb[pl.ds(r, 256), :] = k_ref[pl.ds(r, 256), :].astype(BF16)
        vb[pl.ds(r, 256), :] = v_ref[pl.ds(r, 256), :].astype(BF16)
        return carry
    lax.fori_loop(0, SEQ // 256, cast_rows, 0)

    lam = _lambda_value(lam_ref[...])
    row = lax.broadcasted_iota(jnp.int32, (GROUP * bq, bq), 0) & (bq - 1)
    col = lax.broadcasted_iota(jnp.int32, (GROUP * bq, bq), 1)
    causal = col <= row
    for qi in range(SEQ // bq):
        keys = (qi + 1) * bq
        q = q_ref[qi * bq:(qi + 1) * bq, :]
        o_maps = []
        for c in range(2):
            qc = jnp.concatenate([q[:, (g * 2 + c) * HEAD_DIM:(g * 2 + c + 1) * HEAD_DIM] for g in range(GROUP)],
                                 axis=0)
            s = lax.dot_general(qc, kb[0:keys, c * HEAD_DIM:(c + 1) * HEAD_DIM], _NT, preferred_element_type=F32)
            diag = jnp.where(causal, s[:, keys - bq:], NEG)
            s = diag if qi == 0 else jnp.concatenate([s[:, :keys - bq], diag], axis=1)
            p = jnp.exp2(s - jnp.max(s, axis=-1, keepdims=True))
            l = jnp.sum(p, axis=-1, keepdims=True)
            o_maps.append(jnp.dot(p.astype(BF16), vb[0:keys, :], preferred_element_type=F32) / l)
        o = o_maps[0] - lam * o_maps[1]
        for g in range(GROUP):
            o_ref[qi * bq:(qi + 1) * bq, g * V_DIM:(g + 1) * V_DIM] = _head_norm(
                o[g * bq:(g + 1) * bq], g_ref[...]).astype(o_ref.dtype)


def _attn_prompt(q, k, v, lamv, g_subln):
    bq = 256
    qw = GROUP * 2 * HEAD_DIM
    vmem = (2 * (2 * _nbytes((SEQ, qw), BF16) + 2 * _nbytes((SEQ, V_DIM), F32))
            + 2 * _nbytes((SEQ, V_DIM), BF16) + 10 * _nbytes((GROUP * bq, SEQ), F32))
    return pl.pallas_call(
        functools.partial(_attn_prompt_kernel, bq=bq),
        grid=(BATCH, N_KV_HEADS),
        in_specs=[
            pl.BlockSpec((SEQ, qw), lambda b, h: (b, h)),
            pl.BlockSpec((SEQ, 2 * HEAD_DIM), lambda b, h: (b, h)),
            pl.BlockSpec((SEQ, V_DIM), lambda b, h: (b, h)),
            pl.BlockSpec((4, HEAD_DIM), lambda b, h: (0, 0)),
            pl.BlockSpec((1, V_DIM), lambda b, h: (0, 0)),
        ],
        out_specs=pl.BlockSpec((SEQ, GROUP * V_DIM), lambda b, h: (b, h)),
        out_shape=jax.ShapeDtypeStruct((ROWS_P, N_HEADS * V_DIM), BF16),
        scratch_shapes=[pltpu.VMEM((SEQ, 2 * HEAD_DIM), BF16), pltpu.VMEM((SEQ, V_DIM), BF16)],
        compiler_params=_params(("arbitrary", "arbitrary"), vmem),
        name="attn_prompt",
    )(q, k, v, lamv, g_subln)


PAGES_PER_STEP = 8
QROWS = 2 * GROUP * DEC_SEQ
PAIRS = N_KV_HEADS // 2
PROWS = 2 * QROWS


def _attn_sample_kernel(pt_ref, qbd_ref, knew_ref, vnew_ref, lam_ref, g_ref, *rest):
    del pt_ref
    kp_refs = rest[:PAGES_PER_STEP]
    vp_refs = rest[PAGES_PER_STEP:3 * PAGES_PER_STEP]
    o_ref, m_s, l_s, acc_s = rest[3 * PAGES_PER_STEP:]
    j = pl.program_id(1)

    @pl.when(j == 0)
    def _():
        m_s[...] = jnp.full(m_s.shape, NEG, F32)
        l_s[...] = jnp.zeros(l_s.shape, F32)
        acc_s[...] = jnp.zeros(acc_s.shape, F32)

    def scores(k_tiles):
        cols = []
        for kt in k_tiles:
            rows = [lax.dot_general(qbd_ref[0, hp], kt(hp), _NT, preferred_element_type=F32)
                    for hp in range(PAIRS)]
            cols.append(jnp.concatenate(rows, axis=0))
        return cols[0] if len(cols) == 1 else jnp.concatenate(cols, axis=1)

    def update(s, v_tiles, width):
        m_prev = m_s[...]
        m_new = jnp.maximum(m_prev, jnp.max(s, axis=-1, keepdims=True))
        alpha = jnp.exp2(m_prev - m_new)
        p = jnp.exp2(s - m_new)
        l_s[...] = alpha * l_s[...] + jnp.sum(p, axis=-1, keepdims=True)
        pb = p.astype(BF16)
        pv = []
        for hp in range(PAIRS):
            acc = None
            for i, vt in enumerate(v_tiles):
                d = jnp.dot(pb[hp * PROWS:(hp + 1) * PROWS, i * width:(i + 1) * width], vt(hp),
                            preferred_element_type=F32)
                acc = d if acc is None else acc + d
            pv.append(acc)
        acc_s[...] = alpha * acc_s[...] + jnp.concatenate(pv, axis=0)
        m_s[...] = m_new

    def k_page(ref):
        def get(hp):
            maps = [ref[pl.ds(2 * hp + c, 2 * PAGE_SIZE, stride=N_KV_HEADS), :] for c in range(2)]
            return jnp.concatenate(maps, axis=1).astype(BF16)
        return get

    def v_page(ref_lo, ref_hi):
        def get(hp):
            halves = [r[pl.ds(hp, 2 * PAGE_SIZE, stride=PAIRS), :] for r in (ref_lo, ref_hi)]
            return jnp.concatenate(halves, axis=1).astype(BF16)
        return get

    def which_head(shape):
        return (lax.broadcasted_iota(jnp.int32, shape, 0) >> int(math.log2(QROWS))) & 1

    s = scores([k_page(r) for r in kp_refs])
    col = lax.broadcasted_iota(jnp.int32, s.shape, 1)
    s = jnp.where((col & 1) == which_head(s.shape), s, NEG)
    update(s, [v_page(vp_refs[2 * p], vp_refs[2 * p + 1]) for p in range(PAGES_PER_STEP)], 2 * PAGE_SIZE)

    @pl.when(j == pl.num_programs(1) - 1)
    def _():
        pad = jnp.zeros((PAGE_SIZE - 2 * DEC_SEQ, 2 * HEAD_DIM), F32)

        def new_tile(ref):
            def get(hp):
                parts = [ref[:, h * V_DIM:(h + 1) * V_DIM] for h in (hp, hp + PAIRS)]
                return jnp.concatenate(parts + [pad], axis=0).astype(BF16)
            return get

        s = scores([new_tile(knew_ref)])
        t = lax.broadcasted_iota(jnp.int32, s.shape, 0) & (DEC_SEQ - 1)
        col = lax.broadcasted_iota(jnp.int32, s.shape, 1)
        ok = ((col >> int(math.log2(DEC_SEQ))) == which_head(s.shape)) & ((col & (DEC_SEQ - 1)) <= t)
        update(jnp.where(ok, s, NEG), [new_tile(vnew_ref)], PAGE_SIZE)

        lam = _lambda_value(lam_ref[...])
        o = acc_s[...] / l_s[...]
        half = GROUP * DEC_SEQ
        for hp in range(PAIRS):
            for hi in range(2):
                r0 = hp * PROWS + hi * QROWS
                d = o[r0:r0 + half] - lam * o[r0 + half:r0 + QROWS]
                y = _head_norm(d, g_ref[...])
                for g in range(GROUP):
                    c0 = ((hi * PAIRS + hp) * GROUP + g) * V_DIM
                    o_ref[:, c0:c0 + V_DIM] = y[g * DEC_SEQ:(g + 1) * DEC_SEQ]


def _attn_sample(q_s, k_s, v_s, cache_k, cache_v, page_table, lamv, g_subln):
    q6 = q_s.reshape(DEC_BATCH, DEC_SEQ, N_KV_HEADS, GROUP, 2, HEAD_DIM).transpose(0, 2, 4, 3, 1, 5)
    zero = jnp.zeros_like(q6[:, :, 0])
    top = jnp.concatenate([q6[:, :, 0], zero], axis=-1).reshape(DEC_BATCH, N_KV_HEADS, QROWS // 2, 2 * HEAD_DIM)
    bot = jnp.concatenate([zero, q6[:, :, 1]], axis=-1).reshape(DEC_BATCH, N_KV_HEADS, QROWS // 2, 2 * HEAD_DIM)
    qbd = jnp.concatenate([top, bot], axis=2)
    qbd = qbd.reshape(DEC_BATCH, 2, PAIRS, QROWS, 2 * HEAD_DIM).transpose(0, 2, 1, 3, 4).reshape(
        DEC_BATCH, PAIRS, PROWS, 2 * HEAD_DIM)
    n_pool = cache_k.shape[1]
    ck = cache_k.reshape(n_pool, PAGE_SIZE * N_KV_HEADS * 2, HEAD_DIM)
    cv = cache_v.reshape(n_pool, PAGE_SIZE * N_KV_HEADS, V_DIM)
    pps = PAGES_PER_STEP

    def k_spec(p):
        return pl.BlockSpec((None, ck.shape[1], HEAD_DIM), lambda b, j, pt: (pt[b, j * pps + p], 0, 0))

    def v_spec(p, half):
        return pl.BlockSpec((None, cv.shape[1], HEAD_DIM), lambda b, j, pt: (pt[b, j * pps + p], 0, half))

    in_specs = [
        pl.BlockSpec((1, PAIRS, PROWS, 2 * HEAD_DIM), lambda b, j, pt: (b, 0, 0, 0)),
        pl.BlockSpec((DEC_SEQ, K_W), lambda b, j, pt: (b, 0)),
        pl.BlockSpec((DEC_SEQ, V_W), lambda b, j, pt: (b, 0)),
        pl.BlockSpec((4, HEAD_DIM), lambda b, j, pt: (0, 0)),
        pl.BlockSpec((1, V_DIM), lambda b, j, pt: (0, 0)),
    ] + [k_spec(p) for p in range(pps)] + [v_spec(p, half) for p in range(pps) for half in range(2)]
    rows = N_KV_HEADS * QROWS
    vmem = (2 * 2 * pps * _nbytes((PAGE_SIZE, K_W), F32) + 2 * pps * _nbytes((PAGE_SIZE, K_W), BF16)
            + 6 * _nbytes((rows, pps * 2 * PAGE_SIZE), F32) + 4 * _nbytes((rows, V_DIM), F32)
            + 4 * _nbytes((DEC_SEQ, Q_W), F32))
    return pl.pallas_call(
        _attn_sample_kernel,
        grid_spec=pltpu.PrefetchScalarGridSpec(
            num_scalar_prefetch=1,
            grid=(DEC_BATCH, N_PAGES // pps),
            in_specs=in_specs,
            out_specs=pl.BlockSpec((DEC_SEQ, N_HEADS * V_DIM), lambda b, j, pt: (b, 0)),
            scratch_shapes=[pltpu.VMEM((rows, 1), F32), pltpu.VMEM((rows, 1), F32),
                            pltpu.VMEM((rows, V_DIM), F32)],
        ),
        out_shape=jax.ShapeDtypeStruct((ROWS_S, N_HEADS * V_DIM), F32),
        compiler_params=_params(("arbitrary", "arbitrary"), vmem),
        name="attn_sample",
    )(page_table, qbd, k_s, v_s, lamv, g_subln, *([ck] * pps), *([cv] * (2 * pps)))


def _sgu_kernel(u_ref, v_ref, w_ref, b_ref, g_ref, yb_ref, vn_ref, *, t, last_only):
    v = v_ref[...]
    ms = jnp.mean(v * v, axis=-1, keepdims=True)
    vn = v * lax.rsqrt(ms + EPS) * g_ref[...]
    if last_only:
        @pl.when(pl.program_id(1) == pl.num_programs(1) - 1)
        def _():
            vn_ref[...] = vn
    else:
        vn_ref[...] = vn
    r = v.shape[0]
    row = lax.broadcasted_iota(jnp.int32, (r, r), 0)
    col = lax.broadcasted_iota(jnp.int32, (r, r), 1)
    keep = (col <= row) & (col >= row - (row & (t - 1)))
    for g in range(N_SGU_GROUPS):
        cs = slice(g * SGU_GROUP_W, (g + 1) * SGU_GROUP_W)
        w = jnp.where(keep, w_ref[g], 0.0).astype(BF16)
        mixed = jnp.dot(w, vn[:, cs].astype(BF16), preferred_element_type=F32) + b_ref[g]
        yb_ref[:, cs] = (u_ref[:, cs] * mixed).astype(yb_ref.dtype)


def _sgu_prompt(uv, w_spatial, b_spatial, g_sgu):
    nch = SEQ // CHUNK
    vmem = 2 * (3 * _nbytes((CHUNK, SGU_W), F32) + _nbytes((CHUNK, SGU_W), BF16)) + 6 * _nbytes((CHUNK, SGU_W), F32)
    return pl.pallas_call(
        functools.partial(_sgu_kernel, t=CHUNK, last_only=True),
        grid=(BATCH, nch),
        in_specs=[
            pl.BlockSpec((CHUNK, SGU_W), lambda b, c: (b * nch + c, 0)),
            pl.BlockSpec((CHUNK, SGU_W), lambda b, c: (b * nch + c, 1)),
            pl.BlockSpec((N_SGU_GROUPS, CHUNK, CHUNK), lambda b, c: (0, 0, 0)),
            pl.BlockSpec((N_SGU_GROUPS, CHUNK, 1), lambda b, c: (0, 0, 0)),
            pl.BlockSpec((1, SGU_W), lambda b, c: (0, 0)),
        ],
        out_specs=[pl.BlockSpec((CHUNK, SGU_W), lambda b, c: (b * nch + c, 0)),
                   pl.BlockSpec((None, CHUNK, SGU_W), lambda b, c: (b, 0, 0))],
        out_shape=[jax.ShapeDtypeStruct((ROWS_P, SGU_W), BF16),
                   jax.ShapeDtypeStruct((BATCH, CHUNK, SGU_W), F32)],
        compiler_params=_params(("arbitrary", "arbitrary"), vmem),
        name="sgu_prompt",
    )(uv, uv, w_spatial, b_spatial[:, :CHUNK, None], g_sgu.reshape(1, SGU_W))


def _sgu_sample(uv, w_spatial, b_spatial, g_sgu):
    reps = ROWS_S // DEC_SEQ
    w = jnp.concatenate([w_spatial[:, :DEC_SEQ, :DEC_SEQ]] * reps, axis=2)
    w = jnp.broadcast_to(w[:, None], (N_SGU_GROUPS, reps, DEC_SEQ, ROWS_S)).reshape(N_SGU_GROUPS, ROWS_S, ROWS_S)
    b = jnp.tile(b_spatial[:, :DEC_SEQ], (1, reps))[:, :, None]
    vmem = 2 * (4 * _nbytes((ROWS_S, SGU_W), F32)) + 6 * _nbytes((ROWS_S, SGU_W), F32)
    return pl.pallas_call(
        functools.partial(_sgu_kernel, t=DEC_SEQ, last_only=False),
        grid=(1, 1),
        in_specs=[
            pl.BlockSpec((ROWS_S, SGU_W), lambda b, c: (0, 0)),
            pl.BlockSpec((ROWS_S, SGU_W), lambda b, c: (0, 1)),
            pl.BlockSpec((N_SGU_GROUPS, ROWS_S, ROWS_S), lambda b, c: (0, 0, 0)),
            pl.BlockSpec((N_SGU_GROUPS, ROWS_S, 1), lambda b, c: (0, 0, 0)),
            pl.BlockSpec((1, SGU_W), lambda b, c: (0, 0)),
        ],
        out_specs=[pl.BlockSpec((ROWS_S, SGU_W), lambda b, c: (0, 0)),
                   pl.BlockSpec((ROWS_S, SGU_W), lambda b, c: (0, 0))],
        out_shape=[jax.ShapeDtypeStruct((ROWS_S, SGU_W), BF16),
                   jax.ShapeDtypeStruct((ROWS_S, SGU_W), F32)],
        compiler_params=_params(("arbitrary", "arbitrary"), vmem),
        name="sgu_sample",
    )(uv, uv, w, b, g_sgu.reshape(1, SGU_W))


TAIL = 8


def _ffn_kernel(xp_ref, xs_ref, wg_ref, wu_ref, cw_ref, cb_ref, p1_ref, p2_ref,
                midp_ref, mids_ref, tail_ref, as_ref, carry):
    m, n = pl.program_id(0), pl.program_id(1)
    tiles_per_seq = SEQ // BM

    def conv_silu(a, sh1, sh2):
        cw = cw_ref[...]
        c = cb_ref[...] + ((cw[0:1] * sh2 + cw[1:2] * sh1) + cw[2:3] * a)
        return jax.nn.silu(c)

    @pl.when(m % tiles_per_seq == 0)
    def _():
        carry[n] = jnp.zeros(carry.shape[1:], F32)

    x = xp_ref[...]
    a = jnp.dot(x, wg_ref[...], preferred_element_type=F32)
    prev = carry[n]
    row = lax.broadcasted_iota(jnp.int32, a.shape, 0)
    sh1 = jnp.where(row == 0, prev[TAIL - 1:TAIL], pltpu.roll(a, 1, 0))
    sh2 = jnp.where(row == 0, prev[TAIL - 2:TAIL - 1],
                    jnp.where(row == 1, prev[TAIL - 1:TAIL], pltpu.roll(a, 2, 0)))
    gate = conv_silu(a, sh1, sh2)
    up = jnp.dot(x, wu_ref[...], preferred_element_type=F32)
    midp_ref[...] = (gate * up).astype(midp_ref.dtype)
    carry[n] = a[BM - TAIL:]

    @pl.when(m % tiles_per_seq == tiles_per_seq - 1)
    def _():
        tail_ref[...] = a[BM - TAIL:]

    @pl.when(m == N_MP - 1)
    def _():
        xs = xs_ref[...]
        a_s = jnp.dot(xs, wg_ref[...], preferred_element_type=F32)
        up_s = jnp.dot(xs, wu_ref[...], preferred_element_type=F32)
        t = lax.broadcasted_iota(jnp.int32, a_s.shape, 0) & (DEC_SEQ - 1)
        s1 = jnp.where(t == 0, p1_ref[...], pltpu.roll(a_s, 1, 0))
        s2 = jnp.where(t < 2, p2_ref[...], pltpu.roll(a_s, 2, 0))
        mids_ref[...] = (conv_silu(a_s, s1, s2) * up_s).astype(mids_ref.dtype)
        as_ref[...] = a_s


def _ffn_gate_up(hp, hs, wg, wu, conv_w, conv_b, state):
    bn = 512
    n_tiles = pl.cdiv(D_FF, bn)
    prev1 = jnp.pad(state[:, 1:2], ((0, 0), (0, DEC_SEQ - 1), (0, 0))).reshape(ROWS_S, D_FF)
    prev2 = jnp.pad(state, ((0, 0), (0, DEC_SEQ - 2), (0, 0))).reshape(ROWS_S, D_FF)
    tiles_per_seq = SEQ // BM
    last = lambda m, n: jnp.where(m == N_MP - 1, n, 0)
    seq_end = lambda m, n: jnp.where(m % tiles_per_seq == tiles_per_seq - 1, n, 0)
    vmem = (2 * _nbytes((BM + ROWS_S, D_MODEL), BF16) + 2 * 2 * _nbytes((D_MODEL, bn), BF16)
            + 2 * 4 * _nbytes((BM + ROWS_S, bn), F32) + 12 * _nbytes((BM, bn), F32)
            + _nbytes((n_tiles, TAIL, bn), F32))
    return pl.pallas_call(
        _ffn_kernel,
        grid=(N_MP, n_tiles),
        in_specs=[
            pl.BlockSpec((BM, D_MODEL), lambda m, n: (m, 0)),
            pl.BlockSpec((ROWS_S, D_MODEL), lambda m, n: (0, 0)),
            pl.BlockSpec((D_MODEL, bn), lambda m, n: (0, n)),
            pl.BlockSpec((D_MODEL, bn), lambda m, n: (0, n)),
            pl.BlockSpec((CONV_W, bn), lambda m, n: (0, n)),
            pl.BlockSpec((1, bn), lambda m, n: (0, n)),
            pl.BlockSpec((ROWS_S, bn), lambda m, n: (0, last(m, n))),
            pl.BlockSpec((ROWS_S, bn), lambda m, n: (0, last(m, n))),
        ],
        out_specs=[
            pl.BlockSpec((BM, bn), lambda m, n: (m, n)),
            pl.BlockSpec((ROWS_S, bn), lambda m, n: (0, last(m, n))),
            pl.BlockSpec((None, TAIL, bn), lambda m, n: (m // tiles_per_seq, 0, seq_end(m, n))),
            pl.BlockSpec((ROWS_S, bn), lambda m, n: (0, last(m, n))),
        ],
        out_shape=[
            jax.ShapeDtypeStruct((ROWS_P, D_FF), BF16),
            jax.ShapeDtypeStruct((ROWS_S, D_FF), BF16),
            jax.ShapeDtypeStruct((BATCH, TAIL, D_FF), F32),
            jax.ShapeDtypeStruct((ROWS_S, D_FF), F32),
        ],
        scratch_shapes=[pltpu.VMEM((n_tiles, TAIL, bn), F32)],
        compiler_params=_params(("arbitrary", "arbitrary"), vmem),
        name="ffn_gate_up",
    )(hp, hs, wg, wu, conv_w, conv_b.reshape(1, D_FF), prev1, prev2)


def _cast_kernel(x_ref, o_ref):
    o_ref[...] = x_ref[...].astype(o_ref.dtype)


def _cast_bf16(w, n, name):
    k = w.shape[0]
    rows = 256
    assert k % rows == 0
    vmem = 2 * (_nbytes((rows, n), F32) + _nbytes((rows, n), BF16)) + _nbytes((rows, n), F32)
    return pl.pallas_call(
        _cast_kernel,
        grid=(k // rows,),
        in_specs=[pl.BlockSpec((rows, n), lambda i: (i, 0))],
        out_specs=pl.BlockSpec((rows, n), lambda i: (i, 0)),
        out_shape=jax.ShapeDtypeStruct((k, n), BF16),
        compiler_params=_params(("arbitrary",), vmem),
        name=name,
    )(w)


def _down_kernel(x_ref, w_ref, r_ref, o_ref):
    o_ref[...] = r_ref[...] + jnp.dot(x_ref[...], w_ref[...], preferred_element_type=F32)


def _ffn_down(mid, wb, res, bm, name):
    rows = mid.shape[0]
    bn = 512
    vmem = (2 * (_nbytes((bm, D_FF), BF16) + _nbytes((D_FF, bn), BF16)) + 4 * _nbytes((bm, bn), F32)
            + 2 * _nbytes((bm, bn), F32))
    return pl.pallas_call(
        _down_kernel,
        grid=(rows // bm, D_MODEL // bn),
        in_specs=[
            pl.BlockSpec((bm, D_FF), lambda m, n: (m, 0)),
            pl.BlockSpec((D_FF, bn), lambda m, n: (0, n)),
            pl.BlockSpec((bm, bn), lambda m, n: (m, n)),
        ],
        out_specs=pl.BlockSpec((bm, bn), lambda m, n: (m, n)),
        out_shape=jax.ShapeDtypeStruct((rows, D_MODEL), F32),
        compiler_params=_params(("arbitrary", "arbitrary"), vmem),
        name=name,
    )(mid, wb, res)


def kernel(x_prompt, x_sample, cache_k, cache_v, state_conv, page_table, g_attn, w_in, lam_q1, lam_k1, lam_q2, lam_k2, g_subln, g_sgu, w_spatial, b_spatial, w_o_a, w_o_b, w_out, g_ffn, w_gate, conv_w, conv_b, w_up, w_down, g_final):
    xp = x_prompt.reshape(ROWS_P, D_MODEL)
    xs = x_sample.reshape(ROWS_S, D_MODEL)
    lamv = jnp.stack([lam_q1[0], lam_k1[0], lam_q2[0], lam_k2[0]])
    gsub = g_subln[0].reshape(1, V_DIM)

    hp = _rmsnorm_rows(xp, g_attn[0], BF16, "rms_attn_p")
    hs = _rmsnorm_rows(xs, g_attn[0], BF16, "rms_attn_s")

    bn = 512
    one = lambda f: (lambda accs, ex: [f(accs[0])])
    col = lambda c: c // bn
    whole = lambda w, n_chunks: (w, n_chunks, w.shape[1], 0)
    w_in0 = w_in[0]
    w_qb = _cast_bf16(w_in0, Q_W, "cast_w_q")
    (q_p,), (q_s,), (w_kvb, w_uvb) = _stream_matmul(
        hp, hs, w_qb, 0, Q_W, bn, one(lambda a: a * (SCALE * LOG2E)), [BF16],
        cast=[(w_in0, 64, K_W + V_W, 1), (w_in0, 64, 2 * SGU_W, 1)], name="proj_q")
    (k_p,), (k_s,), (wobb,) = _stream_matmul(hp, hs, w_kvb, 0, K_W, bn, one(lambda a: a), [F32],
                                             cast=[whole(w_o_b[0], 32)], name="proj_k")
    (v_p,), (v_s,), (woutb,) = _stream_matmul(hp, hs, w_kvb, col(K_W), V_W, bn, one(lambda a: a), [F32],
                                              cast=[whole(w_out[0], 32)], name="proj_v")
    (uv_p,), (uv_s,), (w_gtb, wgb) = _stream_matmul(
        hp, hs, w_uvb, 0, 2 * SGU_W, bn, one(_gelu_exact), [F32],
        cast=[(w_in0, 128, 2 * D_MODEL, 2), whole(w_gate[0], 128)], name="proj_uv")
    (gt_p,), (gt_s,), (wub, wdb, woab) = _stream_matmul(
        hp, hs, w_gtb, 0, 2 * D_MODEL, bn, one(jax.nn.sigmoid), [F32],
        cast=[whole(w_up[0], 128), whole(w_down[0], 86), whole(w_o_a[0], 128)], name="proj_gates")

    ya_p = _attn_prompt(q_p, k_p, v_p, lamv, gsub)
    ya_s = _attn_sample(q_s, k_s, v_s, cache_k, cache_v, page_table, lamv, gsub)

    yb_p, vn_p = _sgu_prompt(uv_p, w_spatial[0], b_spatial[0], g_sgu[0])
    yb_s, vn_s = _sgu_sample(uv_s, w_spatial[0], b_spatial[0], g_sgu[0])

    (ta_p,), (ta_s,), _ = _stream_matmul(ya_p, ya_s, woab, 0, D_MODEL, bn,
                                         lambda accs, ex: [ex[0] * accs[0]], [F32],
                                         extras=[(gt_p, gt_s)], ex_off=[0], name="merge_a")
    (z_p,), (z_s,), _ = _stream_matmul(yb_p, yb_s, wobb, 0, D_MODEL, bn,
                                       lambda accs, ex: [ex[1] + ex[0] * accs[0]], [BF16],
                                       extras=[(gt_p, gt_s), (ta_p, ta_s)], ex_off=[col(D_MODEL), 0], name="merge_b")
    (x1_p,), (x1_s,), _ = _stream_matmul(z_p, z_s, woutb, 0, D_MODEL, bn,
                                         lambda accs, ex: [ex[0] + accs[0]], [F32],
                                         extras=[(xp, xs)], ex_off=[0], name="out_proj")

    h2_p = _rmsnorm_rows(x1_p, g_ffn[0], BF16, "rms_ffn_p")
    h2_s = _rmsnorm_rows(x1_s, g_ffn[0], BF16, "rms_ffn_s")
    mid_p, mid_s, tail_p, a_s = _ffn_gate_up(h2_p, h2_s, wgb, wub, conv_w[0], conv_b[0], state_conv[0])
    x2_p = _ffn_down(mid_p, wdb, x1_p, 512, "ffn_down_p")
    x2_s = _ffn_down(mid_s, wdb, x1_s, ROWS_S, "ffn_down_s")

    y_p = _rmsnorm_rows(x2_p, g_final, F32, "rms_final_p")
    y_s = _rmsnorm_rows(x2_s, g_final, F32, "rms_final_s")

    return (
        y_p.reshape(BATCH, SEQ, D_MODEL),
        y_s.reshape(DEC_BATCH, DEC_SEQ, D_MODEL),
        k_p.reshape(1, BATCH, SEQ, N_KV_HEADS, 2, HEAD_DIM),
        v_p.reshape(1, BATCH, SEQ, N_KV_HEADS, V_DIM),
        k_s.reshape(1, DEC_BATCH, DEC_SEQ, N_KV_HEADS, 2, HEAD_DIM),
        v_s.reshape(1, DEC_BATCH, DEC_SEQ, N_KV_HEADS, V_DIM),
        vn_p.reshape(1, BATCH, CHUNK, SGU_W),
        vn_s.reshape(1, DEC_BATCH, DEC_SEQ, SGU_W),
        tail_p[:, TAIL - (CONV_W - 1):].reshape(1, BATCH, CONV_W - 1, D_FF),
        a_s.reshape(DEC_BATCH, DEC_SEQ, D_FF)[:, DEC_SEQ - (CONV_W - 1):].reshape(1, DEC_BATCH, CONV_W - 1, D_FF),
    )
```

```python
import functools
import math

import jax
import jax.numpy as jnp
from jax import lax
from jax.experimental import pallas as pl
from jax.experimental.pallas import tpu as pltpu

D_MODEL = 4096
BATCH = 4
SEQ = 2048
DEC_BATCH = 32
DEC_SEQ = 8
PAST_LEN = 8192
PAGE_SIZE = 128
N_HEADS = 16
N_KV_HEADS = 8
GROUP = N_HEADS // N_KV_HEADS
HEAD_DIM = D_MODEL // (2 * N_HEADS)
V_DIM = 2 * HEAD_DIM
Q_W = N_HEADS * 2 * HEAD_DIM
K_W = N_KV_HEADS * 2 * HEAD_DIM
V_W = N_KV_HEADS * V_DIM
SGU_W = D_MODEL
N_SGU_GROUPS = 8
SGU_GROUP_W = SGU_W // N_SGU_GROUPS
CHUNK = 128
D_FF = 11008
CONV_W = 3
SCALE = 1.0 / math.sqrt(HEAD_DIM)
LOG2E = math.log2(math.e)
NEG = -1e30
EPS = 1e-6
LAMBDA_INIT = 0.8 - 0.6 * math.exp(-0.3 * 0)

ROWS_P = BATCH * SEQ
ROWS_S = DEC_BATCH * DEC_SEQ
N_PAGES = PAST_LEN // PAGE_SIZE

V7X_VMEM_BYTES = 64 * 1024 * 1024
VMEM_CAP_BYTES = V7X_VMEM_BYTES - 6 * 1024 * 1024

MXU_COLS = 256
BM = 1024
N_MP = ROWS_P // BM

F32 = jnp.float32
BF16 = jnp.bfloat16
_NT = (((1,), (1,)), ((), ()))


def _nbytes(shape, dtype):
    return math.prod(shape) * jnp.dtype(dtype).itemsize


def _params(sem, vmem_bytes):
    return pltpu.CompilerParams(dimension_semantics=sem,
                                vmem_limit_bytes=int(min(vmem_bytes, VMEM_CAP_BYTES)))


def _rms_kernel(x_ref, g_ref, o_ref):
    x = x_ref[...].astype(F32)
    ms = jnp.mean(x * x, axis=-1, keepdims=True)
    o_ref[...] = (x * lax.rsqrt(ms + EPS) * g_ref[...]).astype(o_ref.dtype)


def _rmsnorm_rows(x, g, out_dtype, name):
    n, d = x.shape
    rows = min(n, 512)
    vmem = 2 * (_nbytes((rows, d), x.dtype) + _nbytes((rows, d), out_dtype)) + 4 * _nbytes((rows, d), F32)
    return pl.pallas_call(
        _rms_kernel,
        grid=(n // rows,),
        in_specs=[pl.BlockSpec((rows, d), lambda i: (i, 0)), pl.BlockSpec((1, d), lambda i: (0, 0))],
        out_specs=pl.BlockSpec((rows, d), lambda i: (i, 0)),
        out_shape=jax.ShapeDtypeStruct((n, d), out_dtype),
        compiler_params=_params(("arbitrary",), vmem),
        name=name,
    )(x, g.reshape(1, d))


def _mms_kernel(*refs, n_ex, n_out, n_cast, epilogue):
    it = iter(refs)
    xp_ref, xs_ref, w_ref = next(it), next(it), next(it)
    exp_refs = [next(it) for _ in range(n_ex)]
    exs_refs = [next(it) for _ in range(n_ex)]
    cin_refs = [next(it) for _ in range(n_cast)]
    outp_refs = [next(it) for _ in range(n_out)]
    outs_refs = [next(it) for _ in range(n_out)]
    cout_refs = [next(it) for _ in range(n_cast)]

    for ci, co in zip(cin_refs, cout_refs):
        co[...] = ci[...].astype(BF16)

    def run(x_ref, ex_refs, out_refs):
        x = x_ref[...].astype(BF16)
        for c0 in range(0, w_ref.shape[1], MXU_COLS):
            cs = slice(c0, c0 + MXU_COLS)
            acc = jnp.dot(x, w_ref[:, cs], preferred_element_type=F32)
            vals = epilogue([acc], [r[:, cs] for r in ex_refs])
            for o_ref, v in zip(out_refs, vals):
                o_ref[:, cs] = v.astype(o_ref.dtype)

    run(xp_ref, exp_refs, outp_refs)

    @pl.when(pl.program_id(0) == N_MP - 1)
    def _():
        run(xs_ref, exs_refs, outs_refs)


def _stream_matmul(xp, xs, wb, w_off, n_cols, bn, epilogue, out_dtypes, extras=(), ex_off=(), cast=(), name="mms"):
    k_dim = xp.shape[1]
    n_tiles = n_cols // bn
    n_ex, n_out, n_cast = len(extras), len(out_dtypes), len(cast)
    last = lambda m, n: jnp.where(m == N_MP - 1, n, 0)
    in_specs = [pl.BlockSpec((BM, k_dim), lambda m, n: (m, 0)),
                pl.BlockSpec((ROWS_S, k_dim), lambda m, n: (0, 0)),
                pl.BlockSpec((k_dim, bn), lambda m, n: (0, n + w_off))]
    for off in ex_off:
        in_specs.append(pl.BlockSpec((BM, bn), lambda m, n, off=off: (m, n + off)))
    for off in ex_off:
        in_specs.append(pl.BlockSpec((ROWS_S, bn), lambda m, n, off=off: (0, last(m, n) + off)))
    out_specs = ([pl.BlockSpec((BM, bn), lambda m, n: (m, n))] * n_out
                 + [pl.BlockSpec((ROWS_S, bn), lambda m, n: (0, last(m, n)))] * n_out)
    out_shape = ([jax.ShapeDtypeStruct((ROWS_P, n_cols), dt) for dt in out_dtypes]
                 + [jax.ShapeDtypeStruct((ROWS_S, n_cols), dt) for dt in out_dtypes])
    vmem = (2 * (_nbytes((BM, k_dim), xp.dtype) + _nbytes((ROWS_S, k_dim), xs.dtype))
            + 2 * _nbytes((k_dim, bn), BF16)
            + 2 * (n_ex + n_out) * _nbytes((BM + ROWS_S, bn), F32)
            + 3 * _nbytes((BM, bn), F32))
    for arr, n_chunks, width, col_blk in cast:
        assert n_chunks <= N_MP * n_tiles and arr.shape[0] % n_chunks == 0
        blk = (arr.shape[0] // n_chunks, width)
        step = lambda m, n, n_chunks=n_chunks: jnp.minimum(m * n_tiles + n, n_chunks - 1)
        in_specs.append(pl.BlockSpec(blk, lambda m, n, step=step, col_blk=col_blk: (step(m, n), col_blk)))
        out_specs.append(pl.BlockSpec(blk, lambda m, n, step=step: (step(m, n), 0)))
        out_shape.append(jax.ShapeDtypeStruct((arr.shape[0], width), BF16))
        vmem += 2 * (_nbytes(blk, F32) + _nbytes(blk, BF16)) + _nbytes(blk, F32)
    args = [xp, xs, wb] + [e[0] for e in extras] + [e[1] for e in extras] + [c[0] for c in cast]
    res = pl.pallas_call(
        functools.partial(_mms_kernel, n_ex=n_ex, n_out=n_out, n_cast=n_cast, epilogue=epilogue),
        grid=(N_MP, n_tiles),
        in_specs=in_specs,
        out_specs=out_specs,
        out_shape=out_shape,
        compiler_params=_params(("arbitrary", "arbitrary"), vmem),
        name=name,
    )(*args)
    return res[:n_out], res[n_out:2 * n_out], res[2 * n_out:]


def _gelu_exact(x):
    return 0.5 * x * (1.0 + lax.erf(x * math.sqrt(0.5)))


def _lambda_value(lv):
    a = jnp.sum(lv[0:1] * lv[1:2], axis=-1, keepdims=True)
    b = jnp.sum(lv[2:3] * lv[3:4], axis=-1, keepdims=True)
    return jnp.exp(a) - jnp.exp(b) + LAMBDA_INIT


def _head_norm(o, g):
    ms = jnp.mean(o * o, axis=-1, keepdims=True)
    return o * lax.rsqrt(ms + EPS) * g * (1.0 - LAMBDA_INIT)


def _attn_prompt_kernel(q_ref, k_ref, v_ref, lam_ref, g_ref, o_ref, kb, vb, *, bq):
    def cast_rows(i, carry):
        r = pl.multiple_of(i * 256, 256)
        kb[pl.ds(r, 256), :] = k_ref[pl.ds(r, 256), :].astype(BF16)
        vb[pl.ds(r, 256), :] = v_ref[pl.ds(r, 256), :].astype(BF16)
        return carry
    lax.fori_loop(0, SEQ // 256, cast_rows, 0)

    lam = _lambda_value(lam_ref[...])
    row = lax.broadcasted_iota(jnp.int32, (GROUP * bq, bq), 0) & (bq - 1)
    col = lax.broadcasted_iota(jnp.int32, (GROUP * bq, bq), 1)
    causal = col <= row
    for qi in range(SEQ // bq):
        keys = (qi + 1) * bq
        q = q_ref[qi * bq:(qi + 1) * bq, :]
        o_maps = []
        for c in range(2):
            qc = jnp.concatenate([q[:, (g * 2 + c) * HEAD_DIM:(g * 2 + c + 1) * HEAD_DIM] for g in range(GROUP)],
                                 axis=0)
            s = lax.dot_general(qc, kb[0:keys, c * HEAD_DIM:(c + 1) * HEAD_DIM], _NT, preferred_element_type=F32)
            diag = jnp.where(causal, s[:, keys - bq:], NEG)
            s = diag if qi == 0 else jnp.concatenate([s[:, :keys - bq], diag], axis=1)
            p = jnp.exp2(s - jnp.max(s, axis=-1, keepdims=True))
            l = jnp.sum(p, axis=-1, keepdims=True)
            o_maps.append(jnp.dot(p.astype(BF16), vb[0:keys, :], preferred_element_type=F32) / l)
        o = o_maps[0] - lam * o_maps[1]
        for g in range(GROUP):
            o_ref[qi * bq:(qi + 1) * bq, g * V_DIM:(g + 1) * V_DIM] = _head_norm(
                o[g * bq:(g + 1) * bq], g_ref[...]).astype(o_ref.dtype)


def _attn_prompt(q, k, v, lamv, g_subln):
    bq = 256
    qw = GROUP * 2 * HEAD_DIM
    vmem = (2 * (2 * _nbytes((SEQ, qw), BF16) + 2 * _nbytes((SEQ, V_DIM), F32))
            + 2 * _nbytes((SEQ, V_DIM), BF16) + 10 * _nbytes((GROUP * bq, SEQ), F32))
    return pl.pallas_call(
        functools.partial(_attn_prompt_kernel, bq=bq),
        grid=(BATCH, N_KV_HEADS),
        in_specs=[
            pl.BlockSpec((SEQ, qw), lambda b, h: (b, h)),
            pl.BlockSpec((SEQ, 2 * HEAD_DIM), lambda b, h: (b, h)),
            pl.BlockSpec((SEQ, V_DIM), lambda b, h: (b, h)),
            pl.BlockSpec((4, HEAD_DIM), lambda b, h: (0, 0)),
            pl.BlockSpec((1, V_DIM), lambda b, h: (0, 0)),
        ],
        out_specs=pl.BlockSpec((SEQ, GROUP * V_DIM), lambda b, h: (b, h)),
        out_shape=jax.ShapeDtypeStruct((ROWS_P, N_HEADS * V_DIM), BF16),
        scratch_shapes=[pltpu.VMEM((SEQ, 2 * HEAD_DIM), BF16), pltpu.VMEM((SEQ, V_DIM), BF16)],
        compiler_params=_params(("arbitrary", "arbitrary"), vmem),
        name="attn_prompt",
    )(q, k, v, lamv, g_subln)


PAGES_PER_STEP = 8
QROWS = 2 * GROUP * DEC_SEQ
PAIRS = N_KV_HEADS // 2
PROWS = 2 * QROWS


def _attn_sample_kernel(pt_ref, qbd_ref, knew_ref, vnew_ref, lam_ref, g_ref, *rest):
    del pt_ref
    kp_refs = rest[:PAGES_PER_STEP]
    vp_refs = rest[PAGES_PER_STEP:3 * PAGES_PER_STEP]
    o_ref, m_s, l_s, acc_s = rest[3 * PAGES_PER_STEP:]
    j = pl.program_id(1)

    @pl.when(j == 0)
    def _():
        m_s[...] = jnp.full(m_s.shape, NEG, F32)
        l_s[...] = jnp.zeros(l_s.shape, F32)
        acc_s[...] = jnp.zeros(acc_s.shape, F32)

    def scores(k_tiles):
        cols = []
        for kt in k_tiles:
            rows = [lax.dot_general(qbd_ref[0, hp], kt(hp), _NT, preferred_element_type=F32)
                    for hp in range(PAIRS)]
            cols.append(jnp.concatenate(rows, axis=0))
        return cols[0] if len(cols) == 1 else jnp.concatenate(cols, axis=1)

    def update(s, v_tiles, width):
        m_prev = m_s[...]
        m_new = jnp.maximum(m_prev, jnp.max(s, axis=-1, keepdims=True))
        alpha = jnp.exp2(m_prev - m_new)
        p = jnp.exp2(s - m_new)
        l_s[...] = alpha * l_s[...] + jnp.sum(p, axis=-1, keepdims=True)
        pb = p.astype(BF16)
        pv = []
        for hp in range(PAIRS):
            acc = None
            for i, vt in enumerate(v_tiles):
                d = jnp.dot(pb[hp * PROWS:(hp + 1) * PROWS, i * width:(i + 1) * width], vt(hp),
                            preferred_element_type=F32)
                acc = d if acc is None else acc + d
            pv.append(acc)
        acc_s[...] = alpha * acc_s[...] + jnp.concatenate(pv, axis=0)
        m_s[...] = m_new

    def k_page(ref):
        def get(hp):
            maps = [ref[pl.ds(2 * hp + c, 2 * PAGE_SIZE, stride=N_KV_HEADS), :] for c in range(2)]
            return jnp.concatenate(maps, axis=1).astype(BF16)
        return get

    def v_page(ref_lo, ref_hi):
        def get(hp):
            halves = [r[pl.ds(hp, 2 * PAGE_SIZE, stride=PAIRS), :] for r in (ref_lo, ref_hi)]
            return jnp.concatenate(halves, axis=1).astype(BF16)
        return get

    def which_head(shape):
        return (lax.broadcasted_iota(jnp.int32, shape, 0) >> int(math.log2(QROWS))) & 1

    s = scores([k_page(r) for r in kp_refs])
    col = lax.broadcasted_iota(jnp.int32, s.shape, 1)
    s = jnp.where((col & 1) == which_head(s.shape), s, NEG)
    update(s, [v_page(vp_refs[2 * p], vp_refs[2 * p + 1]) for p in range(PAGES_PER_STEP)], 2 * PAGE_SIZE)

    @pl.when(j == pl.num_programs(1) - 1)
    def _():
        pad = jnp.zeros((PAGE_SIZE - 2 * DEC_SEQ, 2 * HEAD_DIM), F32)

        def new_tile(ref):
            def get(hp):
                parts = [ref[:, h * V_DIM:(h + 1) * V_DIM] for h in (hp, hp + PAIRS)]
                return jnp.concatenate(parts + [pad], axis=0).astype(BF16)
            return get

        s = scores([new_tile(knew_ref)])
        t = lax.broadcasted_iota(jnp.int32, s.shape, 0) & (DEC_SEQ - 1)
        col = lax.broadcasted_iota(jnp.int32, s.shape, 1)
        ok = ((col >> int(math.log2(DEC_SEQ))) == which_head(s.shape)) & ((col & (DEC_SEQ - 1)) <= t)
        update(jnp.where(ok, s, NEG), [new_tile(vnew_ref)], PAGE_SIZE)

        lam = _lambda_value(lam_ref[...])
        o = acc_s[...] / l_s[...]
        half = GROUP * DEC_SEQ
        for hp in range(PAIRS):
            for hi in range(2):
                r0 = hp * PROWS + hi * QROWS
                d = o[r0:r0 + half] - lam * o[r0 + half:r0 + QROWS]
                y = _head_norm(d, g_ref[...])
                for g in range(GROUP):
                    c0 = ((hi * PAIRS + hp) * GROUP + g) * V_DIM
                    o_ref[:, c0:c0 + V_DIM] = y[g * DEC_SEQ:(g + 1) * DEC_SEQ]


def _attn_sample(q_s, k_s, v_s, cache_k, cache_v, page_table, lamv, g_subln):
    q6 = q_s.reshape(DEC_BATCH, DEC_SEQ, N_KV_HEADS, GROUP, 2, HEAD_DIM).transpose(0, 2, 4, 3, 1, 5)
    zero = jnp.zeros_like(q6[:, :, 0])
    top = jnp.concatenate([q6[:, :, 0], zero], axis=-1).reshape(DEC_BATCH, N_KV_HEADS, QROWS // 2, 2 * HEAD_DIM)
    bot = jnp.concatenate([zero, q6[:, :, 1]], axis=-1).reshape(DEC_BATCH, N_KV_HEADS, QROWS // 2, 2 * HEAD_DIM)
    qbd = jnp.concatenate([top, bot], axis=2)
    qbd = qbd.reshape(DEC_BATCH, 2, PAIRS, QROWS, 2 * HEAD_DIM).transpose(0, 2, 1, 3, 4).reshape(
        DEC_BATCH, PAIRS, PROWS, 2 * HEAD_DIM)
    n_pool = cache_k.shape[1]
    ck = cache_k.reshape(n_pool, PAGE_SIZE * N_KV_HEADS * 2, HEAD_DIM)
    cv = cache_v.reshape(n_pool, PAGE_SIZE * N_KV_HEADS, V_DIM)
    pps = PAGES_PER_STEP

    def k_spec(p):
        return pl.BlockSpec((None, ck.shape[1], HEAD_DIM), lambda b, j, pt: (pt[b, j * pps + p], 0, 0))

    def v_spec(p, half):
        return pl.BlockSpec((None, cv.shape[1], HEAD_DIM), lambda b, j, pt: (pt[b, j * pps + p], 0, half))

    in_specs = [
        pl.BlockSpec((1, PAIRS, PROWS, 2 * HEAD_DIM), lambda b, j, pt: (b, 0, 0, 0)),
        pl.BlockSpec((DEC_SEQ, K_W), lambda b, j, pt: (b, 0)),
        pl.BlockSpec((DEC_SEQ, V_W), lambda b, j, pt: (b, 0)),
        pl.BlockSpec((4, HEAD_DIM), lambda b, j, pt: (0, 0)),
        pl.BlockSpec((1, V_DIM), lambda b, j, pt: (0, 0)),
    ] + [k_spec(p) for p in range(pps)] + [v_spec(p, half) for p in range(pps) for half in range(2)]
    rows = N_KV_HEADS * QROWS
    vmem = (2 * 2 * pps * _nbytes((PAGE_SIZE, K_W), F32) + 2 * pps * _nbytes((PAGE_SIZE, K_W), BF16)
            + 6 * _nbytes((rows, pps * 2 * PAGE_SIZE), F32) + 4 * _nbytes((rows, V_DIM), F32)
            + 4 * _nbytes((DEC_SEQ, Q_W), F32))
    return pl.pallas_call(
        _attn_sample_kernel,
        grid_spec=pltpu.PrefetchScalarGridSpec(
            num_scalar_prefetch=1,
            grid=(DEC_BATCH, N_PAGES // pps),
            in_specs=in_specs,
            out_specs=pl.BlockSpec((DEC_SEQ, N_HEADS * V_DIM), lambda b, j, pt: (b, 0)),
            scratch_shapes=[pltpu.VMEM((rows, 1), F32), pltpu.VMEM((rows, 1), F32),
                            pltpu.VMEM((rows, V_DIM), F32)],
        ),
        out_shape=jax.ShapeDtypeStruct((ROWS_S, N_HEADS * V_DIM), F32),
        compiler_params=_params(("arbitrary", "arbitrary"), vmem),
        name="attn_sample",
    )(page_table, qbd, k_s, v_s, lamv, g_subln, *([ck] * pps), *([cv] * (2 * pps)))


def _sgu_kernel(u_ref, v_ref, w_ref, b_ref, g_ref, yb_ref, vn_ref, *, t, last_only):
    v = v_ref[...]
    ms = jnp.mean(v * v, axis=-1, keepdims=True)
    vn = v * lax.rsqrt(ms + EPS) * g_ref[...]
    n_out = vn_ref.shape[0]
    if last_only:
        @pl.when(pl.program_id(1) == pl.num_programs(1) - 1)
        def _():
            vn_ref[...] = vn[v.shape[0] - n_out:]
    else:
        vn_ref[...] = vn
    r = w_ref.shape[1]
    row = lax.broadcasted_iota(jnp.int32, (r, r), 0)
    col = lax.broadcasted_iota(jnp.int32, (r, r), 1)
    keep = (col <= row) & (col >= row - (row & (t - 1)))
    for g in range(N_SGU_GROUPS):
        cs = slice(g * SGU_GROUP_W, (g + 1) * SGU_GROUP_W)
        w = jnp.where(keep, w_ref[g], 0.0).astype(BF16)
        for r0 in range(0, v.shape[0], r):
            rs = slice(r0, r0 + r)
            mixed = jnp.dot(w, vn[rs, cs].astype(BF16), preferred_element_type=F32) + b_ref[g]
            yb_ref[rs, cs] = (u_ref[rs, cs] * mixed).astype(yb_ref.dtype)


def _sgu_prompt(uv, w_spatial, b_spatial, g_sgu):
    rows = 2 * CHUNK
    nch = SEQ // rows
    vmem = 2 * (3 * _nbytes((rows, SGU_W), F32) + _nbytes((rows, SGU_W), BF16)) + 6 * _nbytes((rows, SGU_W), F32)
    return pl.pallas_call(
        functools.partial(_sgu_kernel, t=CHUNK, last_only=True),
        grid=(BATCH, nch),
        in_specs=[
            pl.BlockSpec((rows, SGU_W), lambda b, c: (b * nch + c, 0)),
            pl.BlockSpec((rows, SGU_W), lambda b, c: (b * nch + c, 1)),
            pl.BlockSpec((N_SGU_GROUPS, CHUNK, CHUNK), lambda b, c: (0, 0, 0)),
            pl.BlockSpec((N_SGU_GROUPS, CHUNK, 1), lambda b, c: (0, 0, 0)),
            pl.BlockSpec((1, SGU_W), lambda b, c: (0, 0)),
        ],
        out_specs=[pl.BlockSpec((rows, SGU_W), lambda b, c: (b * nch + c, 0)),
                   pl.BlockSpec((None, CHUNK, SGU_W), lambda b, c: (b, 0, 0))],
        out_shape=[jax.ShapeDtypeStruct((ROWS_P, SGU_W), BF16),
                   jax.ShapeDtypeStruct((BATCH, CHUNK, SGU_W), F32)],
        compiler_params=_params(("arbitrary", "arbitrary"), vmem),
        name="sgu_prompt",
    )(uv, uv, w_spatial, b_spatial[:, :CHUNK, None], g_sgu.reshape(1, SGU_W))


def _sgu_sample(uv, w_spatial, b_spatial, g_sgu):
    reps = ROWS_S // DEC_SEQ
    w = jnp.concatenate([w_spatial[:, :DEC_SEQ, :DEC_SEQ]] * reps, axis=2)
    w = jnp.broadcast_to(w[:, None], (N_SGU_GROUPS, reps, DEC_SEQ, ROWS_S)).reshape(N_SGU_GROUPS, ROWS_S, ROWS_S)
    b = jnp.tile(b_spatial[:, :DEC_SEQ], (1, reps))[:, :, None]
    vmem = 2 * (4 * _nbytes((ROWS_S, SGU_W), F32)) + 6 * _nbytes((ROWS_S, SGU_W), F32)
    return pl.pallas_call(
        functools.partial(_sgu_kernel, t=DEC_SEQ, last_only=False),
        grid=(1, 1),
        in_specs=[
            pl.BlockSpec((ROWS_S, SGU_W), lambda b, c: (0, 0)),
            pl.BlockSpec((ROWS_S, SGU_W), lambda b, c: (0, 1)),
            pl.BlockSpec((N_SGU_GROUPS, ROWS_S, ROWS_S), lambda b, c: (0, 0, 0)),
            pl.BlockSpec((N_SGU_GROUPS, ROWS_S, 1), lambda b, c: (0, 0, 0)),
            pl.BlockSpec((1, SGU_W), lambda b, c: (0, 0)),
        ],
        out_specs=[pl.BlockSpec((ROWS_S, SGU_W), lambda b, c: (0, 0)),
                   pl.BlockSpec((ROWS_S, SGU_W), lambda b, c: (0, 0))],
        out_shape=[jax.ShapeDtypeStruct((ROWS_S, SGU_W), BF16),
                   jax.ShapeDtypeStruct((ROWS_S, SGU_W), F32)],
        compiler_params=_params(("arbitrary", "arbitrary"), vmem),
        name="sgu_sample",
    )(uv, uv, w, b, g_sgu.reshape(1, SGU_W))


TAIL = 8


def _ffn_kernel(xp_ref, xs_ref, wg_ref, wu_ref, cw_ref, cb_ref, p1_ref, p2_ref,
                midp_ref, mids_ref, tail_ref, as_ref, carry):
    m, n = pl.program_id(0), pl.program_id(1)
    tiles_per_seq = SEQ // BM

    def conv_silu(a, sh1, sh2):
        cw = cw_ref[...]
        c = cb_ref[...] + ((cw[0:1] * sh2 + cw[1:2] * sh1) + cw[2:3] * a)
        return jax.nn.silu(c)

    @pl.when(m % tiles_per_seq == 0)
    def _():
        carry[n] = jnp.zeros(carry.shape[1:], F32)

    x = xp_ref[...]
    a = jnp.dot(x, wg_ref[...], preferred_element_type=F32)
    prev = carry[n]
    row = lax.broadcasted_iota(jnp.int32, a.shape, 0)
    sh1 = jnp.where(row == 0, prev[TAIL - 1:TAIL], pltpu.roll(a, 1, 0))
    sh2 = jnp.where(row == 0, prev[TAIL - 2:TAIL - 1],
                    jnp.where(row == 1, prev[TAIL - 1:TAIL], pltpu.roll(a, 2, 0)))
    gate = conv_silu(a, sh1, sh2)
    up = jnp.dot(x, wu_ref[...], preferred_element_type=F32)
    midp_ref[...] = (gate * up).astype(midp_ref.dtype)
    carry[n] = a[BM - TAIL:]

    @pl.when(m % tiles_per_seq == tiles_per_seq - 1)
    def _():
        tail_ref[...] = a[BM - TAIL:]

    @pl.when(m == N_MP - 1)
    def _():
        xs = xs_ref[...]
        a_s = jnp.dot(xs, wg_ref[...], preferred_element_type=F32)
        up_s = jnp.dot(xs, wu_ref[...], preferred_element_type=F32)
        t = lax.broadcasted_iota(jnp.int32, a_s.shape, 0) & (DEC_SEQ - 1)
        s1 = jnp.where(t == 0, p1_ref[...], pltpu.roll(a_s, 1, 0))
        s2 = jnp.where(t < 2, p2_ref[...], pltpu.roll(a_s, 2, 0))
        mids_ref[...] = (conv_silu(a_s, s1, s2) * up_s).astype(mids_ref.dtype)
        as_ref[...] = a_s


def _ffn_gate_up(hp, hs, wg, wu, conv_w, conv_b, state):
    bn = 512
    n_tiles = pl.cdiv(D_FF, bn)
    prev1 = jnp.pad(state[:, 1:2], ((0, 0), (0, DEC_SEQ - 1), (0, 0))).reshape(ROWS_S, D_FF)
    prev2 = jnp.pad(state, ((0, 0), (0, DEC_SEQ - 2), (0, 0))).reshape(ROWS_S, D_FF)
    tiles_per_seq = SEQ // BM
    last = lambda m, n: jnp.where(m == N_MP - 1, n, 0)
    seq_end = lambda m, n: jnp.where(m % tiles_per_seq == tiles_per_seq - 1, n, 0)
    vmem = (2 * _nbytes((BM + ROWS_S, D_MODEL), BF16) + 2 * 2 * _nbytes((D_MODEL, bn), BF16)
            + 2 * 4 * _nbytes((BM + ROWS_S, bn), F32) + 12 * _nbytes((BM, bn), F32)
            + _nbytes((n_tiles, TAIL, bn), F32))
    return pl.pallas_call(
        _ffn_kernel,
        grid=(N_MP, n_tiles),
        in_specs=[
            pl.BlockSpec((BM, D_MODEL), lambda m, n: (m, 0)),
            pl.BlockSpec((ROWS_S, D_MODEL), lambda m, n: (0, 0)),
            pl.BlockSpec((D_MODEL, bn), lambda m, n: (0, n)),
            pl.BlockSpec((D_MODEL, bn), lambda m, n: (0, n)),
            pl.BlockSpec((CONV_W, bn), lambda m, n: (0, n)),
            pl.BlockSpec((1, bn), lambda m, n: (0, n)),
            pl.BlockSpec((ROWS_S, bn), lambda m, n: (0, last(m, n))),
            pl.BlockSpec((ROWS_S, bn), lambda m, n: (0, last(m, n))),
        ],
        out_specs=[
            pl.BlockSpec((BM, bn), lambda m, n: (m, n)),
            pl.BlockSpec((ROWS_S, bn), lambda m, n: (0, last(m, n))),
            pl.BlockSpec((None, TAIL, bn), lambda m, n: (m // tiles_per_seq, 0, seq_end(m, n))),
            pl.BlockSpec((ROWS_S, bn), lambda m, n: (0, last(m, n))),
        ],
        out_shape=[
            jax.ShapeDtypeStruct((ROWS_P, D_FF), BF16),
            jax.ShapeDtypeStruct((ROWS_S, D_FF), BF16),
            jax.ShapeDtypeStruct((BATCH, TAIL, D_FF), F32),
            jax.ShapeDtypeStruct((ROWS_S, D_FF), F32),
        ],
        scratch_shapes=[pltpu.VMEM((n_tiles, TAIL, bn), F32)],
        compiler_params=_params(("arbitrary", "arbitrary"), vmem),
        name="ffn_gate_up",
    )(hp, hs, wg, wu, conv_w, conv_b.reshape(1, D_FF), prev1, prev2)


def _cast_kernel(x_ref, o_ref):
    o_ref[...] = x_ref[...].astype(o_ref.dtype)


def _cast_bf16(w, n, name):
    k = w.shape[0]
    rows = 256
    assert k % rows == 0
    vmem = 2 * (_nbytes((rows, n), F32) + _nbytes((rows, n), BF16)) + _nbytes((rows, n), F32)
    return pl.pallas_call(
        _cast_kernel,
        grid=(k // rows,),
        in_specs=[pl.BlockSpec((rows, n), lambda i: (i, 0))],
        out_specs=pl.BlockSpec((rows, n), lambda i: (i, 0)),
        out_shape=jax.ShapeDtypeStruct((k, n), BF16),
        compiler_params=_params(("arbitrary",), vmem),
        name=name,
    )(w)


def _down_kernel(x_ref, w_ref, r_ref, o_ref):
    o_ref[...] = r_ref[...] + jnp.dot(x_ref[...], w_ref[...], preferred_element_type=F32)


def _ffn_down(mid, wb, res, bm, name):
    rows = mid.shape[0]
    bn = 512
    vmem = (2 * (_nbytes((bm, D_FF), BF16) + _nbytes((D_FF, bn), BF16)) + 4 * _nbytes((bm, bn), F32)
            + 2 * _nbytes((bm, bn), F32))
    return pl.pallas_call(
        _down_kernel,
        grid=(rows // bm, D_MODEL // bn),
        in_specs=[
            pl.BlockSpec((bm, D_FF), lambda m, n: (m, 0)),
            pl.BlockSpec((D_FF, bn), lambda m, n: (0, n)),
            pl.BlockSpec((bm, bn), lambda m, n: (m, n)),
        ],
        out_specs=pl.BlockSpec((bm, bn), lambda m, n: (m, n)),
        out_shape=jax.ShapeDtypeStruct((rows, D_MODEL), F32),
        compiler_params=_params(("arbitrary", "arbitrary"), vmem),
        name=name,
    )(mid, wb, res)


def kernel(x_prompt, x_sample, cache_k, cache_v, state_conv, page_table, g_attn, w_in, lam_q1, lam_k1, lam_q2, lam_k2, g_subln, g_sgu, w_spatial, b_spatial, w_o_a, w_o_b, w_out, g_ffn, w_gate, conv_w, conv_b, w_up, w_down, g_final):
    xp = x_prompt.reshape(ROWS_P, D_MODEL)
    xs = x_sample.reshape(ROWS_S, D_MODEL)
    lamv = jnp.stack([lam_q1[0], lam_k1[0], lam_q2[0], lam_k2[0]])
    gsub = g_subln[0].reshape(1, V_DIM)

    hp = _rmsnorm_rows(xp, g_attn[0], BF16, "rms_attn_p")
    hs = _rmsnorm_rows(xs, g_attn[0], BF16, "rms_attn_s")

    bn = 512
    one = lambda f: (lambda accs, ex: [f(accs[0])])
    col = lambda c: c // bn
    whole = lambda w, n_chunks: (w, n_chunks, w.shape[1], 0)
    w_in0 = w_in[0]
    w_qb = _cast_bf16(w_in0, Q_W, "cast_w_q")
    (q_p,), (q_s,), (w_kvb, w_uvb) = _stream_matmul(
        hp, hs, w_qb, 0, Q_W, bn, one(lambda a: a * (SCALE * LOG2E)), [BF16],
        cast=[(w_in0, 64, K_W + V_W, 1), (w_in0, 64, 2 * SGU_W, 1)], name="proj_q")
    (k_p,), (k_s,), (wobb,) = _stream_matmul(hp, hs, w_kvb, 0, K_W, bn, one(lambda a: a), [F32],
                                             cast=[whole(w_o_b[0], 32)], name="proj_k")
    (v_p,), (v_s,), (woutb, woab) = _stream_matmul(hp, hs, w_kvb, col(K_W), V_W, bn, one(lambda a: a), [F32],
                                                   cast=[whole(w_out[0], 32), whole(w_o_a[0], 32)], name="proj_v")
    (uv_p,), (uv_s,), (w_gtb, wgb) = _stream_matmul(
        hp, hs, w_uvb, 0, 2 * SGU_W, bn, one(_gelu_exact), [F32],
        cast=[(w_in0, 128, 2 * D_MODEL, 2), whole(w_gate[0], 128)], name="proj_uv")
    (gt_p,), (gt_s,), (wub, wdb) = _stream_matmul(
        hp, hs, w_gtb, 0, 2 * D_MODEL, bn, one(jax.nn.sigmoid), [F32],
        cast=[whole(w_up[0], 128), whole(w_down[0], 86)], name="proj_gates")

    ya_p = _attn_prompt(q_p, k_p, v_p, lamv, gsub)
    ya_s = _attn_sample(q_s, k_s, v_s, cache_k, cache_v, page_table, lamv, gsub)

    yb_p, vn_p = _sgu_prompt(uv_p, w_spatial[0], b_spatial[0], g_sgu[0])
    yb_s, vn_s = _sgu_sample(uv_s, w_spatial[0], b_spatial[0], g_sgu[0])

    (ta_p,), (ta_s,), _ = _stream_matmul(ya_p, ya_s, woab, 0, D_MODEL, bn,
                                         lambda accs, ex: [ex[0] * accs[0]], [F32],
                                         extras=[(gt_p, gt_s)], ex_off=[0], name="merge_a")
    (z_p,), (z_s,), _ = _stream_matmul(yb_p, yb_s, wobb, 0, D_MODEL, bn,
                                       lambda accs, ex: [ex[1] + ex[0] * accs[0]], [BF16],
                                       extras=[(gt_p, gt_s), (ta_p, ta_s)], ex_off=[col(D_MODEL), 0], name="merge_b")
    (x1_p,), (x1_s,), _ = _stream_matmul(z_p, z_s, woutb, 0, D_MODEL, bn,
                                         lambda accs, ex: [ex[0] + accs[0]], [F32],
                                         extras=[(xp, xs)], ex_off=[0], name="out_proj")

    h2_p = _rmsnorm_rows(x1_p, g_ffn[0], BF16, "rms_ffn_p")
    h2_s = _rmsnorm_rows(x1_s, g_ffn[0], BF16, "rms_ffn_s")
    mid_p, mid_s, tail_p, a_s = _ffn_gate_up(h2_p, h2_s, wgb, wub, conv_w[0], conv_b[0], state_conv[0])
    x2_p = _ffn_down(mid_p, wdb, x1_p, 512, "ffn_down_p")
    x2_s = _ffn_down(mid_s, wdb, x1_s, ROWS_S, "ffn_down_s")

    y_p = _rmsnorm_rows(x2_p, g_final, F32, "rms_final_p")
    y_s = _rmsnorm_rows(x2_s, g_final, F32, "rms_final_s")

    return (
        y_p.reshape(BATCH, SEQ, D_MODEL),
        y_s.reshape(DEC_BATCH, DEC_SEQ, D_MODEL),
        k_p.reshape(1, BATCH, SEQ, N_KV_HEADS, 2, HEAD_DIM),
        v_p.reshape(1, BATCH, SEQ, N_KV_HEADS, V_DIM),
        k_s.reshape(1, DEC_BATCH, DEC_SEQ, N_KV_HEADS, 2, HEAD_DIM),
        v_s.reshape(1, DEC_BATCH, DEC_SEQ, N_KV_HEADS, V_DIM),
        vn_p.reshape(1, BATCH, CHUNK, SGU_W),
        vn_s.reshape(1, DEC_BATCH, DEC_SEQ, SGU_W),
        tail_p[:, TAIL - (CONV_W - 1):].reshape(1, BATCH, CONV_W - 1, D_FF),
        a_s.reshape(DEC_BATCH, DEC_SEQ, D_FF)[:, DEC_SEQ - (CONV_W - 1):].reshape(1, DEC_BATCH, CONV_W - 1, D_FF),
    )
```
